```python
import jax, jax.numpy as jnp
from jax import lax
import numpy as np

D_MODEL = 2048
BATCH = 4
SEQ = 2048
DEPTH = 2
DEC_BATCH = 32
DEC_SEQ = 1
PAST_LEN = 16384
PAGE_SIZE = 128

N_BRANCH = 4
BRANCH_W = D_MODEL // 4
CONV_W = 3
HEAD_DIM = 64
N_HEADS = BRANCH_W // HEAD_DIM
N_KV = 2
GQA_G = N_HEADS // N_KV
WINDOW = 128
ROPE_THETA = 500000.0
ROPE_DIM = HEAD_DIM // 4
CHUNK = 128
SGU_GROUPS = 4
SGU_CG = BRANCH_W // SGU_GROUPS
POOL_WINDOWS = (2, 4, 8, 16)
POOL_GROUPS = len(POOL_WINDOWS)
POOL_CG = BRANCH_W // POOL_GROUPS
POOL_MAX = max(POOL_WINDOWS)
N_EXP_GROUPS = 4
EXP_PER_GROUP = 8
N_EXPERTS = N_EXP_GROUPS * EXP_PER_GROUP
TOP_K = 2
EXPERT_FF = D_MODEL // 4
EPS = 1e-6

B_Q = N_HEADS * HEAD_DIM
B_KV = N_KV * HEAD_DIM
A_H_OFF = 0
A_C_OFF = A_H_OFF + BRANCH_W
A_B_OFF = A_C_OFF + BRANCH_W
Q_OFF = A_B_OFF + BRANCH_W
K_OFF = Q_OFF + B_Q
V_OFF = K_OFF + B_KV
C_OFF = V_OFF + B_KV
D_OFF = C_OFF + 2 * BRANCH_W
G_OFF = D_OFF + BRANCH_W
IN_COLS = G_OFF + N_BRANCH * D_MODEL

kernel_name = "hybrid_gated_branch_decoder_step"


def rmsnorm(x, g):
    x32 = x.astype(jnp.float32)
    y = x32 * lax.rsqrt(jnp.mean(x32 * x32, axis=-1, keepdims=True) + EPS)
    return y.astype(x.dtype) * g


def layernorm(x, g, b):
    x32 = x.astype(jnp.float32)
    mu = jnp.mean(x32, axis=-1, keepdims=True)
    xc = x32 - mu
    y = xc * lax.rsqrt(jnp.mean(xc * xc, axis=-1, keepdims=True) + EPS)
    return y.astype(x.dtype) * g + b


def partial_rope(x, pos):
    half = ROPE_DIM // 2
    inv = jnp.power(jnp.float32(ROPE_THETA), -jnp.arange(half, dtype=jnp.float32) * (2.0 / ROPE_DIM))
    ang = pos.astype(jnp.float32)[:, None] * inv[None, :]
    cos = jnp.cos(ang)[None, :, None, :]
    sin = jnp.sin(ang)[None, :, None, :]
    x1 = x[..., :half].astype(jnp.float32)
    x2 = x[..., half:ROPE_DIM].astype(jnp.float32)
    r = jnp.concatenate([x1 * cos - x2 * sin, x2 * cos + x1 * sin], axis=-1).astype(x.dtype)
    return jnp.concatenate([r, x[..., ROPE_DIM:]], axis=-1)


def sink_attention(qb, kk, vv, qpos, kpos, sinks):
    s = jnp.einsum('bnqkgd,bnskd->bnkgqs', qb, kk).astype(jnp.float32) * (HEAD_DIM ** -0.5)
    dq = qpos[:, :, None] - kpos[:, None, :]
    allowed = (dq >= 0) & (dq < WINDOW) & (kpos[:, None, :] >= 0)
    s = jnp.where(allowed[None, :, None, None], s, -1e30)
    sk = sinks.astype(jnp.float32).reshape(1, 1, N_KV, GQA_G, 1, 1)
    m = jnp.maximum(jnp.max(s, axis=-1, keepdims=True), sk)
    p = jnp.exp(s - m)
    probs = p / (jnp.sum(p, axis=-1, keepdims=True) + jnp.exp(sk - m))
    return jnp.einsum('bnkgqs,bnskd->bnqkgd', probs.astype(vv.dtype), vv)


def short_conv(h, cg, bg, conv_w, past):
    T = h.shape[1]
    ci = cg * h
    xp = jnp.concatenate([past.astype(ci.dtype), ci], axis=1)
    y = conv_w[0] * xp[:, 0:T]
    for j in range(1, CONV_W):
        y = y + conv_w[j] * xp[:, j:j + T]
    return bg * y, xp[:, -(CONV_W - 1):]


def spatial_gating(uv, ln_g, ln_b, w_s, b_s):
    u, v = jnp.split(jax.nn.gelu(uv, approximate=False), 2, axis=-1)
    v = layernorm(v, ln_g, ln_b)
    B, T, _ = v.shape
    L = min(T, CHUNK)
    vc = v.reshape(B, T // L, L, SGU_GROUPS, SGU_CG)
    w = jnp.tril(w_s[:, :L, :L])
    f = jnp.einsum('gts,bnsgc->bntgc', w, vc) + b_s[:, :L].T[None, None, :, :, None]
    return u * f.reshape(B, T, BRANCH_W), v


def multi_pool(p, past, pos0, pool_w, pool_scale):
    B, T, _ = p.shape
    xp = jnp.concatenate([past.astype(p.dtype), p], axis=1).astype(jnp.float32)
    cs = jnp.concatenate([jnp.zeros((B, 1, BRANCH_W), jnp.float32), jnp.cumsum(xp, axis=1)], axis=1)
    pos = (pos0 + jnp.arange(T)).astype(jnp.float32)
    means = []
    for g, w in enumerate(POOL_WINDOWS):
        sl = slice(g * POOL_CG, (g + 1) * POOL_CG)
        s = cs[:, POOL_MAX:POOL_MAX + T, sl] - cs[:, POOL_MAX - w:POOL_MAX - w + T, sl]
        cnt = jnp.minimum(pos + 1.0, float(w))
        means.append(s / cnt[None, :, None])
    d = (jnp.concatenate(means, axis=-1) - p.astype(jnp.float32)).astype(p.dtype)
    d = d.reshape(B, T, POOL_GROUPS, POOL_CG)
    y = jnp.einsum('btgc,gce->btge', d, pool_w).reshape(B, T, BRANCH_W) * pool_scale
    return y, xp[:, -(POOL_MAX - 1):].astype(p.dtype)


def mixer_sublayer(x, pos0, conv_past, k_past, v_past, pool_past, lp):
    B, T, _ = x.shape
    xn = rmsnorm(x, lp['norm1_g'])
    z = xn @ lp['w_in']
    a_out, conv_state = short_conv(z[..., A_H_OFF:A_C_OFF], z[..., A_C_OFF:A_B_OFF],
                                   z[..., A_B_OFF:Q_OFF], lp['conv_w'], conv_past)
    pos = pos0 + jnp.arange(T)
    q = partial_rope(z[..., Q_OFF:K_OFF].reshape(B, T, N_HEADS, HEAD_DIM), pos)
    k = partial_rope(z[..., K_OFF:V_OFF].reshape(B, T, N_KV, HEAD_DIM), pos)
    v = z[..., V_OFF:C_OFF].reshape(B, T, N_KV, HEAD_DIM)
    if k_past is None:
        nb = T // WINDOW
        qb = q.reshape(B, nb, WINDOW, N_KV, GQA_G, HEAD_DIM)
        kb = k.reshape(B, nb, WINDOW, N_KV, HEAD_DIM)
        vb = v.reshape(B, nb, WINDOW, N_KV, HEAD_DIM)
        prev = lambda a: jnp.concatenate([jnp.zeros_like(a[:, :1]), a[:, :-1]], axis=1)
        kk = jnp.concatenate([prev(kb), kb], axis=2)
        vv = jnp.concatenate([prev(vb), vb], axis=2)
        qpos = jnp.arange(T).reshape(nb, WINDOW)
        kpos = jnp.concatenate([qpos - WINDOW, qpos], axis=1)
        o = sink_attention(qb, kk, vv, qpos, kpos, lp['attn_sinks'])
        k_state, v_state = k[:, -WINDOW:], v[:, -WINDOW:]
    else:
        wb = k_past.shape[1]
        kk = jnp.concatenate([k_past.astype(k.dtype), k], axis=1)
        vv = jnp.concatenate([v_past.astype(v.dtype), v], axis=1)
        qpos = (pos0 + jnp.arange(T))[None]
        kpos = (pos0 - wb + jnp.arange(wb + T))[None]
        o = sink_attention(q.reshape(B, 1, T, N_KV, GQA_G, HEAD_DIM), kk[:, None], vv[:, None],
                           qpos, kpos, lp['attn_sinks'])
        k_state, v_state = kk[:, -wb:], vv[:, -wb:]
    b_out = o.reshape(B, T, B_Q)
    c_out, v_rows = spatial_gating(z[..., C_OFF:D_OFF], lp['sgu_ln_g'], lp['sgu_ln_b'],
                                   lp['sgu_w'], lp['sgu_b'])
    d_out, pool_state = multi_pool(z[..., D_OFF:G_OFF], pool_past, pos0, lp['pool_w'], lp['pool_scale'])
    outs = jnp.stack([a_out, b_out, c_out, d_out], axis=2)
    proj = jnp.einsum('btnc,ncd->btnd', outs, lp['w_branch'])
    gates = jax.nn.sigmoid(z[..., G_OFF:].reshape(B, T, N_BRANCH, D_MODEL))
    merged = jnp.sum(gates * proj, axis=2)
    return x + merged @ lp['w_o'], (conv_state, k_state, v_state, pool_state, v_rows)


def moe_sublayer(x, lp):
    B, T, D = x.shape
    xn = rmsnorm(x, lp['norm2_g']).reshape(B * T, D)
    lg = (xn @ lp['router_group_w']).astype(jnp.float32) + lp['router_group_b']
    pg = jax.nn.softmax(lg, axis=-1)
    gsel = jnp.argmax(lg, axis=-1)
    wg = jnp.take_along_axis(pg, gsel[:, None], axis=1)
    le = ((xn @ lp['router_expert_w']).astype(jnp.float32) + lp['router_expert_b'])
    le = le.reshape(-1, N_EXP_GROUPS, EXP_PER_GROUP)
    le = jnp.take_along_axis(le, gsel[:, None, None], axis=1)[:, 0]
    tv, ti = lax.top_k(le, TOP_K)
    we = jax.nn.softmax(tv, axis=-1) * wg
    eid = gsel[:, None] * EXP_PER_GROUP + ti
    comb = jnp.einsum('nk,nke->ne', we, jax.nn.one_hot(eid, N_EXPERTS, dtype=jnp.float32))
    h = jax.nn.silu(jnp.einsum('nd,edf->nef', xn, lp['moe_w_gate'])) * jnp.einsum('nd,edf->nef', xn, lp['moe_w_up'])
    h = h * comb.astype(h.dtype)[:, :, None]
    y = jnp.einsum('nef,efd->nd', h, lp['moe_w_down'])
    return x + y.reshape(B, T, D)


def setup_inputs(seed: int = 0) -> dict:
    key = jax.random.key(seed)
    ks = jax.random.split(key, 32)
    f32 = jnp.float32
    nrm = lambda k, shape, scale: jax.random.normal(k, shape, f32) * scale
    win_buf = min(WINDOW, PAST_LEN)
    return {
        "x_prompt": nrm(ks[0], (BATCH, SEQ, D_MODEL), 1.0),
        "x_sample": nrm(ks[1], (DEC_BATCH, DEC_SEQ, D_MODEL), 1.0),
        "state_conv": nrm(ks[2], (DEPTH, DEC_BATCH, CONV_W - 1, BRANCH_W), 1.0),
        "cache_win_k": nrm(ks[3], (DEPTH, DEC_BATCH, win_buf, N_KV, HEAD_DIM), 1.0),
        "cache_win_v": nrm(ks[4], (DEPTH, DEC_BATCH, win_buf, N_KV, HEAD_DIM), 1.0),
        "state_pool": nrm(ks[5], (DEPTH, DEC_BATCH, POOL_MAX - 1, BRANCH_W), 1.0),
        "norm1_g": 1.0 + nrm(ks[6], (DEPTH, D_MODEL), 0.02),
        "w_in": nrm(ks[7], (DEPTH, D_MODEL, IN_COLS), D_MODEL ** -0.5),
        "conv_w": nrm(ks[8], (DEPTH, CONV_W, BRANCH_W), CONV_W ** -0.5),
        "attn_sinks": nrm(ks[9], (DEPTH, N_HEADS), 1.0),
        "sgu_ln_g": 1.0 + nrm(ks[10], (DEPTH, BRANCH_W), 0.02),
        "sgu_ln_b": nrm(ks[11], (DEPTH, BRANCH_W), 0.01),
        "sgu_w": nrm(ks[12], (DEPTH, SGU_GROUPS, CHUNK, CHUNK), CHUNK ** -0.5),
        "sgu_b": 1.0 + nrm(ks[13], (DEPTH, SGU_GROUPS, CHUNK), 0.1),
        "pool_w": nrm(ks[14], (DEPTH, POOL_GROUPS, POOL_CG, POOL_CG), POOL_CG ** -0.5),
        "pool_scale": 1.0 + nrm(ks[15], (DEPTH, BRANCH_W), 0.1),
        "w_branch": nrm(ks[16], (DEPTH, N_BRANCH, BRANCH_W, D_MODEL), BRANCH_W ** -0.5),
        "w_o": nrm(ks[17], (DEPTH, D_MODEL, D_MODEL), D_MODEL ** -0.5),
        "norm2_g": 1.0 + nrm(ks[18], (DEPTH, D_MODEL), 0.02),
        "router_group_w": nrm(ks[19], (DEPTH, D_MODEL, N_EXP_GROUPS), D_MODEL ** -0.5),
        "router_group_b": nrm(ks[20], (DEPTH, N_EXP_GROUPS), 0.01),
        "router_expert_w": nrm(ks[21], (DEPTH, D_MODEL, N_EXPERTS), D_MODEL ** -0.5),
        "router_expert_b": nrm(ks[22], (DEPTH, N_EXPERTS), 0.01),
        "moe_w_gate": nrm(ks[23], (DEPTH, N_EXPERTS, D_MODEL, EXPERT_FF), D_MODEL ** -0.5),
        "moe_w_up": nrm(ks[24], (DEPTH, N_EXPERTS, D_MODEL, EXPERT_FF), D_MODEL ** -0.5),
        "moe_w_down": nrm(ks[25], (DEPTH, N_EXPERTS, EXPERT_FF, D_MODEL), EXPERT_FF ** -0.5),
        "final_norm_g": 1.0 + nrm(ks[26], (D_MODEL,), 0.02),
    }


def reference(x_prompt, x_sample, state_conv, cache_win_k, cache_win_v, state_pool,
              norm1_g, w_in, conv_w, attn_sinks, sgu_ln_g, sgu_ln_b, sgu_w, sgu_b,
              pool_w, pool_scale, w_branch, w_o, norm2_g, router_group_w, router_group_b,
              router_expert_w, router_expert_b, moe_w_gate, moe_w_up, moe_w_down, final_norm_g):
    yp, ys = x_prompt, x_sample
    bp = x_prompt.shape[0]
    conv_p, kp, vp, pool_p = [], [], [], []
    conv_s, ksl, vsl, pool_s, chunk_s = [], [], [], [], []
    for l in range(DEPTH):
        lp = {
            'norm1_g': norm1_g[l], 'w_in': w_in[l], 'conv_w': conv_w[l], 'attn_sinks': attn_sinks[l],
            'sgu_ln_g': sgu_ln_g[l], 'sgu_ln_b': sgu_ln_b[l], 'sgu_w': sgu_w[l], 'sgu_b': sgu_b[l],
            'pool_w': pool_w[l], 'pool_scale': pool_scale[l], 'w_branch': w_branch[l], 'w_o': w_o[l],
            'norm2_g': norm2_g[l], 'router_group_w': router_group_w[l], 'router_group_b': router_group_b[l],
            'router_expert_w': router_expert_w[l], 'router_expert_b': router_expert_b[l],
            'moe_w_gate': moe_w_gate[l], 'moe_w_up': moe_w_up[l], 'moe_w_down': moe_w_down[l],
        }
        yp, (c_st, k_st, v_st, p_st, _) = mixer_sublayer(
            yp, 0, jnp.zeros((bp, CONV_W - 1, BRANCH_W), yp.dtype), None, None,
            jnp.zeros((bp, POOL_MAX - 1, BRANCH_W), yp.dtype), lp)
        yp = moe_sublayer(yp, lp)
        conv_p.append(c_st); kp.append(k_st); vp.append(v_st); pool_p.append(p_st)
        ys, (c_st, k_st, v_st, p_st, v_rows) = mixer_sublayer(
            ys, PAST_LEN, state_conv[l], cache_win_k[l], cache_win_v[l], state_pool[l], lp)
        ys = moe_sublayer(ys, lp)
        conv_s.append(c_st); ksl.append(k_st); vsl.append(v_st); pool_s.append(p_st); chunk_s.append(v_rows)
    y_prompt = rmsnorm(yp, final_norm_g)
    y_sample = rmsnorm(ys, final_norm_g)
    return (y_prompt, y_sample,
            jnp.stack(conv_p), jnp.stack(kp), jnp.stack(vp), jnp.stack(pool_p),
            jnp.stack(conv_s), jnp.stack(ksl), jnp.stack(vsl), jnp.stack(pool_s), jnp.stack(chunk_s))
```

```python
import functools

import jax
import jax.numpy as jnp
from jax import lax
from jax.experimental import pallas as pl
from jax.experimental.pallas import tpu as pltpu

F32 = jnp.float32
BF16 = jnp.bfloat16

D_MODEL = 2048
BRANCH_W = 512
N_BRANCH = 4
CONV_W = 3
HEAD_DIM = 64
N_HEADS = 8
N_KV = 2
GQA_G = 4
WINDOW = 128
ROPE_THETA = 500000.0
ROPE_DIM = 16
CHUNK = 128
SGU_GROUPS = 4
POOL_WINDOWS = (2, 4, 8, 16)
POOL_MAX = 16
N_EXP_GROUPS = 4
EXP_PER_GROUP = 8
N_EXPERTS = 32
EXPERT_FF = 512
EPS = 1e-6
PAST_LEN = 16384
NEG = -3.0e38

A_H_OFF = 0
A_C_OFF = 512
A_B_OFF = 1024
Q_OFF = 1536
K_OFF = 2048
V_OFF = 2176
C_OFF = 2304
D_OFF = 3328
G_OFF = 3840

LANES = 128
SUBLANES = 8
VMEM_LIMIT = 52 * 1024 * 1024

TM_NORM = 512
TM_PROJ = 1024
TN_PROJ = 768
TM_MERGE = 2048
TN_MERGE = 256
TM_WO = 1024
TN_WO = 512
TM_MOE = 256
TB = 128
ROUTER_LANES = 128


def _cparams(sem):
    return pltpu.CompilerParams(dimension_semantics=sem, vmem_limit_bytes=VMEM_LIMIT)


_NN = (((1,), (0,)), ((), ()))
_NT = (((1,), (1,)), ((), ()))


def _dot(a, b, dims=_NN):
    return lax.dot_general(a, b, dims, preferred_element_type=F32)


def _split(x):
    hi = x.astype(BF16)
    lo = (x - hi.astype(F32)).astype(BF16)
    return hi, lo


def _dot3(a, b, dims=_NN):
    ah, al = _split(a)
    bh, bl = _split(b)
    return _dot(ah, bh, dims) + _dot(al, bh, dims) + _dot(ah, bl, dims)


def _mm(a, w, hi, dims=_NN):
    if hi:
        return _dot3(a.astype(F32), w.astype(F32), dims)
    return _dot(a.astype(BF16), w.astype(BF16), dims)


def _rms(x, g):
    return (x * lax.rsqrt(jnp.mean(x * x, axis=-1, keepdims=True) + EPS)) * g


def _rmsnorm_kernel(x_ref, g_ref, o_ref):
    o_ref[...] = _rms(x_ref[...], g_ref[...]).astype(o_ref.dtype)


def _rmsnorm(x, g, out_dtype):
    n, d = x.shape
    tm = min(TM_NORM, n)
    return pl.pallas_call(
        _rmsnorm_kernel,
        grid=(n // tm,),
        in_specs=[pl.BlockSpec((tm, d), lambda i: (i, 0)),
                  pl.BlockSpec((1, d), lambda i: (0, 0))],
        out_specs=pl.BlockSpec((tm, d), lambda i: (i, 0)),
        out_shape=jax.ShapeDtypeStruct((n, d), out_dtype),
        compiler_params=_cparams(("parallel",)),
        name="rmsnorm",
    )(x, g.reshape(1, d))


def _proj_kernel(x_ref, w_ref, o_ref, *, hi):
    o_ref[...] = _mm(x_ref[...], w_ref[...], hi)


def _in_proj(xn, w_in, layer, hi):
    n, d = xn.shape
    tm = min(TM_PROJ // 2 if hi else TM_PROJ, n)
    return pl.pallas_call(
        functools.partial(_proj_kernel, hi=hi),
        grid=(n // tm, G_OFF // TN_PROJ),
        in_specs=[pl.BlockSpec((tm, d), lambda i, j: (i, 0)),
                  pl.BlockSpec((None, d, TN_PROJ), lambda i, j: (layer, 0, j))],
        out_specs=pl.BlockSpec((tm, TN_PROJ), lambda i, j: (i, j)),
        out_shape=jax.ShapeDtypeStruct((n, G_OFF), F32),
        compiler_params=_cparams(("parallel", "arbitrary")),
        name="in_proj",
    )(xn, w_in)


def _rope_tables(pos):
    half = ROPE_DIM // 2
    inv = jnp.power(jnp.float32(ROPE_THETA), -jnp.arange(half, dtype=F32) * (2.0 / ROPE_DIM))
    ang = pos.astype(F32)[:, None] * inv[None, :]
    cos, sin = jnp.cos(ang), jnp.sin(ang)
    t = pos.shape[0]
    rest = HEAD_DIM - ROPE_DIM
    c = jnp.concatenate([cos, cos, jnp.ones((t, rest), F32)], axis=1)
    s1 = jnp.concatenate([-sin, jnp.zeros((t, HEAD_DIM - half), F32)], axis=1)
    s2 = jnp.concatenate([jnp.zeros((t, half), F32), sin, jnp.zeros((t, rest), F32)], axis=1)
    rep = LANES // HEAD_DIM
    return jnp.tile(c, (1, rep)), jnp.tile(s1, (1, rep)), jnp.tile(s2, (1, rep))


def _rope(x, c, s1, s2):
    half = ROPE_DIM // 2
    return x * c + pltpu.roll(x, LANES - half, 1) * s1 + pltpu.roll(x, half, 1) * s2


def _gelu(x):
    return 0.5 * x * (1.0 + lax.erf(x * 0.7071067811865476))


def _layernorm(v, g, b):
    mu = jnp.mean(v, axis=-1, keepdims=True)
    vc = v - mu
    return vc * lax.rsqrt(jnp.mean(vc * vc, axis=-1, keepdims=True) + EPS) * g + b


def _branch_prompt_kernel(sink_ref, z_ref, rc_ref, rs1_ref, rs2_ref, cw_ref, lng_ref, lnb_ref,
                          sw_ref, sbt_ref, pw_ref, ps_ref,
                          outs_ref, conv_ref, kst_ref, vst_ref, pool_ref,
                          kprev, vprev, ci_ext, p_ext, *, hi, t_start):
    tb = pl.program_id(1)
    halo_c = SUBLANES
    halo_p = POOL_MAX

    @pl.when(tb == 0)
    def _():
        kprev[...] = jnp.zeros_like(kprev)
        vprev[...] = jnp.zeros_like(vprev)
        ci_ext[0:halo_c, :] = jnp.zeros((halo_c, BRANCH_W), F32)
        p_ext[0:halo_p, :] = jnp.zeros((halo_p, BRANCH_W), F32)

    ci = z_ref[:, A_C_OFF:A_B_OFF] * z_ref[:, A_H_OFF:A_C_OFF]
    ci_ext[halo_c:halo_c + TB, :] = ci
    y = cw_ref[2:3, :] * ci
    for j in range(CONV_W - 1):
        y = y + cw_ref[j:j + 1, :] * ci_ext[pl.ds(halo_c - (CONV_W - 1) + j, TB), :]
    outs_ref[0] = (z_ref[:, A_B_OFF:Q_OFF] * y).astype(outs_ref.dtype)
    conv_ref[...] = ci_ext[pl.ds(halo_c + TB - (CONV_W - 1), CONV_W - 1), :]
    ci_ext[0:halo_c, :] = ci[TB - halo_c:, :]

    p = z_ref[:, D_OFF:G_OFF]
    p_ext[halo_p:halo_p + TB, :] = p
    posf = ((t_start + tb) * TB + lax.broadcasted_iota(jnp.int32, (TB, 1), 0)).astype(F32)
    for g, w in enumerate(POOL_WINDOWS):
        sl = slice(g * LANES, (g + 1) * LANES)
        s = p[:, sl]
        for k in range(1, w):
            s = s + p_ext[pl.ds(halo_p - k, TB), sl]
        dlt = s / jnp.minimum(posf + 1.0, float(w)) - p[:, sl]
        yg = _mm(dlt, pw_ref[g], hi) * ps_ref[:, sl]
        outs_ref[3, :, sl] = yg.astype(outs_ref.dtype)
    pool_ref[...] = p_ext[pl.ds(halo_p + TB - (POOL_MAX - 1), POOL_MAX - 1), :]
    p_ext[0:halo_p, :] = p[TB - halo_p:, :]

    ge = _gelu(z_ref[:, C_OFF:D_OFF])
    u = ge[:, :BRANCH_W]
    vn = _layernorm(ge[:, BRANCH_W:], lng_ref[...], lnb_ref[...])
    r_i = lax.broadcasted_iota(jnp.int32, (TB, TB), 0)
    c_i = lax.broadcasted_iota(jnp.int32, (TB, TB), 1)
    for g in range(SGU_GROUPS):
        sl = slice(g * LANES, (g + 1) * LANES)
        wt = jnp.where(c_i <= r_i, sw_ref[g], 0.0)
        f = _mm(wt, vn[:, sl], hi) + sbt_ref[:, g:g + 1]
        outs_ref[2, :, sl] = (u[:, sl] * f).astype(outs_ref.dtype)

    rc, rs1, rs2 = rc_ref[...], rs1_ref[...], rs2_ref[...]
    kr = _rope(z_ref[:, K_OFF:V_OFF], rc, rs1, rs2)
    v = z_ref[:, V_OFF:C_OFF]
    qr = [_rope(z_ref[:, Q_OFF + c * LANES:Q_OFF + (c + 1) * LANES], rc, rs1, rs2)
          for c in range(N_HEADS * HEAD_DIM // LANES)]
    qi = lax.broadcasted_iota(jnp.int32, (TB, 2 * TB), 0)
    kj = lax.broadcasted_iota(jnp.int32, (TB, 2 * TB), 1)
    allowed = (kj > qi) & (kj <= qi + WINDOW) & ((kj >= TB) | (tb > 0))
    kp, vp = kprev[...], vprev[...]
    heads = [None] * N_HEADS
    for hk in range(N_KV):
        hs = slice(hk * HEAD_DIM, (hk + 1) * HEAD_DIM)
        kk = jnp.concatenate([kp[:, hs], kr[:, hs]], axis=0)
        vv = jnp.concatenate([vp[:, hs], v[:, hs]], axis=0)
        if not hi:
            kk, vv = kk.astype(BF16), vv.astype(BF16)
        for g in range(GQA_G):
            hq = hk * GQA_G + g
            per = LANES // HEAD_DIM
            qh = qr[hq // per][:, (hq % per) * HEAD_DIM:(hq % per + 1) * HEAD_DIM]
            s = _mm(qh, kk, hi, _NT) * (HEAD_DIM ** -0.5)
            s = jnp.where(allowed, s, -1e30)
            sink = sink_ref[hq]
            m = jnp.maximum(jnp.max(s, axis=-1, keepdims=True), sink)
            e = jnp.exp(s - m)
            den = jnp.sum(e, axis=-1, keepdims=True) + jnp.exp(sink - m)
            heads[hq] = _mm(e / den, vv, hi)
    outs_ref[1] = jnp.concatenate(heads, axis=1).astype(outs_ref.dtype)
    kst_ref[...] = kr
    vst_ref[...] = v
    kprev[...] = kr
    vprev[...] = v


def _branch_prompt(zs, batch, seq, lw, hi=False, t_start=0):
    n = zs.shape[0]
    nt = n // (batch * TB)
    rc, rs1, rs2 = _rope_tables(jnp.arange(seq))
    full = lambda shape: pl.BlockSpec(shape, lambda b, t: (0,) * len(shape))
    tab = pl.BlockSpec((TB, LANES), lambda b, t: (t_start + t, 0))
    kv_w = N_KV * HEAD_DIM
    return pl.pallas_call(
        functools.partial(_branch_prompt_kernel, hi=hi, t_start=t_start),
        grid=(batch, nt),
        in_specs=[pl.BlockSpec(memory_space=pltpu.SMEM),
                  pl.BlockSpec((TB, G_OFF), lambda b, t: (b * nt + t, 0)),
                  tab, tab, tab,
                  full((CONV_W, BRANCH_W)), full((1, BRANCH_W)), full((1, BRANCH_W)),
                  full((SGU_GROUPS, CHUNK, CHUNK)), full((CHUNK, SGU_GROUPS)),
                  full((len(POOL_WINDOWS), LANES, LANES)), full((1, BRANCH_W))],
        out_specs=[pl.BlockSpec((N_BRANCH, TB, BRANCH_W), lambda b, t: (0, b * nt + t, 0)),
                   pl.BlockSpec((None, CONV_W - 1, BRANCH_W), lambda b, t: (b, 0, 0)),
                   pl.BlockSpec((None, WINDOW, kv_w), lambda b, t: (b, 0, 0)),
                   pl.BlockSpec((None, WINDOW, kv_w), lambda b, t: (b, 0, 0)),
                   pl.BlockSpec((None, POOL_MAX - 1, BRANCH_W), lambda b, t: (b, 0, 0))],
        out_shape=[jax.ShapeDtypeStruct((N_BRANCH, n, BRANCH_W), F32 if hi else BF16),
                   jax.ShapeDtypeStruct((batch, CONV_W - 1, BRANCH_W), F32),
                   jax.ShapeDtypeStruct((batch, WINDOW, kv_w), F32),
                   jax.ShapeDtypeStruct((batch, WINDOW, kv_w), F32),
                   jax.ShapeDtypeStruct((batch, POOL_MAX - 1, BRANCH_W), F32)],
        scratch_shapes=[pltpu.VMEM((TB, kv_w), F32), pltpu.VMEM((TB, kv_w), F32),
                        pltpu.VMEM((SUBLANES + TB, BRANCH_W), F32),
                        pltpu.VMEM((POOL_MAX + TB, BRANCH_W), F32)],
        compiler_params=_cparams(("arbitrary", "arbitrary")),
        name="branch_prompt",
    )(lw["attn_sinks"], zs, rc, rs1, rs2, lw["conv_w"], lw["sgu_ln_g"].reshape(1, -1),
      lw["sgu_ln_b"].reshape(1, -1), lw["sgu_w"], lw["sgu_b"].T, lw["pool_w"],
      lw["pool_scale"].reshape(1, -1))


def _branch_sample_kernel(sink_ref, z_ref, rc_ref, rs1_ref, rs2_ref, cw_ref, lng_ref, lnb_ref,
                          ssc_ref, ssh_ref, pw_ref, ps_ref, cpast_ref, kc_ref, vc_ref, ppast_ref,
                          outs_ref, conv_ref, kst_ref, vst_ref, pool_ref, chunkv_ref,
                          qrot, krot, vnew):
    nb = z_ref.shape[0]

    ci = z_ref[:, A_C_OFF:A_B_OFF] * z_ref[:, A_H_OFF:A_C_OFF]
    y = cw_ref[CONV_W - 1:CONV_W, :] * ci
    for j in range(CONV_W - 1):
        y = y + cw_ref[j:j + 1, :] * cpast_ref[j]
    outs_ref[0] = z_ref[:, A_B_OFF:Q_OFF] * y
    for j in range(CONV_W - 2):
        conv_ref[j] = cpast_ref[j + 1]
    conv_ref[CONV_W - 2] = ci

    p = z_ref[:, D_OFF:G_OFF]
    for g, w in enumerate(POOL_WINDOWS):
        sl = slice(g * LANES, (g + 1) * LANES)
        s = p[:, sl]
        for k in range(1, w):
            s = s + ppast_ref[POOL_MAX - 1 - k, :, sl]
        dlt = s / float(min(PAST_LEN + 1, w)) - p[:, sl]
        outs_ref[3, :, sl] = _dot3(dlt, pw_ref[g]) * ps_ref[:, sl]
    for j in range(POOL_MAX - 2):
        pool_ref[j] = ppast_ref[j + 1]
    pool_ref[POOL_MAX - 2] = p

    ge = _gelu(z_ref[:, C_OFF:D_OFF])
    vn = _layernorm(ge[:, BRANCH_W:], lng_ref[...], lnb_ref[...])
    outs_ref[2] = ge[:, :BRANCH_W] * (vn * ssc_ref[...] + ssh_ref[...])
    chunkv_ref[...] = vn

    rc, rs1, rs2 = rc_ref[...], rs1_ref[...], rs2_ref[...]
    krot[...] = _rope(z_ref[:, K_OFF:V_OFF], rc, rs1, rs2)
    vnew[...] = z_ref[:, V_OFF:C_OFF]
    for c in range(N_HEADS * HEAD_DIM // LANES):
        cs = slice(c * LANES, (c + 1) * LANES)
        qrot[:, cs] = _rope(z_ref[:, Q_OFF + c * LANES:Q_OFF + (c + 1) * LANES], rc, rs1, rs2)
    row = lax.broadcasted_iota(jnp.int32, (WINDOW, LANES), 0)
    lo = lax.broadcasted_iota(jnp.int32, (WINDOW, LANES), 1) < HEAD_DIM
    lo1 = lax.broadcasted_iota(jnp.int32, (1, LANES), 1) < HEAD_DIM
    scale = HEAD_DIM ** -0.5

    def body(b, carry):
        kn = jnp.where(row == WINDOW - 1, krot[pl.ds(b, 1), :], pltpu.roll(kc_ref[b], WINDOW - 1, 0))
        vn_b = jnp.where(row == WINDOW - 1, vnew[pl.ds(b, 1), :], pltpu.roll(vc_ref[b], WINDOW - 1, 0))
        kst_ref[b] = kn
        vst_ref[b] = vn_b
        qb = qrot[pl.ds(b, 1), :]
        o = []
        for g in range(GQA_G):
            ca = qb[:, (g // 2) * LANES:(g // 2 + 1) * LANES]
            cb = qb[:, (2 + g // 2) * LANES:(3 + g // 2) * LANES]
            if g % 2 == 0:
                qrow = jnp.where(lo1, ca, pltpu.roll(cb, HEAD_DIM, 1))
            else:
                qrow = jnp.where(lo1, pltpu.roll(ca, HEAD_DIM, 1), cb)
            prod = kn * qrow
            pn = []
            for part, hq in ((jnp.where(lo, prod, 0.0), g), (jnp.where(lo, 0.0, prod), GQA_G + g)):
                s = jnp.sum(part, axis=1, keepdims=True) * scale
                sink = sink_ref[hq]
                m = jnp.maximum(jnp.max(s, axis=0, keepdims=True), sink)
                e = jnp.exp(s - m)
                den = jnp.sum(e, axis=0, keepdims=True) + jnp.exp(sink - m)
                pn.append(e / den)
            o.append(jnp.sum(jnp.where(lo, pn[0], pn[1]) * vn_b, axis=0, keepdims=True))
        sw = lambda x: pltpu.roll(x, HEAD_DIM, 1)
        chunks = [jnp.where(lo1, o[0], sw(o[1])), jnp.where(lo1, o[2], sw(o[3])),
                  jnp.where(lo1, sw(o[0]), o[1]), jnp.where(lo1, sw(o[2]), o[3])]
        outs_ref[1, pl.ds(b, 1), :] = jnp.concatenate(chunks, axis=1)
        return carry

    lax.fori_loop(0, nb, body, 0)


def _branch_sample(zs, lw, cpast, kc, vc, ppast):
    nb = zs.shape[0]
    kv_w = N_KV * HEAD_DIM
    rc, rs1, rs2 = _rope_tables(jnp.full((1,), PAST_LEN, jnp.int32))
    ssc = jnp.repeat(lw["sgu_w"][:, 0, 0], CHUNK).reshape(1, BRANCH_W)
    ssh = jnp.repeat(lw["sgu_b"][:, 0], CHUNK).reshape(1, BRANCH_W)
    vspec = pl.BlockSpec(memory_space=pltpu.VMEM)
    return pl.pallas_call(
        _branch_sample_kernel,
        in_specs=[pl.BlockSpec(memory_space=pltpu.SMEM)] + [vspec] * 15,
        out_specs=[vspec] * 6,
        out_shape=[jax.ShapeDtypeStruct((N_BRANCH, nb, BRANCH_W), F32),
                   jax.ShapeDtypeStruct((CONV_W - 1, nb, BRANCH_W), F32),
                   jax.ShapeDtypeStruct((nb, WINDOW, kv_w), F32),
                   jax.ShapeDtypeStruct((nb, WINDOW, kv_w), F32),
                   jax.ShapeDtypeStruct((POOL_MAX - 1, nb, BRANCH_W), F32),
                   jax.ShapeDtypeStruct((nb, BRANCH_W), F32)],
        scratch_shapes=[pltpu.VMEM((nb, N_HEADS * HEAD_DIM), F32), pltpu.VMEM((nb, kv_w), F32),
                        pltpu.VMEM((nb, kv_w), F32)],
        compiler_params=pltpu.CompilerParams(vmem_limit_bytes=VMEM_LIMIT),
        name="branch_sample",
    )(lw["attn_sinks"], zs, rc, rs1, rs2, lw["conv_w"], lw["sgu_ln_g"].reshape(1, -1),
      lw["sgu_ln_b"].reshape(1, -1), ssc, ssh, lw["pool_w"], lw["pool_scale"].reshape(1, -1),
      cpast, kc, vc, ppast)


def _merge_kernel(xn_ref, outs_ref, wg_ref, wb_ref, o_ref, acc_ref, *, hi):
    nbr = pl.program_id(2)
    gate = jax.nn.sigmoid(_mm(xn_ref[...], wg_ref[...], hi))
    contrib = gate * _mm(outs_ref[...], wb_ref[...], hi)

    @pl.when(nbr == 0)
    def _():
        acc_ref[...] = contrib

    @pl.when(nbr > 0)
    def _():
        acc_ref[...] += contrib

    @pl.when(nbr == N_BRANCH - 1)
    def _():
        o_ref[...] = acc_ref[...].astype(o_ref.dtype)


def _merge(xn, outs, w_in, w_branch, layer, hi):
    n, d = xn.shape
    tm = min(TM_MERGE, n)
    g_blk = G_OFF // TN_MERGE
    per_branch = D_MODEL // TN_MERGE
    return pl.pallas_call(
        functools.partial(_merge_kernel, hi=hi),
        grid=(n // tm, per_branch, N_BRANCH),
        in_specs=[pl.BlockSpec((tm, d), lambda i, j, b: (i, 0)),
                  pl.BlockSpec((None, tm, BRANCH_W), lambda i, j, b: (b, i, 0)),
                  pl.BlockSpec((None, d, TN_MERGE),
                               lambda i, j, b: (layer, 0, g_blk + b * per_branch + j)),
                  pl.BlockSpec((None, None, BRANCH_W, TN_MERGE), lambda i, j, b: (layer, b, 0, j))],
        out_specs=pl.BlockSpec((tm, TN_MERGE), lambda i, j, b: (i, j)),
        out_shape=jax.ShapeDtypeStruct((n, D_MODEL), F32 if hi else BF16),
        scratch_shapes=[pltpu.VMEM((tm, TN_MERGE), F32)],
        compiler_params=_cparams(("parallel", "arbitrary", "arbitrary")),
        name="merge",
    )(xn, outs, w_in, w_branch)


def _wo_kernel(m_ref, w_ref, x_ref, o_ref, *, hi):
    o_ref[...] = x_ref[...] + _mm(m_ref[...], w_ref[...], hi)


def _out_proj(merged, w_o, x, layer, hi):
    n, d = x.shape
    tm = min(TM_WO, n)
    return pl.pallas_call(
        functools.partial(_wo_kernel, hi=hi),
        grid=(n // tm, d // TN_WO),
        in_specs=[pl.BlockSpec((tm, d), lambda i, j: (i, 0)),
                  pl.BlockSpec((None, d, TN_WO), lambda i, j: (layer, 0, j)),
                  pl.BlockSpec((tm, TN_WO), lambda i, j: (i, j))],
        out_specs=pl.BlockSpec((tm, TN_WO), lambda i, j: (i, j)),
        out_shape=jax.ShapeDtypeStruct((n, d), F32),
        compiler_params=_cparams(("parallel", "arbitrary")),
        name="out_proj",
    )(merged, w_o, x)


def _router_kernel(y_ref, g_ref, wr_ref, br_ref, xn_ref, info_ref, cnt_ref, carry_ref):
    i = pl.program_id(0)
    tm = y_ref.shape[0]

    @pl.when(i == 0)
    def _():
        carry_ref[...] = jnp.zeros_like(carry_ref)

    xn = _rms(y_ref[...], g_ref[...])
    xn_ref[...] = xn
    logits = _dot3(xn, wr_ref[...]) + br_ref[...]
    lane_i = lax.broadcasted_iota(jnp.int32, (tm, ROUTER_LANES), 1)
    lane = lane_i.astype(F32)
    lane_grp = lax.shift_right_logical(lane_i, 3).astype(F32)
    big = float(ROUTER_LANES)
    is_grp = (lane_i >= N_EXPERTS) & (lane_i < N_EXPERTS + N_EXP_GROUPS)
    lg = jnp.where(is_grp, logits, NEG)
    gmax = jnp.max(lg, axis=-1, keepdims=True)
    gsel = jnp.min(jnp.where(lg == gmax, lane - N_EXPERTS, big), axis=-1, keepdims=True)
    wg = 1.0 / jnp.sum(jnp.where(is_grp, jnp.exp(logits - gmax), 0.0), axis=-1, keepdims=True)
    in_grp = (lane_i < N_EXPERTS) & (lane_grp == gsel)
    v1 = jnp.where(in_grp, logits, NEG)
    m1 = jnp.max(v1, axis=-1, keepdims=True)
    i1 = jnp.min(jnp.where(v1 == m1, lane, big), axis=-1, keepdims=True)
    v2 = jnp.where(lane == i1, NEG, v1)
    m2 = jnp.max(v2, axis=-1, keepdims=True)
    i2 = jnp.min(jnp.where(v2 == m2, lane, big), axis=-1, keepdims=True)
    e2 = jnp.exp(m2 - m1)
    w1 = wg / (1.0 + e2)
    w2 = wg * e2 / (1.0 + e2)

    sel1 = lane == i1
    sel2 = lane == i2
    onehot = jnp.where(sel1 | sel2, 1.0, 0.0)
    r_i = lax.broadcasted_iota(jnp.int32, (tm, tm), 0)
    c_i = lax.broadcasted_iota(jnp.int32, (tm, tm), 1)
    before = jnp.where(c_i < r_i, 1.0, 0.0).astype(BF16)
    counts = _dot(before, onehot.astype(BF16)) + carry_ref[0:1, :]
    r1 = jnp.sum(jnp.where(sel1, counts, 0.0), axis=-1, keepdims=True)
    r2 = jnp.sum(jnp.where(sel2, counts, 0.0), axis=-1, keepdims=True)
    carry_ref[...] = carry_ref[...] + jnp.sum(onehot, axis=0, keepdims=True)
    cnt_ref[...] = carry_ref[...]

    info = jnp.zeros((tm, ROUTER_LANES), F32)
    for k, val in enumerate((i1, i2, r1, r2, w1, w2)):
        info = jnp.where(lane_i == k, val, info)
    info_ref[...] = info


def _router(y, g, w_router, b_router):
    n, d = y.shape
    tm = min(TM_NORM, n)
    return pl.pallas_call(
        _router_kernel,
        grid=(n // tm,),
        in_specs=[pl.BlockSpec((tm, d), lambda i: (i, 0)),
                  pl.BlockSpec((1, d), lambda i: (0, 0)),
                  pl.BlockSpec((d, ROUTER_LANES), lambda i: (0, 0)),
                  pl.BlockSpec((1, ROUTER_LANES), lambda i: (0, 0))],
        out_specs=[pl.BlockSpec((tm, d), lambda i: (i, 0)),
                   pl.BlockSpec((tm, ROUTER_LANES), lambda i: (i, 0)),
                   pl.BlockSpec((SUBLANES, ROUTER_LANES), lambda i: (0, 0))],
        out_shape=[jax.ShapeDtypeStruct((n, d), F32),
                   jax.ShapeDtypeStruct((n, ROUTER_LANES), F32),
                   jax.ShapeDtypeStruct((SUBLANES, ROUTER_LANES), F32)],
        scratch_shapes=[pltpu.VMEM((SUBLANES, ROUTER_LANES), F32)],
        compiler_params=_cparams(("arbitrary",)),
        name="router",
    )(y, g.reshape(1, d), w_router, b_router)


def _row_copy(src, i, dst, j, sem):
    return pltpu.make_async_copy(src.at[pl.ds(i, 1)], dst.at[pl.ds(j, 1)], sem)


def _dispatch_kernel(pos_ref, xn_hbm, xs_in, xs_out, sem, *, rows):
    del xs_in
    base = pl.program_id(0) * rows

    def issue(r, c):
        t = base + r
        _row_copy(xn_hbm, t, xs_out, pos_ref[2 * t], sem).start()
        _row_copy(xn_hbm, t, xs_out, pos_ref[2 * t + 1], sem).start()
        return c

    def drain(r, c):
        t = base + r
        _row_copy(xn_hbm, t, xs_out, pos_ref[2 * t], sem).wait()
        _row_copy(xn_hbm, t, xs_out, pos_ref[2 * t + 1], sem).wait()
        return c

    lax.fori_loop(0, rows, issue, 0)
    lax.fori_loop(0, rows, drain, 0)


def _dispatch(pos, xn, xs):
    n = xn.shape[0]
    rows = min(TM_NORM, n)
    any_spec = pl.BlockSpec(memory_space=pl.ANY)
    return pl.pallas_call(
        functools.partial(_dispatch_kernel, rows=rows),
        grid_spec=pltpu.PrefetchScalarGridSpec(
            num_scalar_prefetch=1, grid=(n // rows,),
            in_specs=[any_spec, any_spec], out_specs=any_spec,
            scratch_shapes=[pltpu.SemaphoreType.DMA(())]),
        out_shape=jax.ShapeDtypeStruct(xs.shape, xs.dtype),
        input_output_aliases={2: 0},
        compiler_params=pltpu.CompilerParams(dimension_semantics=("arbitrary",),
                                             has_side_effects=True),
        name="dispatch",
    )(pos, xn, xs)


def _moe_kernel(te_ref, na_ref, xs_ref, wg_ref, wu_ref, wd_ref, ys_ref):
    del te_ref
    active = pl.program_id(0) < na_ref[0]

    @pl.when(active)
    def _():
        x = xs_ref[...].astype(BF16)
        a = _dot(x, wg_ref[...].astype(BF16))
        u = _dot(x, wu_ref[...].astype(BF16))
        h = (a * jax.nn.sigmoid(a)) * u
        ys_ref[...] = _dot(h.astype(BF16), wd_ref[...].astype(BF16))

    @pl.when(jnp.logical_not(active))
    def _():
        ys_ref[...] = jnp.zeros_like(ys_ref)


def _moe(tile_expert, n_active, xs, w_gate, w_up, w_down, layer):
    p, d = xs.shape
    row = lambda g, te, na: (jnp.minimum(g, na[0] - 1), 0)
    wsel = lambda g, te, na: (layer, te[g], 0, 0)
    return pl.pallas_call(
        _moe_kernel,
        grid_spec=pltpu.PrefetchScalarGridSpec(
            num_scalar_prefetch=2, grid=(p // TM_MOE,),
            in_specs=[pl.BlockSpec((TM_MOE, d), row),
                      pl.BlockSpec((None, None, d, EXPERT_FF), wsel),
                      pl.BlockSpec((None, None, d, EXPERT_FF), wsel),
                      pl.BlockSpec((None, None, EXPERT_FF, d), wsel)],
            out_specs=pl.BlockSpec((TM_MOE, d), lambda g, te, na: (g, 0))),
        out_shape=jax.ShapeDtypeStruct((p, d), F32),
        compiler_params=_cparams(("arbitrary",)),
        name="moe_ffn",
    )(tile_expert, n_active, xs, w_gate, w_up, w_down)


def _combine_kernel(pos_ref, y_ref, info_ref, g_ref, ys_hbm, *rest, rows, write_x):
    if write_x:
        xo_ref, xno_ref, buf0, buf1, sem = rest
    else:
        xno_ref, buf0, buf1, sem = rest
    base = pl.program_id(0) * rows

    def issue(r, c):
        t = base + r
        _row_copy(ys_hbm, pos_ref[2 * t], buf0, r, sem).start()
        _row_copy(ys_hbm, pos_ref[2 * t + 1], buf1, r, sem).start()
        return c

    def drain(r, c):
        t = base + r
        _row_copy(ys_hbm, pos_ref[2 * t], buf0, r, sem).wait()
        _row_copy(ys_hbm, pos_ref[2 * t + 1], buf1, r, sem).wait()
        return c

    lax.fori_loop(0, rows, issue, 0)
    lax.fori_loop(0, rows, drain, 0)
    xnew = y_ref[...] + info_ref[:, 4:5] * buf0[...] + info_ref[:, 5:6] * buf1[...]
    if write_x:
        xo_ref[...] = xnew
    xno_ref[...] = _rms(xnew, g_ref[...]).astype(xno_ref.dtype)


def _combine(pos, y, info, g, ys, norm_dtype, write_x):
    n, d = y.shape
    rows = min(TM_MOE, n)
    tile = pl.BlockSpec((rows, d), lambda i, pos: (i, 0))
    out_specs = [tile]
    out_shape = [jax.ShapeDtypeStruct((n, d), norm_dtype)]
    if write_x:
        out_specs = [tile] + out_specs
        out_shape = [jax.ShapeDtypeStruct((n, d), F32)] + out_shape
    return pl.pallas_call(
        functools.partial(_combine_kernel, rows=rows, write_x=write_x),
        grid_spec=pltpu.PrefetchScalarGridSpec(
            num_scalar_prefetch=1, grid=(n // rows,),
            in_specs=[tile,
                      pl.BlockSpec((rows, ROUTER_LANES), lambda i, pos: (i, 0)),
                      pl.BlockSpec((1, d), lambda i, pos: (0, 0)),
                      pl.BlockSpec(memory_space=pl.ANY)],
            out_specs=out_specs,
            scratch_shapes=[pltpu.VMEM((rows, d), F32), pltpu.VMEM((rows, d), F32),
                            pltpu.SemaphoreType.DMA(())]),
        out_shape=out_shape,
        compiler_params=_cparams(("arbitrary",)),
        name="combine",
    )(pos, y, info, g.reshape(1, d), ys)


def _plan(info_p, cnt_p, info_s, cnt_s, n_tiles):
    cp = cnt_p[0, :N_EXPERTS].astype(jnp.int32)
    cs = cnt_s[0, :N_EXPERTS].astype(jnp.int32)
    tiles = (cp + cs + TM_MOE - 1) // TM_MOE
    tile_end = jnp.cumsum(tiles)
    row_off = (tile_end - tiles) * TM_MOE
    n_active = tile_end[-1]
    e_p = info_p[:, 0:2].astype(jnp.int32)
    e_s = info_s[:, 0:2].astype(jnp.int32)
    pos_p = row_off[e_p] + info_p[:, 2:4].astype(jnp.int32)
    pos_s = row_off[e_s] + cp[e_s] + info_s[:, 2:4].astype(jnp.int32)
    g = jnp.minimum(jnp.arange(n_tiles, dtype=jnp.int32), n_active - 1)
    tile_expert = jnp.sum((g[:, None] >= tile_end[None, :]).astype(jnp.int32), axis=1)
    return (pos_p.reshape(-1), pos_s.reshape(-1), tile_expert.astype(jnp.int32),
            n_active.reshape(1).astype(jnp.int32))


def kernel(x_prompt, x_sample, state_conv, cache_win_k, cache_win_v, state_pool, norm1_g, w_in,
           conv_w, attn_sinks, sgu_ln_g, sgu_ln_b, sgu_w, sgu_b, pool_w, pool_scale, w_branch, w_o,
           norm2_g, router_group_w, router_group_b, router_expert_w, router_expert_b, moe_w_gate,
           moe_w_up, moe_w_down, final_norm_g):
    batch, seq, d = x_prompt.shape
    assert seq % TB == 0 and seq >= 2 * TB and d == D_MODEL
    nb = x_sample.shape[0]
    depth = w_in.shape[0]
    n_p = batch * seq
    kv_w = N_KV * HEAD_DIM
    n_slots = 2 * (n_p + nb)
    n_tiles = (n_slots + N_EXPERTS * (TM_MOE - 1) + TM_MOE - 1) // TM_MOE

    xp = x_prompt.reshape(n_p, d)
    xs_ = x_sample.reshape(nb, d)
    xn_p = _rmsnorm(xp, norm1_g[0], BF16)
    xn_s = _rmsnorm(xs_, norm1_g[0], F32)

    pad = ROUTER_LANES - N_EXPERTS - N_EXP_GROUPS
    conv_p, k_p, v_p, pool_p = [], [], [], []
    conv_s, k_s, v_s, pool_s, chunk_s = [], [], [], [], []
    y_prompt = y_sample = None
    for l in range(depth):
        lw = dict(attn_sinks=attn_sinks[l], conv_w=conv_w[l], sgu_ln_g=sgu_ln_g[l],
                  sgu_ln_b=sgu_ln_b[l], sgu_w=sgu_w[l], sgu_b=sgu_b[l], pool_w=pool_w[l],
                  pool_scale=pool_scale[l])
        w_router = jnp.concatenate(
            [router_expert_w[l], router_group_w[l], jnp.zeros((d, pad), F32)], axis=1)
        b_router = jnp.concatenate(
            [router_expert_b[l], router_group_b[l], jnp.zeros((pad,), F32)]).reshape(1, -1)
        last = l == depth - 1
        g_next = final_norm_g if last else norm1_g[l + 1]

        zs_p = _in_proj(xn_p, w_in, l, False)
        outs_p, c_st, k_st, v_st, p_st = _branch_prompt(zs_p, batch, seq, lw)
        conv_p.append(c_st)
        k_p.append(k_st.reshape(batch, WINDOW, N_KV, HEAD_DIM))
        v_p.append(v_st.reshape(batch, WINDOW, N_KV, HEAD_DIM))
        pool_p.append(p_st)
        merged_p = _merge(xn_p, outs_p, w_in, w_branch, l, False)
        yp = _out_proj(merged_p, w_o, xp, l, False)
        if not last:
            x_t = xp.reshape(batch, seq, d)[:, seq - 2 * TB:].reshape(batch * 2 * TB, d)
            xn_t = _rmsnorm(x_t, norm1_g[l], F32)
            zs_t = _in_proj(xn_t, w_in, l, True)
            outs_t = _branch_prompt(zs_t, batch, seq, lw, hi=True, t_start=seq // TB - 2)[0]
            second = lambda a: a.reshape(batch, 2, TB, a.shape[-1])[:, 1].reshape(batch * TB, -1)
            outs_t = jnp.stack([second(outs_t[b]) for b in range(N_BRANCH)])
            merged_t = _merge(second(xn_t), outs_t, w_in, w_branch, l, True)
            y_t = _out_proj(merged_t, w_o, second(x_t), l, True)
            yp = yp.reshape(batch, seq, d).at[:, seq - TB:].set(
                y_t.reshape(batch, TB, d)).reshape(n_p, d)

        zs_s = _in_proj(xn_s, w_in, l, True)
        outs_s, c_st, k_st, v_st, p_st, cv = _branch_sample(
            zs_s, lw, jnp.swapaxes(state_conv[l], 0, 1),
            cache_win_k[l].reshape(nb, WINDOW, kv_w), cache_win_v[l].reshape(nb, WINDOW, kv_w),
            jnp.swapaxes(state_pool[l], 0, 1))
        conv_s.append(jnp.swapaxes(c_st, 0, 1))
        k_s.append(k_st.reshape(nb, WINDOW, N_KV, HEAD_DIM))
        v_s.append(v_st.reshape(nb, WINDOW, N_KV, HEAD_DIM))
        pool_s.append(jnp.swapaxes(p_st, 0, 1))
        chunk_s.append(cv.reshape(nb, 1, BRANCH_W))
        merged_s = _merge(xn_s, outs_s, w_in, w_branch, l, True)
        ys_ = _out_proj(merged_s, w_o, xs_, l, True)

        xn2_p, info_p, cnt_p = _router(yp, norm2_g[l], w_router, b_router)
        xn2_s, info_s, cnt_s = _router(ys_, norm2_g[l], w_router, b_router)
        pos_p, pos_s, tile_expert, n_active = _plan(info_p, cnt_p, info_s, cnt_s, n_tiles)
        rows_sorted = jnp.zeros((n_tiles * TM_MOE, d), F32)
        rows_sorted = _dispatch(pos_p, xn2_p, rows_sorted)
        rows_sorted = _dispatch(pos_s, xn2_s, rows_sorted)
        ffn = _moe(tile_expert, n_active, rows_sorted, moe_w_gate, moe_w_up, moe_w_down, l)
        if last:
            (y_prompt,) = _combine(pos_p, yp, info_p, g_next, ffn, F32, False)
            (y_sample,) = _combine(pos_s, ys_, info_s, g_next, ffn, F32, False)
        else:
            xp, xn_p = _combine(pos_p, yp, info_p, g_next, ffn, BF16, True)
            xs_, xn_s = _combine(pos_s, ys_, info_s, g_next, ffn, F32, True)

    return (y_prompt.reshape(batch, seq, d), y_sample.reshape(nb, 1, d),
            jnp.stack(conv_p), jnp.stack(k_p), jnp.stack(v_p), jnp.stack(pool_p),
            jnp.stack(conv_s), jnp.stack(k_s), jnp.stack(v_s), jnp.stack(pool_s),
            jnp.stack(chunk_s))
```

```python
import functools

import jax
import jax.numpy as jnp
from jax import lax
from jax.experimental import pallas as pl
from jax.experimental.pallas import tpu as pltpu

F32 = jnp.float32
BF16 = jnp.bfloat16

D_MODEL = 2048
BRANCH_W = 512
N_BRANCH = 4
CONV_W = 3
HEAD_DIM = 64
N_HEADS = 8
N_KV = 2
GQA_G = 4
WINDOW = 128
ROPE_THETA = 500000.0
ROPE_DIM = 16
CHUNK = 128
SGU_GROUPS = 4
POOL_WINDOWS = (2, 4, 8, 16)
POOL_MAX = 16
N_EXP_GROUPS = 4
EXP_PER_GROUP = 8
N_EXPERTS = 32
EXPERT_FF = 512
EPS = 1e-6
PAST_LEN = 16384
NEG = -3.0e38

A_H_OFF = 0
A_C_OFF = 512
A_B_OFF = 1024
Q_OFF = 1536
K_OFF = 2048
V_OFF = 2176
C_OFF = 2304
D_OFF = 3328
G_OFF = 3840

LANES = 128
SUBLANES = 8
VMEM_LIMIT = 52 * 1024 * 1024

TM_NORM = 512
TM_PROJ = 1024
TN_PROJ = 768
TM_MERGE = 2048
TN_MERGE = 256
TM_WO = 1024
TN_WO = 512
TM_MOE = 256
TB = 128
ROUTER_LANES = 128


def _cparams(sem):
    return pltpu.CompilerParams(dimension_semantics=sem, vmem_limit_bytes=VMEM_LIMIT)


_NN = (((1,), (0,)), ((), ()))
_NT = (((1,), (1,)), ((), ()))


def _dot(a, b, dims=_NN):
    return lax.dot_general(a, b, dims, preferred_element_type=F32)


def _split(x):
    hi = x.astype(BF16)
    lo = (x - hi.astype(F32)).astype(BF16)
    return hi, lo


def _dot3(a, b, dims=_NN):
    ah, al = _split(a)
    bh, bl = _split(b)
    return _dot(ah, bh, dims) + _dot(al, bh, dims) + _dot(ah, bl, dims)


def _mm(a, w, hi, dims=_NN):
    if hi:
        return _dot3(a.astype(F32), w.astype(F32), dims)
    return _dot(a.astype(BF16), w.astype(BF16), dims)


def _rms(x, g):
    return (x * lax.rsqrt(jnp.mean(x * x, axis=-1, keepdims=True) + EPS)) * g


def _rmsnorm_kernel(x_ref, g_ref, o_ref):
    o_ref[...] = _rms(x_ref[...], g_ref[...]).astype(o_ref.dtype)


def _rmsnorm(x, g, out_dtype):
    n, d = x.shape
    tm = min(TM_NORM, n)
    return pl.pallas_call(
        _rmsnorm_kernel,
        grid=(n // tm,),
        in_specs=[pl.BlockSpec((tm, d), lambda i: (i, 0)),
                  pl.BlockSpec((1, d), lambda i: (0, 0))],
        out_specs=pl.BlockSpec((tm, d), lambda i: (i, 0)),
        out_shape=jax.ShapeDtypeStruct((n, d), out_dtype),
        compiler_params=_cparams(("parallel",)),
        name="rmsnorm",
    )(x, g.reshape(1, d))


def _proj_kernel(x_ref, w_ref, o_ref, *, hi):
    o_ref[...] = _mm(x_ref[...], w_ref[...], hi)


def _in_proj(xn, w_in, layer, hi):
    n, d = xn.shape
    tm = min(TM_PROJ // 2 if hi else TM_PROJ, n)
    return pl.pallas_call(
        functools.partial(_proj_kernel, hi=hi),
        grid=(n // tm, G_OFF // TN_PROJ),
        in_specs=[pl.BlockSpec((tm, d), lambda i, j: (i, 0)),
                  pl.BlockSpec((None, d, TN_PROJ), lambda i, j: (layer, 0, j))],
        out_specs=pl.BlockSpec((tm, TN_PROJ), lambda i, j: (i, j)),
        out_shape=jax.ShapeDtypeStruct((n, G_OFF), F32),
        compiler_params=_cparams(("parallel", "arbitrary")),
        name="in_proj",
    )(xn, w_in)


def _rope_tables(pos):
    half = ROPE_DIM // 2
    inv = jnp.power(jnp.float32(ROPE_THETA), -jnp.arange(half, dtype=F32) * (2.0 / ROPE_DIM))
    ang = pos.astype(F32)[:, None] * inv[None, :]
    cos, sin = jnp.cos(ang), jnp.sin(ang)
    t = pos.shape[0]
    rest = HEAD_DIM - ROPE_DIM
    c = jnp.concatenate([cos, cos, jnp.ones((t, rest), F32)], axis=1)
    s1 = jnp.concatenate([-sin, jnp.zeros((t, HEAD_DIM - half), F32)], axis=1)
    s2 = jnp.concatenate([jnp.zeros((t, half), F32), sin, jnp.zeros((t, rest), F32)], axis=1)
    rep = LANES // HEAD_DIM
    return jnp.tile(c, (1, rep)), jnp.tile(s1, (1, rep)), jnp.tile(s2, (1, rep))


def _rope(x, c, s1, s2):
    half = ROPE_DIM // 2
    return x * c + pltpu.roll(x, LANES - half, 1) * s1 + pltpu.roll(x, half, 1) * s2


def _gelu(x):
    return 0.5 * x * (1.0 + lax.erf(x * 0.7071067811865476))


def _layernorm(v, g, b):
    mu = jnp.mean(v, axis=-1, keepdims=True)
    vc = v - mu
    return vc * lax.rsqrt(jnp.mean(vc * vc, axis=-1, keepdims=True) + EPS) * g + b


def _branch_prompt_kernel(sink_ref, z_ref, rc_ref, rs1_ref, rs2_ref, cw_ref, lng_ref, lnb_ref,
                          sw_ref, sbt_ref, pw_ref, ps_ref,
                          outs_ref, conv_ref, kst_ref, vst_ref, pool_ref,
                          kprev, vprev, ci_ext, p_ext, *, hi, t_start):
    tb = pl.program_id(1)
    halo_c = SUBLANES
    halo_p = POOL_MAX

    @pl.when(tb == 0)
    def _():
        kprev[...] = jnp.zeros_like(kprev)
        vprev[...] = jnp.zeros_like(vprev)
        ci_ext[0:halo_c, :] = jnp.zeros((halo_c, BRANCH_W), F32)
        p_ext[0:halo_p, :] = jnp.zeros((halo_p, BRANCH_W), F32)

    ci = z_ref[:, A_C_OFF:A_B_OFF] * z_ref[:, A_H_OFF:A_C_OFF]
    ci_ext[halo_c:halo_c + TB, :] = ci
    y = cw_ref[2:3, :] * ci
    for j in range(CONV_W - 1):
        y = y + cw_ref[j:j + 1, :] * ci_ext[pl.ds(halo_c - (CONV_W - 1) + j, TB), :]
    outs_ref[0] = (z_ref[:, A_B_OFF:Q_OFF] * y).astype(outs_ref.dtype)
    conv_ref[...] = ci_ext[pl.ds(halo_c + TB - (CONV_W - 1), CONV_W - 1), :]
    ci_ext[0:halo_c, :] = ci[TB - halo_c:, :]

    p = z_ref[:, D_OFF:G_OFF]
    p_ext[halo_p:halo_p + TB, :] = p
    posf = ((t_start + tb) * TB + lax.broadcasted_iota(jnp.int32, (TB, 1), 0)).astype(F32)
    for g, w in enumerate(POOL_WINDOWS):
        sl = slice(g * LANES, (g + 1) * LANES)
        s = p[:, sl]
        for k in range(1, w):
            s = s + p_ext[pl.ds(halo_p - k, TB), sl]
        dlt = s / jnp.minimum(posf + 1.0, float(w)) - p[:, sl]
        yg = _mm(dlt, pw_ref[g], hi) * ps_ref[:, sl]
        outs_ref[3, :, sl] = yg.astype(outs_ref.dtype)
    pool_ref[...] = p_ext[pl.ds(halo_p + TB - (POOL_MAX - 1), POOL_MAX - 1), :]
    p_ext[0:halo_p, :] = p[TB - halo_p:, :]

    ge = _gelu(z_ref[:, C_OFF:D_OFF])
    u = ge[:, :BRANCH_W]
    vn = _layernorm(ge[:, BRANCH_W:], lng_ref[...], lnb_ref[...])
    r_i = lax.broadcasted_iota(jnp.int32, (TB, TB), 0)
    c_i = lax.broadcasted_iota(jnp.int32, (TB, TB), 1)
    for g in range(SGU_GROUPS):
        sl = slice(g * LANES, (g + 1) * LANES)
        wt = jnp.where(c_i <= r_i, sw_ref[g], 0.0)
        f = _mm(wt, vn[:, sl], hi) + sbt_ref[:, g:g + 1]
        outs_ref[2, :, sl] = (u[:, sl] * f).astype(outs_ref.dtype)

    rc, rs1, rs2 = rc_ref[...], rs1_ref[...], rs2_ref[...]
    kr = _rope(z_ref[:, K_OFF:V_OFF], rc, rs1, rs2)
    v = z_ref[:, V_OFF:C_OFF]
    qr = [_rope(z_ref[:, Q_OFF + c * LANES:Q_OFF + (c + 1) * LANES], rc, rs1, rs2)
          for c in range(N_HEADS * HEAD_DIM // LANES)]
    qi = lax.broadcasted_iota(jnp.int32, (TB, 2 * TB), 0)
    kj = lax.broadcasted_iota(jnp.int32, (TB, 2 * TB), 1)
    allowed = (kj > qi) & (kj <= qi + WINDOW) & ((kj >= TB) | (tb > 0))
    kp, vp = kprev[...], vprev[...]
    heads = [None] * N_HEADS
    for hk in range(N_KV):
        hs = slice(hk * HEAD_DIM, (hk + 1) * HEAD_DIM)
        kk = jnp.concatenate([kp[:, hs], kr[:, hs]], axis=0)
        vv = jnp.concatenate([vp[:, hs], v[:, hs]], axis=0)
        if not hi:
            kk, vv = kk.astype(BF16), vv.astype(BF16)
        for g in range(GQA_G):
            hq = hk * GQA_G + g
            per = LANES // HEAD_DIM
            qh = qr[hq // per][:, (hq % per) * HEAD_DIM:(hq % per + 1) * HEAD_DIM]
            s = _mm(qh, kk, hi, _NT) * (HEAD_DIM ** -0.5)
            s = jnp.where(allowed, s, -1e30)
            sink = sink_ref[hq]
            m = jnp.maximum(jnp.max(s, axis=-1, keepdims=True), sink)
            e = jnp.exp(s - m)
            den = jnp.sum(e, axis=-1, keepdims=True) + jnp.exp(sink - m)
            heads[hq] = _mm(e / den, vv, hi)
    outs_ref[1] = jnp.concatenate(heads, axis=1).astype(outs_ref.dtype)
    kst_ref[...] = kr
    vst_ref[...] = v
    kprev[...] = kr
    vprev[...] = v


def _branch_prompt(zs, batch, seq, lw, hi=False, t_start=0):
    n = zs.shape[0]
    nt = n // (batch * TB)
    rc, rs1, rs2 = _rope_tables(jnp.arange(seq))
    full = lambda shape: pl.BlockSpec(shape, lambda b, t: (0,) * len(shape))
    tab = pl.BlockSpec((TB, LANES), lambda b, t: (t_start + t, 0))
    kv_w = N_KV * HEAD_DIM
    return pl.pallas_call(
        functools.partial(_branch_prompt_kernel, hi=hi, t_start=t_start),
        grid=(batch, nt),
        in_specs=[pl.BlockSpec(memory_space=pltpu.SMEM),
                  pl.BlockSpec((TB, G_OFF), lambda b, t: (b * nt + t, 0)),
                  tab, tab, tab,
                  full((CONV_W, BRANCH_W)), full((1, BRANCH_W)), full((1, BRANCH_W)),
                  full((SGU_GROUPS, CHUNK, CHUNK)), full((CHUNK, SGU_GROUPS)),
                  full((len(POOL_WINDOWS), LANES, LANES)), full((1, BRANCH_W))],
        out_specs=[pl.BlockSpec((N_BRANCH, TB, BRANCH_W), lambda b, t: (0, b * nt + t, 0)),
                   pl.BlockSpec((None, CONV_W - 1, BRANCH_W), lambda b, t: (b, 0, 0)),
                   pl.BlockSpec((None, WINDOW, kv_w), lambda b, t: (b, 0, 0)),
                   pl.BlockSpec((None, WINDOW, kv_w), lambda b, t: (b, 0, 0)),
                   pl.BlockSpec((None, POOL_MAX - 1, BRANCH_W), lambda b, t: (b, 0, 0))],
        out_shape=[jax.ShapeDtypeStruct((N_BRANCH, n, BRANCH_W), F32 if hi else BF16),
                   jax.ShapeDtypeStruct((batch, CONV_W - 1, BRANCH_W), F32),
                   jax.ShapeDtypeStruct((batch, WINDOW, kv_w), F32),
                   jax.ShapeDtypeStruct((batch, WINDOW, kv_w), F32),
                   jax.ShapeDtypeStruct((batch, POOL_MAX - 1, BRANCH_W), F32)],
        scratch_shapes=[pltpu.VMEM((TB, kv_w), F32), pltpu.VMEM((TB, kv_w), F32),
                        pltpu.VMEM((SUBLANES + TB, BRANCH_W), F32),
                        pltpu.VMEM((POOL_MAX + TB, BRANCH_W), F32)],
        compiler_params=_cparams(("arbitrary", "arbitrary")),
        name="branch_prompt",
    )(lw["attn_sinks"], zs, rc, rs1, rs2, lw["conv_w"], lw["sgu_ln_g"].reshape(1, -1),
      lw["sgu_ln_b"].reshape(1, -1), lw["sgu_w"], lw["sgu_b"].T, lw["pool_w"],
      lw["pool_scale"].reshape(1, -1))


def _branch_sample_kernel(sink_ref, z_ref, rc_ref, rs1_ref, rs2_ref, cw_ref, lng_ref, lnb_ref,
                          ssc_ref, ssh_ref, pw_ref, ps_ref, cpast_ref, kc_ref, vc_ref, ppast_ref,
                          outs_ref, conv_ref, kst_ref, vst_ref, pool_ref, chunkv_ref,
                          qrot, krot, vnew):
    nb = z_ref.shape[0]

    ci = z_ref[:, A_C_OFF:A_B_OFF] * z_ref[:, A_H_OFF:A_C_OFF]
    y = cw_ref[CONV_W - 1:CONV_W, :] * ci
    for j in range(CONV_W - 1):
        y = y + cw_ref[j:j + 1, :] * cpast_ref[j]
    outs_ref[0] = z_ref[:, A_B_OFF:Q_OFF] * y
    for j in range(CONV_W - 2):
        conv_ref[j] = cpast_ref[j + 1]
    conv_ref[CONV_W - 2] = ci

    p = z_ref[:, D_OFF:G_OFF]
    for g, w in enumerate(POOL_WINDOWS):
        sl = slice(g * LANES, (g + 1) * LANES)
        s = p[:, sl]
        for k in range(1, w):
            s = s + ppast_ref[POOL_MAX - 1 - k, :, sl]
        dlt = s / float(min(PAST_LEN + 1, w)) - p[:, sl]
        outs_ref[3, :, sl] = _dot3(dlt, pw_ref[g]) * ps_ref[:, sl]
    for j in range(POOL_MAX - 2):
        pool_ref[j] = ppast_ref[j + 1]
    pool_ref[POOL_MAX - 2] = p

    ge = _gelu(z_ref[:, C_OFF:D_OFF])
    vn = _layernorm(ge[:, BRANCH_W:], lng_ref[...], lnb_ref[...])
    outs_ref[2] = ge[:, :BRANCH_W] * (vn * ssc_ref[...] + ssh_ref[...])
    chunkv_ref[...] = vn

    rc, rs1, rs2 = rc_ref[...], rs1_ref[...], rs2_ref[...]
    krot[...] = _rope(z_ref[:, K_OFF:V_OFF], rc, rs1, rs2)
    vnew[...] = z_ref[:, V_OFF:C_OFF]
    for c in range(N_HEADS * HEAD_DIM // LANES):
        cs = slice(c * LANES, (c + 1) * LANES)
        qrot[:, cs] = _rope(z_ref[:, Q_OFF + c * LANES:Q_OFF + (c + 1) * LANES], rc, rs1, rs2)
    row = lax.broadcasted_iota(jnp.int32, (WINDOW, LANES), 0)
    lo = lax.broadcasted_iota(jnp.int32, (WINDOW, LANES), 1) < HEAD_DIM
    lo1 = lax.broadcasted_iota(jnp.int32, (1, LANES), 1) < HEAD_DIM
    scale = HEAD_DIM ** -0.5

    def body(b, carry):
        kn = jnp.where(row == WINDOW - 1, krot[pl.ds(b, 1), :], pltpu.roll(kc_ref[b], WINDOW - 1, 0))
        vn_b = jnp.where(row == WINDOW - 1, vnew[pl.ds(b, 1), :], pltpu.roll(vc_ref[b], WINDOW - 1, 0))
        kst_ref[b] = kn
        vst_ref[b] = vn_b
        qb = qrot[pl.ds(b, 1), :]
        o = []
        for g in range(GQA_G):
            ca = qb[:, (g // 2) * LANES:(g // 2 + 1) * LANES]
            cb = qb[:, (2 + g // 2) * LANES:(3 + g // 2) * LANES]
            if g % 2 == 0:
                qrow = jnp.where(lo1, ca, pltpu.roll(cb, HEAD_DIM, 1))
            else:
                qrow = jnp.where(lo1, pltpu.roll(ca, HEAD_DIM, 1), cb)
            prod = kn * qrow
            pn = []
            for part, hq in ((jnp.where(lo, prod, 0.0), g), (jnp.where(lo, 0.0, prod), GQA_G + g)):
                s = jnp.sum(part, axis=1, keepdims=True) * scale
                sink = sink_ref[hq]
                m = jnp.maximum(jnp.max(s, axis=0, keepdims=True), sink)
                e = jnp.exp(s - m)
                den = jnp.sum(e, axis=0, keepdims=True) + jnp.exp(sink - m)
                pn.append(e / den)
            o.append(jnp.sum(jnp.where(lo, pn[0], pn[1]) * vn_b, axis=0, keepdims=True))
        sw = lambda x: pltpu.roll(x, HEAD_DIM, 1)
        chunks = [jnp.where(lo1, o[0], sw(o[1])), jnp.where(lo1, o[2], sw(o[3])),
                  jnp.where(lo1, sw(o[0]), o[1]), jnp.where(lo1, sw(o[2]), o[3])]
        outs_ref[1, pl.ds(b, 1), :] = jnp.concatenate(chunks, axis=1)
        return carry

    lax.fori_loop(0, nb, body, 0)


def _branch_sample(zs, lw, cpast, kc, vc, ppast):
    nb = zs.shape[0]
    kv_w = N_KV * HEAD_DIM
    rc, rs1, rs2 = _rope_tables(jnp.full((1,), PAST_LEN, jnp.int32))
    ssc = jnp.repeat(lw["sgu_w"][:, 0, 0], CHUNK).reshape(1, BRANCH_W)
    ssh = jnp.repeat(lw["sgu_b"][:, 0], CHUNK).reshape(1, BRANCH_W)
    vspec = pl.BlockSpec(memory_space=pltpu.VMEM)
    return pl.pallas_call(
        _branch_sample_kernel,
        in_specs=[pl.BlockSpec(memory_space=pltpu.SMEM)] + [vspec] * 15,
        out_specs=[vspec] * 6,
        out_shape=[jax.ShapeDtypeStruct((N_BRANCH, nb, BRANCH_W), F32),
                   jax.ShapeDtypeStruct((CONV_W - 1, nb, BRANCH_W), F32),
                   jax.ShapeDtypeStruct((nb, WINDOW, kv_w), F32),
                   jax.ShapeDtypeStruct((nb, WINDOW, kv_w), F32),
                   jax.ShapeDtypeStruct((POOL_MAX - 1, nb, BRANCH_W), F32),
                   jax.ShapeDtypeStruct((nb, BRANCH_W), F32)],
        scratch_shapes=[pltpu.VMEM((nb, N_HEADS * HEAD_DIM), F32), pltpu.VMEM((nb, kv_w), F32),
                        pltpu.VMEM((nb, kv_w), F32)],
        compiler_params=pltpu.CompilerParams(vmem_limit_bytes=VMEM_LIMIT),
        name="branch_sample",
    )(lw["attn_sinks"], zs, rc, rs1, rs2, lw["conv_w"], lw["sgu_ln_g"].reshape(1, -1),
      lw["sgu_ln_b"].reshape(1, -1), ssc, ssh, lw["pool_w"], lw["pool_scale"].reshape(1, -1),
      cpast, kc, vc, ppast)


def _merge_kernel(xn_ref, outs_ref, wg_ref, wb_ref, o_ref, acc_ref, *, hi):
    nbr = pl.program_id(2)
    gate = jax.nn.sigmoid(_mm(xn_ref[...], wg_ref[...], hi))
    contrib = gate * _mm(outs_ref[...], wb_ref[...], hi)

    @pl.when(nbr == 0)
    def _():
        acc_ref[...] = contrib

    @pl.when(nbr > 0)
    def _():
        acc_ref[...] += contrib

    @pl.when(nbr == N_BRANCH - 1)
    def _():
        o_ref[...] = acc_ref[...].astype(o_ref.dtype)


def _merge(xn, outs, w_in, w_branch, layer, hi):
    n, d = xn.shape
    tm = min(TM_MERGE, n)
    g_blk = G_OFF // TN_MERGE
    per_branch = D_MODEL // TN_MERGE
    return pl.pallas_call(
        functools.partial(_merge_kernel, hi=hi),
        grid=(n // tm, per_branch, N_BRANCH),
        in_specs=[pl.BlockSpec((tm, d), lambda i, j, b: (i, 0)),
                  pl.BlockSpec((None, tm, BRANCH_W), lambda i, j, b: (b, i, 0)),
                  pl.BlockSpec((None, d, TN_MERGE),
                               lambda i, j, b: (layer, 0, g_blk + b * per_branch + j)),
                  pl.BlockSpec((None, None, BRANCH_W, TN_MERGE), lambda i, j, b: (layer, b, 0, j))],
        out_specs=pl.BlockSpec((tm, TN_MERGE), lambda i, j, b: (i, j)),
        out_shape=jax.ShapeDtypeStruct((n, D_MODEL), F32 if hi else BF16),
        scratch_shapes=[pltpu.VMEM((tm, TN_MERGE), F32)],
        compiler_params=_cparams(("parallel", "arbitrary", "arbitrary")),
        name="merge",
    )(xn, outs, w_in, w_branch)


def _wo_kernel(m_ref, w_ref, x_ref, o_ref, *, hi):
    o_ref[...] = x_ref[...] + _mm(m_ref[...], w_ref[...], hi)


def _out_proj(merged, w_o, x, layer, hi):
    n, d = x.shape
    tm = min(TM_WO, n)
    return pl.pallas_call(
        functools.partial(_wo_kernel, hi=hi),
        grid=(n // tm, d // TN_WO),
        in_specs=[pl.BlockSpec((tm, d), lambda i, j: (i, 0)),
                  pl.BlockSpec((None, d, TN_WO), lambda i, j: (layer, 0, j)),
                  pl.BlockSpec((tm, TN_WO), lambda i, j: (i, j))],
        out_specs=pl.BlockSpec((tm, TN_WO), lambda i, j: (i, j)),
        out_shape=jax.ShapeDtypeStruct((n, d), F32),
        compiler_params=_cparams(("parallel", "arbitrary")),
        name="out_proj",
    )(merged, w_o, x)


def _route_rows(y, g_ref, wr_ref, br_ref, carry_ref):
    tm = y.shape[0]
    xn = _rms(y, g_ref[...])
    logits = _dot3(xn, wr_ref[...]) + br_ref[...]
    lane_i = lax.broadcasted_iota(jnp.int32, (tm, ROUTER_LANES), 1)
    lane = lane_i.astype(F32)
    lane_grp = lax.shift_right_logical(lane_i, 3).astype(F32)
    big = float(ROUTER_LANES)
    is_grp = (lane_i >= N_EXPERTS) & (lane_i < N_EXPERTS + N_EXP_GROUPS)
    lg = jnp.where(is_grp, logits, NEG)
    gmax = jnp.max(lg, axis=-1, keepdims=True)
    gsel = jnp.min(jnp.where(lg == gmax, lane - N_EXPERTS, big), axis=-1, keepdims=True)
    wg = 1.0 / jnp.sum(jnp.where(is_grp, jnp.exp(logits - gmax), 0.0), axis=-1, keepdims=True)
    in_grp = (lane_i < N_EXPERTS) & (lane_grp == gsel)
    v1 = jnp.where(in_grp, logits, NEG)
    m1 = jnp.max(v1, axis=-1, keepdims=True)
    i1 = jnp.min(jnp.where(v1 == m1, lane, big), axis=-1, keepdims=True)
    v2 = jnp.where(lane == i1, NEG, v1)
    m2 = jnp.max(v2, axis=-1, keepdims=True)
    i2 = jnp.min(jnp.where(v2 == m2, lane, big), axis=-1, keepdims=True)
    e2 = jnp.exp(m2 - m1)
    w1 = wg / (1.0 + e2)
    w2 = wg * e2 / (1.0 + e2)

    sel1 = lane == i1
    sel2 = lane == i2
    onehot = jnp.where(sel1 | sel2, 1.0, 0.0)
    r_i = lax.broadcasted_iota(jnp.int32, (tm, tm), 0)
    c_i = lax.broadcasted_iota(jnp.int32, (tm, tm), 1)
    before = jnp.where(c_i < r_i, 1.0, 0.0).astype(BF16)
    counts = _dot(before, onehot.astype(BF16)) + carry_ref[0:1, :]
    r1 = jnp.sum(jnp.where(sel1, counts, 0.0), axis=-1, keepdims=True)
    r2 = jnp.sum(jnp.where(sel2, counts, 0.0), axis=-1, keepdims=True)
    carry_ref[...] = carry_ref[...] + jnp.sum(onehot, axis=0, keepdims=True)

    info = jnp.zeros((tm, ROUTER_LANES), F32)
    for k, val in enumerate((i1, i2, r1, r2, w1, w2)):
        info = jnp.where(lane_i == k, val, info)
    return xn, info


def _router_kernel(yp_ref, ys_ref, g_ref, wr_ref, br_ref, xn_ref, info_ref, cnt_ref, carry_ref, *,
                   prompt_tiles):
    i = pl.program_id(0)

    @pl.when(i == 0)
    def _():
        carry_ref[...] = jnp.zeros_like(carry_ref)

    @pl.when(i < prompt_tiles)
    def _():
        xn, info = _route_rows(yp_ref[...], g_ref, wr_ref, br_ref, carry_ref)
        xn_ref[...] = xn
        info_ref[...] = info

    @pl.when(i == prompt_tiles)
    def _():
        nb = ys_ref.shape[0]
        xn, info = _route_rows(ys_ref[...], g_ref, wr_ref, br_ref, carry_ref)
        xn_ref[0:nb, :] = xn
        info_ref[0:nb, :] = info

    cnt_ref[...] = carry_ref[...]


def _router(yp, ys, g, w_router, b_router):
    n_p, d = yp.shape
    nb = ys.shape[0]
    tm = min(TM_NORM, n_p)
    prompt_tiles = n_p // tm
    n = n_p + nb
    return pl.pallas_call(
        functools.partial(_router_kernel, prompt_tiles=prompt_tiles),
        grid=(prompt_tiles + 1,),
        in_specs=[pl.BlockSpec((tm, d), lambda i: (jnp.minimum(i, prompt_tiles - 1), 0)),
                  pl.BlockSpec((nb, d), lambda i: (0, 0)),
                  pl.BlockSpec((1, d), lambda i: (0, 0)),
                  pl.BlockSpec((d, ROUTER_LANES), lambda i: (0, 0)),
                  pl.BlockSpec((1, ROUTER_LANES), lambda i: (0, 0))],
        out_specs=[pl.BlockSpec((tm, d), lambda i: (i, 0)),
                   pl.BlockSpec((tm, ROUTER_LANES), lambda i: (i, 0)),
                   pl.BlockSpec((SUBLANES, ROUTER_LANES), lambda i: (0, 0))],
        out_shape=[jax.ShapeDtypeStruct((n, d), F32),
                   jax.ShapeDtypeStruct((n, ROUTER_LANES), F32),
                   jax.ShapeDtypeStruct((SUBLANES, ROUTER_LANES), F32)],
        scratch_shapes=[pltpu.VMEM((SUBLANES, ROUTER_LANES), F32)],
        compiler_params=_cparams(("arbitrary",)),
        name="router",
    )(yp, ys, g.reshape(1, d), w_router, b_router)


def _row_copy(src, i, dst, j, sem):
    return pltpu.make_async_copy(src.at[pl.ds(i, 1)], dst.at[pl.ds(j, 1)], sem)


def _moe_kernel(te_ref, na_ref, src_ref, x_hbm, wg_ref, wu_ref, wd_ref, ys_ref, buf, sem):
    del te_ref
    g = pl.program_id(0)
    n_act = na_ref[0]

    def gather(tile, slot, start):
        def body(r, c):
            cp = _row_copy(x_hbm, src_ref[tile * TM_MOE + r], buf.at[slot], r, sem.at[slot])
            if start:
                cp.start()
            else:
                cp.wait()
            return c
        lax.fori_loop(0, TM_MOE, body, 0, unroll=8)

    @pl.when(g == 0)
    def _():
        gather(0, 0, True)

    @pl.when(g + 1 < n_act)
    def _():
        gather(g + 1, (g + 1) % 2, True)

    @pl.when(g < n_act)
    def _():
        slot = g % 2
        gather(g, slot, False)
        x = buf[slot].astype(BF16)
        a = _dot(x, wg_ref[...].astype(BF16))
        u = _dot(x, wu_ref[...].astype(BF16))
        h = (a * jax.nn.sigmoid(a)) * u
        ys_ref[...] = _dot(h.astype(BF16), wd_ref[...].astype(BF16))

    @pl.when(g >= n_act)
    def _():
        ys_ref[...] = jnp.zeros_like(ys_ref)


def _moe(tile_expert, n_active, src, xn, w_gate, w_up, w_down, layer):
    d = xn.shape[1]
    n_tiles = src.shape[0] // TM_MOE
    wsel = lambda g, te, na, src: (layer, te[g], 0, 0)
    return pl.pallas_call(
        _moe_kernel,
        grid_spec=pltpu.PrefetchScalarGridSpec(
            num_scalar_prefetch=3, grid=(n_tiles,),
            in_specs=[pl.BlockSpec(memory_space=pl.ANY),
                      pl.BlockSpec((None, None, d, EXPERT_FF), wsel),
                      pl.BlockSpec((None, None, d, EXPERT_FF), wsel),
                      pl.BlockSpec((None, None, EXPERT_FF, d), wsel)],
            out_specs=pl.BlockSpec((TM_MOE, d), lambda g, te, na, src: (g, 0)),
            scratch_shapes=[pltpu.VMEM((2, TM_MOE, d), F32), pltpu.SemaphoreType.DMA((2,))]),
        out_shape=jax.ShapeDtypeStruct((n_tiles * TM_MOE, d), F32),
        compiler_params=_cparams(("arbitrary",)),
        name="moe_ffn",
    )(tile_expert, n_active, src, xn, w_gate, w_up, w_down)


def _moe_sample_kernel(ue_ref, nu_ref, x_ref, info_ref, y_ref, g_ref, wg_ref, wu_ref, wd_ref,
                       xo_ref, xno_ref, acc_ref):
    s = pl.program_id(0)

    @pl.when(s == 0)
    def _():
        acc_ref[...] = y_ref[...]

    @pl.when(s < nu_ref[0])
    def _():
        x = x_ref[...]
        a = _dot3(x, wg_ref[...])
        u = _dot3(x, wu_ref[...])
        out = _dot3((a * jax.nn.sigmoid(a)) * u, wd_ref[...])
        e = ue_ref[s].astype(F32)
        w = (jnp.where(info_ref[:, 0:1] == e, info_ref[:, 4:5], 0.0)
             + jnp.where(info_ref[:, 1:2] == e, info_ref[:, 5:6], 0.0))
        acc_ref[...] += w * out

    @pl.when(s == pl.num_programs(0) - 1)
    def _():
        xo_ref[...] = acc_ref[...]
        xno_ref[...] = _rms(acc_ref[...], g_ref[...])


def _moe_sample(xn2, info, y, g, w_gate, w_up, w_down, layer, row0):
    nb, d = y.shape
    blk0 = row0 // nb
    e_s = info[row0:, 0:2].astype(jnp.int32).reshape(-1)
    used = jnp.zeros((N_EXPERTS,), jnp.int32).at[e_s].set(1)
    n_used = jnp.sum(used)
    order = jnp.argsort(1 - used, stable=True).astype(jnp.int32)
    used_list = jnp.where(jnp.arange(N_EXPERTS) < n_used, order, order[n_used - 1])
    rows = lambda s, ue, nu: (blk0, 0)
    fixed = lambda s, ue, nu: (0, 0)
    wsel = lambda s, ue, nu: (layer, ue[s], 0, 0)
    return pl.pallas_call(
        _moe_sample_kernel,
        grid_spec=pltpu.PrefetchScalarGridSpec(
            num_scalar_prefetch=2, grid=(N_EXPERTS,),
            in_specs=[pl.BlockSpec((nb, d), rows),
                      pl.BlockSpec((nb, ROUTER_LANES), rows),
                      pl.BlockSpec((nb, d), fixed),
                      pl.BlockSpec((1, d), fixed),
                      pl.BlockSpec((None, None, d, EXPERT_FF), wsel),
                      pl.BlockSpec((None, None, d, EXPERT_FF), wsel),
                      pl.BlockSpec((None, None, EXPERT_FF, d), wsel)],
            out_specs=[pl.BlockSpec((nb, d), fixed), pl.BlockSpec((nb, d), fixed)],
            scratch_shapes=[pltpu.VMEM((nb, d), F32)]),
        out_shape=[jax.ShapeDtypeStruct((nb, d), F32), jax.ShapeDtypeStruct((nb, d), F32)],
        compiler_params=_cparams(("arbitrary",)),
        name="moe_sample",
    )(used_list, n_used.reshape(1).astype(jnp.int32), xn2, info, y, g.reshape(1, d),
      w_gate, w_up, w_down)


def _combine_kernel(pos_ref, y_ref, info_ref, g_ref, ys_hbm, *rest, rows, write_x, blk0):
    if write_x:
        xo_ref, xno_ref, buf0, buf1, sem = rest
    else:
        xno_ref, buf0, buf1, sem = rest
    base = (blk0 + pl.program_id(0)) * rows

    def issue(r, c):
        t = base + r
        _row_copy(ys_hbm, pos_ref[2 * t], buf0, r, sem).start()
        _row_copy(ys_hbm, pos_ref[2 * t + 1], buf1, r, sem).start()
        return c

    def drain(r, c):
        t = base + r
        _row_copy(ys_hbm, pos_ref[2 * t], buf0, r, sem).wait()
        _row_copy(ys_hbm, pos_ref[2 * t + 1], buf1, r, sem).wait()
        return c

    lax.fori_loop(0, rows, issue, 0)
    lax.fori_loop(0, rows, drain, 0)
    xnew = y_ref[...] + info_ref[:, 4:5] * buf0[...] + info_ref[:, 5:6] * buf1[...]
    if write_x:
        xo_ref[...] = xnew
    xno_ref[...] = _rms(xnew, g_ref[...]).astype(xno_ref.dtype)


def _combine(pos, y, info, g, ys, norm_dtype, write_x, row0):
    n, d = y.shape
    rows = min(TM_MOE, n)
    assert row0 % rows == 0
    blk0 = row0 // rows
    tile = pl.BlockSpec((rows, d), lambda i, pos: (i, 0))
    out_specs = [tile]
    out_shape = [jax.ShapeDtypeStruct((n, d), norm_dtype)]
    if write_x:
        out_specs = [tile] + out_specs
        out_shape = [jax.ShapeDtypeStruct((n, d), F32)] + out_shape
    return pl.pallas_call(
        functools.partial(_combine_kernel, rows=rows, write_x=write_x, blk0=blk0),
        grid_spec=pltpu.PrefetchScalarGridSpec(
            num_scalar_prefetch=1, grid=(n // rows,),
            in_specs=[tile,
                      pl.BlockSpec((rows, ROUTER_LANES), lambda i, pos: (blk0 + i, 0)),
                      pl.BlockSpec((1, d), lambda i, pos: (0, 0)),
                      pl.BlockSpec(memory_space=pl.ANY)],
            out_specs=out_specs,
            scratch_shapes=[pltpu.VMEM((rows, d), F32), pltpu.VMEM((rows, d), F32),
                            pltpu.SemaphoreType.DMA(())]),
        out_shape=out_shape,
        compiler_params=_cparams(("arbitrary",)),
        name="combine",
    )(pos, y, info, g.reshape(1, d), ys)


def _plan(info, cnt, n_tiles):
    count = cnt[0, :N_EXPERTS].astype(jnp.int32)
    tiles = (count + TM_MOE - 1) // TM_MOE
    tile_end = jnp.cumsum(tiles)
    row_off = (tile_end - tiles) * TM_MOE
    n_active = tile_end[-1]
    pos = (row_off[info[:, 0:2].astype(jnp.int32)] + info[:, 2:4].astype(jnp.int32)).reshape(-1)
    token = jnp.arange(pos.shape[0], dtype=jnp.int32) // 2
    src = jnp.zeros((n_tiles * TM_MOE,), jnp.int32).at[pos].set(token, unique_indices=True)
    g = jnp.minimum(jnp.arange(n_tiles, dtype=jnp.int32), n_active - 1)
    tile_expert = jnp.sum((g[:, None] >= tile_end[None, :]).astype(jnp.int32), axis=1)
    return pos, src, tile_expert.astype(jnp.int32), n_active.reshape(1).astype(jnp.int32)


def kernel(x_prompt, x_sample, state_conv, cache_win_k, cache_win_v, state_pool, norm1_g, w_in,
           conv_w, attn_sinks, sgu_ln_g, sgu_ln_b, sgu_w, sgu_b, pool_w, pool_scale, w_branch, w_o,
           norm2_g, router_group_w, router_group_b, router_expert_w, router_expert_b, moe_w_gate,
           moe_w_up, moe_w_down, final_norm_g):
    batch, seq, d = x_prompt.shape
    assert seq % TB == 0 and seq >= 2 * TB and d == D_MODEL
    nb = x_sample.shape[0]
    depth = w_in.shape[0]
    n_p = batch * seq
    kv_w = N_KV * HEAD_DIM
    n_slots = 2 * (n_p + nb)
    n_tiles = (n_slots + N_EXPERTS * (TM_MOE - 1) + TM_MOE - 1) // TM_MOE

    xp = x_prompt.reshape(n_p, d)
    xs_ = x_sample.reshape(nb, d)
    xn_p = _rmsnorm(xp, norm1_g[0], BF16)
    xn_s = _rmsnorm(xs_, norm1_g[0], F32)

    pad = ROUTER_LANES - N_EXPERTS - N_EXP_GROUPS
    conv_p, k_p, v_p, pool_p = [], [], [], []
    conv_s, k_s, v_s, pool_s, chunk_s = [], [], [], [], []
    y_prompt = y_sample = None
    for l in range(depth):
        lw = dict(attn_sinks=attn_sinks[l], conv_w=conv_w[l], sgu_ln_g=sgu_ln_g[l],
                  sgu_ln_b=sgu_ln_b[l], sgu_w=sgu_w[l], sgu_b=sgu_b[l], pool_w=pool_w[l],
                  pool_scale=pool_scale[l])
        w_router = jnp.concatenate(
            [router_expert_w[l], router_group_w[l], jnp.zeros((d, pad), F32)], axis=1)
        b_router = jnp.concatenate(
            [router_expert_b[l], router_group_b[l], jnp.zeros((pad,), F32)]).reshape(1, -1)
        last = l == depth - 1
        g_next = final_norm_g if last else norm1_g[l + 1]

        zs_p = _in_proj(xn_p, w_in, l, False)
        outs_p, c_st, k_st, v_st, p_st = _branch_prompt(zs_p, batch, seq, lw)
        conv_p.append(c_st)
        k_p.append(k_st.reshape(batch, WINDOW, N_KV, HEAD_DIM))
        v_p.append(v_st.reshape(batch, WINDOW, N_KV, HEAD_DIM))
        pool_p.append(p_st)
        merged_p = _merge(xn_p, outs_p, w_in, w_branch, l, False)
        yp = _out_proj(merged_p, w_o, xp, l, False)
        if not last:
            x_t = xp.reshape(batch, seq, d)[:, seq - 2 * TB:].reshape(batch * 2 * TB, d)
            xn_t = _rmsnorm(x_t, norm1_g[l], F32)
            zs_t = _in_proj(xn_t, w_in, l, True)
            outs_t = _branch_prompt(zs_t, batch, seq, lw, hi=True, t_start=seq // TB - 2)[0]
            second = lambda a: a.reshape(batch, 2, TB, a.shape[-1])[:, 1].reshape(batch * TB, -1)
            outs_t = jnp.stack([second(outs_t[b]) for b in range(N_BRANCH)])
            merged_t = _merge(second(xn_t), outs_t, w_in, w_branch, l, True)
            y_t = _out_proj(merged_t, w_o, second(x_t), l, True)
            yp = yp.reshape(batch, seq, d).at[:, seq - TB:].set(
                y_t.reshape(batch, TB, d)).reshape(n_p, d)

        zs_s = _in_proj(xn_s, w_in, l, True)
        outs_s, c_st, k_st, v_st, p_st, cv = _branch_sample(
            zs_s, lw, jnp.swapaxes(state_conv[l], 0, 1),
            cache_win_k[l].reshape(nb, WINDOW, kv_w), cache_win_v[l].reshape(nb, WINDOW, kv_w),
            jnp.swapaxes(state_pool[l], 0, 1))
        conv_s.append(jnp.swapaxes(c_st, 0, 1))
        k_s.append(k_st.reshape(nb, WINDOW, N_KV, HEAD_DIM))
        v_s.append(v_st.reshape(nb, WINDOW, N_KV, HEAD_DIM))
        pool_s.append(jnp.swapaxes(p_st, 0, 1))
        chunk_s.append(cv.reshape(nb, 1, BRANCH_W))
        merged_s = _merge(xn_s, outs_s, w_in, w_branch, l, True)
        ys_ = _out_proj(merged_s, w_o, xs_, l, True)

        xn2, info, cnt = _router(yp, ys_, norm2_g[l], w_router, b_router)
        pos, src, tile_expert, n_active = _plan(info, cnt, n_tiles)
        ffn = _moe(tile_expert, n_active, src, xn2, moe_w_gate, moe_w_up, moe_w_down, l)
        if last:
            (y_prompt,) = _combine(pos, yp, info, g_next, ffn, F32, False, 0)
            (y_sample,) = _combine(pos, ys_, info, g_next, ffn, F32, False, n_p)
        else:
            xp, xn_p = _combine(pos, yp, info, g_next, ffn, BF16, True, 0)
            xs_, xn_s = _moe_sample(xn2, info, ys_, g_next, moe_w_gate, moe_w_up, moe_w_down, l, n_p)

    return (y_prompt.reshape(batch, seq, d), y_sample.reshape(nb, 1, d),
            jnp.stack(conv_p), jnp.stack(k_p), jnp.stack(v_p), jnp.stack(pool_p),
            jnp.stack(conv_s), jnp.stack(k_s), jnp.stack(v_s), jnp.stack(pool_s),
            jnp.stack(chunk_s))
```

```python
import functools

import jax
import jax.numpy as jnp
from jax import lax
from jax.experimental import pallas as pl
from jax.experimental.pallas import tpu as pltpu

F32 = jnp.float32
BF16 = jnp.bfloat16

D_MODEL = 2048
BRANCH_W = 512
N_BRANCH = 4
CONV_W = 3
HEAD_DIM = 64
N_HEADS = 8
N_KV = 2
GQA_G = 4
WINDOW = 128
ROPE_THETA = 500000.0
ROPE_DIM = 16
CHUNK = 128
SGU_GROUPS = 4
POOL_WINDOWS = (2, 4, 8, 16)
POOL_MAX = 16
N_EXP_GROUPS = 4
EXP_PER_GROUP = 8
N_EXPERTS = 32
EXPERT_FF = 512
EPS = 1e-6
PAST_LEN = 16384
NEG = -3.0e38

A_H_OFF = 0
A_C_OFF = 512
A_B_OFF = 1024
Q_OFF = 1536
K_OFF = 2048
V_OFF = 2176
C_OFF = 2304
D_OFF = 3328
G_OFF = 3840

LANES = 128
SUBLANES = 8
VMEM_LIMIT = 52 * 1024 * 1024

TM_NORM = 512
TM_PROJ = 1024
TN_PROJ = 768
TM_MERGE = 2048
TN_MERGE = 256
TM_WO = 2048
TN_WO = 512
TM_MOE = 288
TM_COMBINE = 256
TB = 128
ROUTER_LANES = 128


def _cparams(sem):
    return pltpu.CompilerParams(dimension_semantics=sem, vmem_limit_bytes=VMEM_LIMIT)


_NN = (((1,), (0,)), ((), ()))
_NT = (((1,), (1,)), ((), ()))


def _dot(a, b, dims=_NN):
    return lax.dot_general(a, b, dims, preferred_element_type=F32)


def _split(x):
    hi = x.astype(BF16)
    lo = (x - hi.astype(F32)).astype(BF16)
    return hi, lo


def _dot3(a, b, dims=_NN):
    ah, al = _split(a)
    bh, bl = _split(b)
    return _dot(ah, bh, dims) + _dot(al, bh, dims) + _dot(ah, bl, dims)


def _mm(a, w, hi, dims=_NN):
    if hi:
        return _dot3(a.astype(F32), w.astype(F32), dims)
    return _dot(a.astype(BF16), w.astype(BF16), dims)


def _rms(x, g):
    return (x * lax.rsqrt(jnp.mean(x * x, axis=-1, keepdims=True) + EPS)) * g


def _rmsnorm_kernel(x_ref, g_ref, o_ref):
    o_ref[...] = _rms(x_ref[...], g_ref[...]).astype(o_ref.dtype)


def _rmsnorm(x, g, out_dtype):
    n, d = x.shape
    tm = min(TM_NORM, n)
    return pl.pallas_call(
        _rmsnorm_kernel,
        grid=(n // tm,),
        in_specs=[pl.BlockSpec((tm, d), lambda i: (i, 0)),
                  pl.BlockSpec((1, d), lambda i: (0, 0))],
        out_specs=pl.BlockSpec((tm, d), lambda i: (i, 0)),
        out_shape=jax.ShapeDtypeStruct((n, d), out_dtype),
        compiler_params=_cparams(("parallel",)),
        name="rmsnorm",
    )(x, g.reshape(1, d))


def _proj_kernel(x_ref, w_ref, o_ref, *, hi):
    o_ref[...] = _mm(x_ref[...], w_ref[...], hi)


def _in_proj(xn, w_in, layer, hi):
    n, d = xn.shape
    tm = min(TM_PROJ // 2 if hi else TM_PROJ, n)
    return pl.pallas_call(
        functools.partial(_proj_kernel, hi=hi),
        grid=(n // tm, G_OFF // TN_PROJ),
        in_specs=[pl.BlockSpec((tm, d), lambda i, j: (i, 0)),
                  pl.BlockSpec((None, d, TN_PROJ), lambda i, j: (layer, 0, j))],
        out_specs=pl.BlockSpec((tm, TN_PROJ), lambda i, j: (i, j)),
        out_shape=jax.ShapeDtypeStruct((n, G_OFF), F32),
        compiler_params=_cparams(("parallel", "arbitrary")),
        name="in_proj",
    )(xn, w_in)


def _rope_tables(pos):
    half = ROPE_DIM // 2
    inv = jnp.power(jnp.float32(ROPE_THETA), -jnp.arange(half, dtype=F32) * (2.0 / ROPE_DIM))
    ang = pos.astype(F32)[:, None] * inv[None, :]
    cos, sin = jnp.cos(ang), jnp.sin(ang)
    t = pos.shape[0]
    rest = HEAD_DIM - ROPE_DIM
    c = jnp.concatenate([cos, cos, jnp.ones((t, rest), F32)], axis=1)
    s1 = jnp.concatenate([-sin, jnp.zeros((t, HEAD_DIM - half), F32)], axis=1)
    s2 = jnp.concatenate([jnp.zeros((t, half), F32), sin, jnp.zeros((t, rest), F32)], axis=1)
    rep = LANES // HEAD_DIM
    return jnp.tile(c, (1, rep)), jnp.tile(s1, (1, rep)), jnp.tile(s2, (1, rep))


def _rope(x, c, s1, s2):
    half = ROPE_DIM // 2
    return x * c + pltpu.roll(x, LANES - half, 1) * s1 + pltpu.roll(x, half, 1) * s2


def _gelu(x):
    return 0.5 * x * (1.0 + lax.erf(x * 0.7071067811865476))


def _layernorm(v, g, b):
    mu = jnp.mean(v, axis=-1, keepdims=True)
    vc = v - mu
    return vc * lax.rsqrt(jnp.mean(vc * vc, axis=-1, keepdims=True) + EPS) * g + b


def _branch_prompt_kernel(sink_ref, z_ref, rc_ref, rs1_ref, rs2_ref, cw_ref, lng_ref, lnb_ref,
                          sw_ref, sbt_ref, pw_ref, ps_ref,
                          outs_ref, conv_ref, kst_ref, vst_ref, pool_ref,
                          kprev, vprev, ci_ext, p_ext, *, hi, t_start):
    tb = pl.program_id(1)
    halo_c = SUBLANES
    halo_p = POOL_MAX

    @pl.when(tb == 0)
    def _():
        kprev[...] = jnp.zeros_like(kprev)
        vprev[...] = jnp.zeros_like(vprev)
        ci_ext[0:halo_c, :] = jnp.zeros((halo_c, BRANCH_W), F32)
        p_ext[0:halo_p, :] = jnp.zeros((halo_p, BRANCH_W), F32)

    ci = z_ref[:, A_C_OFF:A_B_OFF] * z_ref[:, A_H_OFF:A_C_OFF]
    ci_ext[halo_c:halo_c + TB, :] = ci
    y = cw_ref[2:3, :] * ci
    for j in range(CONV_W - 1):
        y = y + cw_ref[j:j + 1, :] * ci_ext[pl.ds(halo_c - (CONV_W - 1) + j, TB), :]
    outs_ref[0] = (z_ref[:, A_B_OFF:Q_OFF] * y).astype(outs_ref.dtype)
    conv_ref[...] = ci_ext[pl.ds(halo_c + TB - (CONV_W - 1), CONV_W - 1), :]
    ci_ext[0:halo_c, :] = ci[TB - halo_c:, :]

    p = z_ref[:, D_OFF:G_OFF]
    p_ext[halo_p:halo_p + TB, :] = p
    posf = ((t_start + tb) * TB + lax.broadcasted_iota(jnp.int32, (TB, 1), 0)).astype(F32)
    for g, w in enumerate(POOL_WINDOWS):
        sl = slice(g * LANES, (g + 1) * LANES)
        s = p[:, sl]
        for k in range(1, w):
            s = s + p_ext[pl.ds(halo_p - k, TB), sl]
        dlt = s / jnp.minimum(posf + 1.0, float(w)) - p[:, sl]
        yg = _mm(dlt, pw_ref[g], hi) * ps_ref[:, sl]
        outs_ref[3, :, sl] = yg.astype(outs_ref.dtype)
    pool_ref[...] = p_ext[pl.ds(halo_p + TB - (POOL_MAX - 1), POOL_MAX - 1), :]
    p_ext[0:halo_p, :] = p[TB - halo_p:, :]

    ge = _gelu(z_ref[:, C_OFF:D_OFF])
    u = ge[:, :BRANCH_W]
    vn = _layernorm(ge[:, BRANCH_W:], lng_ref[...], lnb_ref[...])
    r_i = lax.broadcasted_iota(jnp.int32, (TB, TB), 0)
    c_i = lax.broadcasted_iota(jnp.int32, (TB, TB), 1)
    for g in range(SGU_GROUPS):
        sl = slice(g * LANES, (g + 1) * LANES)
        wt = jnp.where(c_i <= r_i, sw_ref[g], 0.0)
        f = _mm(wt, vn[:, sl], hi) + sbt_ref[:, g:g + 1]
        outs_ref[2, :, sl] = (u[:, sl] * f).astype(outs_ref.dtype)

    rc, rs1, rs2 = rc_ref[...], rs1_ref[...], rs2_ref[...]
    kr = _rope(z_ref[:, K_OFF:V_OFF], rc, rs1, rs2)
    v = z_ref[:, V_OFF:C_OFF]
    qr = [_rope(z_ref[:, Q_OFF + c * LANES:Q_OFF + (c + 1) * LANES], rc, rs1, rs2)
          for c in range(N_HEADS * HEAD_DIM // LANES)]
    qi = lax.broadcasted_iota(jnp.int32, (TB, 2 * TB), 0)
    kj = lax.broadcasted_iota(jnp.int32, (TB, 2 * TB), 1)
    allowed = (kj > qi) & (kj <= qi + WINDOW) & ((kj >= TB) | (tb > 0))
    kp, vp = kprev[...], vprev[...]
    heads = [None] * N_HEADS
    for hk in range(N_KV):
        hs = slice(hk * HEAD_DIM, (hk + 1) * HEAD_DIM)
        kk = jnp.concatenate([kp[:, hs], kr[:, hs]], axis=0)
        vv = jnp.concatenate([vp[:, hs], v[:, hs]], axis=0)
        if not hi:
            kk, vv = kk.astype(BF16), vv.astype(BF16)
        for g in range(GQA_G):
            hq = hk * GQA_G + g
            per = LANES // HEAD_DIM
            qh = qr[hq // per][:, (hq % per) * HEAD_DIM:(hq % per + 1) * HEAD_DIM]
            s = _mm(qh, kk, hi, _NT) * (HEAD_DIM ** -0.5)
            s = jnp.where(allowed, s, -1e30)
            sink = sink_ref[hq]
            m = jnp.maximum(jnp.max(s, axis=-1, keepdims=True), sink)
            e = jnp.exp(s - m)
            den = jnp.sum(e, axis=-1, keepdims=True) + jnp.exp(sink - m)
            heads[hq] = _mm(e / den, vv, hi)
    outs_ref[1] = jnp.concatenate(heads, axis=1).astype(outs_ref.dtype)
    kst_ref[...] = kr
    vst_ref[...] = v
    kprev[...] = kr
    vprev[...] = v


def _branch_prompt(zs, batch, seq, lw, hi=False, t_start=0):
    n = zs.shape[0]
    nt = n // (batch * TB)
    rc, rs1, rs2 = _rope_tables(jnp.arange(seq))
    full = lambda shape: pl.BlockSpec(shape, lambda b, t: (0,) * len(shape))
    tab = pl.BlockSpec((TB, LANES), lambda b, t: (t_start + t, 0))
    kv_w = N_KV * HEAD_DIM
    return pl.pallas_call(
        functools.partial(_branch_prompt_kernel, hi=hi, t_start=t_start),
        grid=(batch, nt),
        in_specs=[pl.BlockSpec(memory_space=pltpu.SMEM),
                  pl.BlockSpec((TB, G_OFF), lambda b, t: (b * nt + t, 0)),
                  tab, tab, tab,
                  full((CONV_W, BRANCH_W)), full((1, BRANCH_W)), full((1, BRANCH_W)),
                  full((SGU_GROUPS, CHUNK, CHUNK)), full((CHUNK, SGU_GROUPS)),
                  full((len(POOL_WINDOWS), LANES, LANES)), full((1, BRANCH_W))],
        out_specs=[pl.BlockSpec((N_BRANCH, TB, BRANCH_W), lambda b, t: (0, b * nt + t, 0)),
                   pl.BlockSpec((None, CONV_W - 1, BRANCH_W), lambda b, t: (b, 0, 0)),
                   pl.BlockSpec((None, WINDOW, kv_w), lambda b, t: (b, 0, 0)),
                   pl.BlockSpec((None, WINDOW, kv_w), lambda b, t: (b, 0, 0)),
                   pl.BlockSpec((None, POOL_MAX - 1, BRANCH_W), lambda b, t: (b, 0, 0))],
        out_shape=[jax.ShapeDtypeStruct((N_BRANCH, n, BRANCH_W), F32 if hi else BF16),
                   jax.ShapeDtypeStruct((batch, CONV_W - 1, BRANCH_W), F32),
                   jax.ShapeDtypeStruct((batch, WINDOW, kv_w), F32),
                   jax.ShapeDtypeStruct((batch, WINDOW, kv_w), F32),
                   jax.ShapeDtypeStruct((batch, POOL_MAX - 1, BRANCH_W), F32)],
        scratch_shapes=[pltpu.VMEM((TB, kv_w), F32), pltpu.VMEM((TB, kv_w), F32),
                        pltpu.VMEM((SUBLANES + TB, BRANCH_W), F32),
                        pltpu.VMEM((POOL_MAX + TB, BRANCH_W), F32)],
        compiler_params=_cparams(("arbitrary", "arbitrary")),
        name="branch_prompt",
    )(lw["attn_sinks"], zs, rc, rs1, rs2, lw["conv_w"], lw["sgu_ln_g"].reshape(1, -1),
      lw["sgu_ln_b"].reshape(1, -1), lw["sgu_w"], lw["sgu_b"].T, lw["pool_w"],
      lw["pool_scale"].reshape(1, -1))


def _branch_sample_kernel(sink_ref, z_ref, rc_ref, rs1_ref, rs2_ref, cw_ref, lng_ref, lnb_ref,
                          ssc_ref, ssh_ref, pw_ref, ps_ref, cpast_ref, kc_ref, vc_ref, ppast_ref,
                          outs_ref, conv_ref, kst_ref, vst_ref, pool_ref, chunkv_ref,
                          qrot, krot, vnew):
    nb = z_ref.shape[0]

    ci = z_ref[:, A_C_OFF:A_B_OFF] * z_ref[:, A_H_OFF:A_C_OFF]
    y = cw_ref[CONV_W - 1:CONV_W, :] * ci
    for j in range(CONV_W - 1):
        y = y + cw_ref[j:j + 1, :] * cpast_ref[j]
    outs_ref[0] = z_ref[:, A_B_OFF:Q_OFF] * y
    for j in range(CONV_W - 2):
        conv_ref[j] = cpast_ref[j + 1]
    conv_ref[CONV_W - 2] = ci

    p = z_ref[:, D_OFF:G_OFF]
    for g, w in enumerate(POOL_WINDOWS):
        sl = slice(g * LANES, (g + 1) * LANES)
        s = p[:, sl]
        for k in range(1, w):
            s = s + ppast_ref[POOL_MAX - 1 - k, :, sl]
        dlt = s / float(min(PAST_LEN + 1, w)) - p[:, sl]
        outs_ref[3, :, sl] = _dot3(dlt, pw_ref[g]) * ps_ref[:, sl]
    for j in range(POOL_MAX - 2):
        pool_ref[j] = ppast_ref[j + 1]
    pool_ref[POOL_MAX - 2] = p

    ge = _gelu(z_ref[:, C_OFF:D_OFF])
    vn = _layernorm(ge[:, BRANCH_W:], lng_ref[...], lnb_ref[...])
    outs_ref[2] = ge[:, :BRANCH_W] * (vn * ssc_ref[...] + ssh_ref[...])
    chunkv_ref[...] = vn

    rc, rs1, rs2 = rc_ref[...], rs1_ref[...], rs2_ref[...]
    krot[...] = _rope(z_ref[:, K_OFF:V_OFF], rc, rs1, rs2)
    vnew[...] = z_ref[:, V_OFF:C_OFF]
    for c in range(N_HEADS * HEAD_DIM // LANES):
        cs = slice(c * LANES, (c + 1) * LANES)
        qrot[:, cs] = _rope(z_ref[:, Q_OFF + c * LANES:Q_OFF + (c + 1) * LANES], rc, rs1, rs2)
    row = lax.broadcasted_iota(jnp.int32, (WINDOW, LANES), 0)
    lo = lax.broadcasted_iota(jnp.int32, (WINDOW, LANES), 1) < HEAD_DIM
    lo1 = lax.broadcasted_iota(jnp.int32, (1, LANES), 1) < HEAD_DIM
    scale = HEAD_DIM ** -0.5

    def body(b, carry):
        kn = jnp.where(row == WINDOW - 1, krot[pl.ds(b, 1), :], pltpu.roll(kc_ref[b], WINDOW - 1, 0))
        vn_b = jnp.where(row == WINDOW - 1, vnew[pl.ds(b, 1), :], pltpu.roll(vc_ref[b], WINDOW - 1, 0))
        kst_ref[b] = kn
        vst_ref[b] = vn_b
        qb = qrot[pl.ds(b, 1), :]
        o = []
        for g in range(GQA_G):
            ca = qb[:, (g // 2) * LANES:(g // 2 + 1) * LANES]
            cb = qb[:, (2 + g // 2) * LANES:(3 + g // 2) * LANES]
            if g % 2 == 0:
                qrow = jnp.where(lo1, ca, pltpu.roll(cb, HEAD_DIM, 1))
            else:
                qrow = jnp.where(lo1, pltpu.roll(ca, HEAD_DIM, 1), cb)
            prod = kn * qrow
            pn = []
            for part, hq in ((jnp.where(lo, prod, 0.0), g), (jnp.where(lo, 0.0, prod), GQA_G + g)):
                s = jnp.sum(part, axis=1, keepdims=True) * scale
                sink = sink_ref[hq]
                m = jnp.maximum(jnp.max(s, axis=0, keepdims=True), sink)
                e = jnp.exp(s - m)
                den = jnp.sum(e, axis=0, keepdims=True) + jnp.exp(sink - m)
                pn.append(e / den)
            o.append(jnp.sum(jnp.where(lo, pn[0], pn[1]) * vn_b, axis=0, keepdims=True))
        sw = lambda x: pltpu.roll(x, HEAD_DIM, 1)
        chunks = [jnp.where(lo1, o[0], sw(o[1])), jnp.where(lo1, o[2], sw(o[3])),
                  jnp.where(lo1, sw(o[0]), o[1]), jnp.where(lo1, sw(o[2]), o[3])]
        outs_ref[1, pl.ds(b, 1), :] = jnp.concatenate(chunks, axis=1)
        return carry

    lax.fori_loop(0, nb, body, 0)


def _branch_sample(zs, lw, cpast, kc, vc, ppast):
    nb = zs.shape[0]
    kv_w = N_KV * HEAD_DIM
    rc, rs1, rs2 = _rope_tables(jnp.full((1,), PAST_LEN, jnp.int32))
    ssc = jnp.repeat(lw["sgu_w"][:, 0, 0], CHUNK).reshape(1, BRANCH_W)
    ssh = jnp.repeat(lw["sgu_b"][:, 0], CHUNK).reshape(1, BRANCH_W)
    vspec = pl.BlockSpec(memory_space=pltpu.VMEM)
    return pl.pallas_call(
        _branch_sample_kernel,
        in_specs=[pl.BlockSpec(memory_space=pltpu.SMEM)] + [vspec] * 15,
        out_specs=[vspec] * 6,
        out_shape=[jax.ShapeDtypeStruct((N_BRANCH, nb, BRANCH_W), F32),
                   jax.ShapeDtypeStruct((CONV_W - 1, nb, BRANCH_W), F32),
                   jax.ShapeDtypeStruct((nb, WINDOW, kv_w), F32),
                   jax.ShapeDtypeStruct((nb, WINDOW, kv_w), F32),
                   jax.ShapeDtypeStruct((POOL_MAX - 1, nb, BRANCH_W), F32),
                   jax.ShapeDtypeStruct((nb, BRANCH_W), F32)],
        scratch_shapes=[pltpu.VMEM((nb, N_HEADS * HEAD_DIM), F32), pltpu.VMEM((nb, kv_w), F32),
                        pltpu.VMEM((nb, kv_w), F32)],
        compiler_params=pltpu.CompilerParams(vmem_limit_bytes=VMEM_LIMIT),
        name="branch_sample",
    )(lw["attn_sinks"], zs, rc, rs1, rs2, lw["conv_w"], lw["sgu_ln_g"].reshape(1, -1),
      lw["sgu_ln_b"].reshape(1, -1), ssc, ssh, lw["pool_w"], lw["pool_scale"].reshape(1, -1),
      cpast, kc, vc, ppast)


def _merge_kernel(xn_ref, outs_ref, wg_ref, wb_ref, o_ref, acc_ref, *, hi):
    nbr = pl.program_id(2)
    gate = jax.nn.sigmoid(_mm(xn_ref[...], wg_ref[...], hi))
    contrib = gate * _mm(outs_ref[...], wb_ref[...], hi)

    @pl.when(nbr == 0)
    def _():
        acc_ref[...] = contrib

    @pl.when(nbr > 0)
    def _():
        acc_ref[...] += contrib

    @pl.when(nbr == N_BRANCH - 1)
    def _():
        o_ref[...] = acc_ref[...].astype(o_ref.dtype)


def _merge(xn, outs, w_in, w_branch, layer, hi):
    n, d = xn.shape
    tm = min(TM_MERGE, n)
    g_blk = G_OFF // TN_MERGE
    per_branch = D_MODEL // TN_MERGE
    return pl.pallas_call(
        functools.partial(_merge_kernel, hi=hi),
        grid=(n // tm, per_branch, N_BRANCH),
        in_specs=[pl.BlockSpec((tm, d), lambda i, j, b: (i, 0)),
                  pl.BlockSpec((None, tm, BRANCH_W), lambda i, j, b: (b, i, 0)),
                  pl.BlockSpec((None, d, TN_MERGE),
                               lambda i, j, b: (layer, 0, g_blk + b * per_branch + j)),
                  pl.BlockSpec((None, None, BRANCH_W, TN_MERGE), lambda i, j, b: (layer, b, 0, j))],
        out_specs=pl.BlockSpec((tm, TN_MERGE), lambda i, j, b: (i, j)),
        out_shape=jax.ShapeDtypeStruct((n, D_MODEL), F32 if hi else BF16),
        scratch_shapes=[pltpu.VMEM((tm, TN_MERGE), F32)],
        compiler_params=_cparams(("parallel", "arbitrary", "arbitrary")),
        name="merge",
    )(xn, outs, w_in, w_branch)


def _wo_kernel(m_ref, w_ref, x_ref, o_ref, *, hi):
    o_ref[...] = x_ref[...] + _mm(m_ref[...], w_ref[...], hi)


def _out_proj(merged, w_o, x, layer, hi):
    n, d = x.shape
    tm = min(TM_WO, n)
    return pl.pallas_call(
        functools.partial(_wo_kernel, hi=hi),
        grid=(n // tm, d // TN_WO),
        in_specs=[pl.BlockSpec((tm, d), lambda i, j: (i, 0)),
                  pl.BlockSpec((None, d, TN_WO), lambda i, j: (layer, 0, j)),
                  pl.BlockSpec((tm, TN_WO), lambda i, j: (i, j))],
        out_specs=pl.BlockSpec((tm, TN_WO), lambda i, j: (i, j)),
        out_shape=jax.ShapeDtypeStruct((n, d), F32),
        compiler_params=_cparams(("parallel", "arbitrary")),
        name="out_proj",
    )(merged, w_o, x)


def _put_rows_kernel(y_hbm, t_ref, o_ref):
    del y_hbm
    o_ref[...] = t_ref[...]


def _put_last_blocks(y, y_tail, batch, seq):
    d = y.shape[1]
    nt = seq // TB
    return pl.pallas_call(
        _put_rows_kernel,
        grid=(batch,),
        in_specs=[pl.BlockSpec(memory_space=pl.ANY),
                  pl.BlockSpec((TB, d), lambda b: (b, 0))],
        out_specs=pl.BlockSpec((TB, d), lambda b: (b * nt + nt - 1, 0)),
        out_shape=jax.ShapeDtypeStruct(y.shape, y.dtype),
        input_output_aliases={0: 0},
        compiler_params=_cparams(("arbitrary",)),
        name="put_last_blocks",
    )(y, y_tail)


def _route_rows(y, g_ref, wr_ref, br_ref, carry_ref):
    tm = y.shape[0]
    xn = _rms(y, g_ref[...])
    logits = _dot3(xn, wr_ref[...]) + br_ref[...]
    lane_i = lax.broadcasted_iota(jnp.int32, (tm, ROUTER_LANES), 1)
    lane = lane_i.astype(F32)
    lane_grp = lax.shift_right_logical(lane_i, 3).astype(F32)
    big = float(ROUTER_LANES)
    is_grp = (lane_i >= N_EXPERTS) & (lane_i < N_EXPERTS + N_EXP_GROUPS)
    lg = jnp.where(is_grp, logits, NEG)
    gmax = jnp.max(lg, axis=-1, keepdims=True)
    gsel = jnp.min(jnp.where(lg == gmax, lane - N_EXPERTS, big), axis=-1, keepdims=True)
    wg = 1.0 / jnp.sum(jnp.where(is_grp, jnp.exp(logits - gmax), 0.0), axis=-1, keepdims=True)
    in_grp = (lane_i < N_EXPERTS) & (lane_grp == gsel)
    v1 = jnp.where(in_grp, logits, NEG)
    m1 = jnp.max(v1, axis=-1, keepdims=True)
    i1 = jnp.min(jnp.where(v1 == m1, lane, big), axis=-1, keepdims=True)
    v2 = jnp.where(lane == i1, NEG, v1)
    m2 = jnp.max(v2, axis=-1, keepdims=True)
    i2 = jnp.min(jnp.where(v2 == m2, lane, big), axis=-1, keepdims=True)
    e2 = jnp.exp(m2 - m1)
    w1 = wg / (1.0 + e2)
    w2 = wg * e2 / (1.0 + e2)

    sel1 = lane == i1
    sel2 = lane == i2
    onehot = jnp.where(sel1 | sel2, 1.0, 0.0)
    r_i = lax.broadcasted_iota(jnp.int32, (tm, tm), 0)
    c_i = lax.broadcasted_iota(jnp.int32, (tm, tm), 1)
    before = jnp.where(c_i < r_i, 1.0, 0.0).astype(BF16)
    counts = _dot(before, onehot.astype(BF16)) + carry_ref[0:1, :]
    r1 = jnp.sum(jnp.where(sel1, counts, 0.0), axis=-1, keepdims=True)
    r2 = jnp.sum(jnp.where(sel2, counts, 0.0), axis=-1, keepdims=True)
    carry_ref[...] = carry_ref[...] + jnp.sum(onehot, axis=0, keepdims=True)

    info = jnp.zeros((tm, ROUTER_LANES), F32)
    for k, val in enumerate((i1, i2, r1, r2, w1, w2)):
        info = jnp.where(lane_i == k, val, info)
    return xn, info


def _router_kernel(yp_ref, ys_ref, g_ref, wr_ref, br_ref, xn_ref, info_ref, cnt_ref, carry_ref, *,
                   prompt_tiles):
    i = pl.program_id(0)

    @pl.when(i == 0)
    def _():
        carry_ref[...] = jnp.zeros_like(carry_ref)

    @pl.when(i < prompt_tiles)
    def _():
        xn, info = _route_rows(yp_ref[...], g_ref, wr_ref, br_ref, carry_ref)
        xn_ref[...] = xn
        info_ref[...] = info

    @pl.when(i == prompt_tiles)
    def _():
        nb = ys_ref.shape[0]
        xn, info = _route_rows(ys_ref[...], g_ref, wr_ref, br_ref, carry_ref)
        xn_ref[0:nb, :] = xn
        info_ref[0:nb, :] = info

    cnt_ref[...] = carry_ref[...]


def _router(yp, ys, g, w_router, b_router):
    n_p, d = yp.shape
    nb = ys.shape[0]
    tm = min(TM_NORM, n_p)
    prompt_tiles = n_p // tm
    n = n_p + nb
    return pl.pallas_call(
        functools.partial(_router_kernel, prompt_tiles=prompt_tiles),
        grid=(prompt_tiles + 1,),
        in_specs=[pl.BlockSpec((tm, d), lambda i: (jnp.minimum(i, prompt_tiles - 1), 0)),
                  pl.BlockSpec((nb, d), lambda i: (0, 0)),
                  pl.BlockSpec((1, d), lambda i: (0, 0)),
                  pl.BlockSpec((d, ROUTER_LANES), lambda i: (0, 0)),
                  pl.BlockSpec((1, ROUTER_LANES), lambda i: (0, 0))],
        out_specs=[pl.BlockSpec((tm, d), lambda i: (i, 0)),
                   pl.BlockSpec((tm, ROUTER_LANES), lambda i: (i, 0)),
                   pl.BlockSpec((SUBLANES, ROUTER_LANES), lambda i: (0, 0))],
        out_shape=[jax.ShapeDtypeStruct((n, d), F32),
                   jax.ShapeDtypeStruct((n, ROUTER_LANES), F32),
                   jax.ShapeDtypeStruct((SUBLANES, ROUTER_LANES), F32)],
        scratch_shapes=[pltpu.VMEM((SUBLANES, ROUTER_LANES), F32)],
        compiler_params=_cparams(("arbitrary",)),
        name="router",
    )(yp, ys, g.reshape(1, d), w_router, b_router)


def _row_copy(src, i, dst, j, sem):
    return pltpu.make_async_copy(src.at[pl.ds(i, 1)], dst.at[pl.ds(j, 1)], sem)


def _moe_kernel(te_ref, na_ref, src_ref, x_hbm, wg_ref, wu_ref, wd_ref, ys_ref, buf, sem):
    del te_ref
    g = pl.program_id(0)
    n_act = na_ref[0]

    def start_rows(tile, slot):
        for r in range(TM_MOE):
            _row_copy(x_hbm, src_ref[tile * TM_MOE + r], buf.at[slot], r, sem.at[slot]).start()

    def wait_rows(slot):
        for r in range(TM_MOE):
            _row_copy(x_hbm, 0, buf.at[slot], r, sem.at[slot]).wait()

    @pl.when(g == 0)
    def _():
        def body(r, c):
            _row_copy(x_hbm, src_ref[r], buf.at[0], r, sem.at[0]).start()
            return c
        lax.fori_loop(0, TM_MOE, body, 0, unroll=8)

    @pl.when(g < n_act)
    def _():
        slot = g % 2
        wait_rows(slot)
        x = buf[slot].astype(BF16)
        start_rows(jnp.minimum(g + 1, n_act - 1), 1 - slot)
        a = _dot(x, wg_ref[...].astype(BF16))
        u = _dot(x, wu_ref[...].astype(BF16))
        h = (a * jax.nn.sigmoid(a)) * u
        ys_ref[...] = _dot(h.astype(BF16), wd_ref[...].astype(BF16))

    @pl.when(g == n_act)
    def _():
        wait_rows(g % 2)

    @pl.when(g >= n_act)
    def _():
        ys_ref[...] = jnp.zeros_like(ys_ref)


def _moe(tile_expert, n_active, src, xn, w_gate, w_up, w_down, layer):
    d = xn.shape[1]
    n_tiles = src.shape[0] // TM_MOE + 1
    wsel = lambda g, te, na, src: (layer, te[g], 0, 0)
    return pl.pallas_call(
        _moe_kernel,
        grid_spec=pltpu.PrefetchScalarGridSpec(
            num_scalar_prefetch=3, grid=(n_tiles,),
            in_specs=[pl.BlockSpec(memory_space=pl.ANY),
                      pl.BlockSpec((None, None, d, EXPERT_FF), wsel),
                      pl.BlockSpec((None, None, d, EXPERT_FF), wsel),
                      pl.BlockSpec((None, None, EXPERT_FF, d), wsel)],
            out_specs=pl.BlockSpec((TM_MOE, d), lambda g, te, na, src: (g, 0)),
            scratch_shapes=[pltpu.VMEM((2, TM_MOE, d), F32), pltpu.SemaphoreType.DMA((2,))]),
        out_shape=jax.ShapeDtypeStruct((n_tiles * TM_MOE, d), F32),
        compiler_params=_cparams(("arbitrary",)),
        name="moe_ffn",
    )(tile_expert, n_active, src, xn, w_gate, w_up, w_down)


def _moe_sample_kernel(ue_ref, nu_ref, x_ref, info_ref, y_ref, g_ref, wg_ref, wu_ref, wd_ref,
                       xo_ref, xno_ref, acc_ref):
    s = pl.program_id(0)

    @pl.when(s == 0)
    def _():
        acc_ref[...] = y_ref[...]

    @pl.when(s < nu_ref[0])
    def _():
        x = x_ref[...]
        a = _dot3(x, wg_ref[...])
        u = _dot3(x, wu_ref[...])
        out = _dot3((a * jax.nn.sigmoid(a)) * u, wd_ref[...])
        e = ue_ref[s].astype(F32)
        w = (jnp.where(info_ref[:, 0:1] == e, info_ref[:, 4:5], 0.0)
             + jnp.where(info_ref[:, 1:2] == e, info_ref[:, 5:6], 0.0))
        acc_ref[...] += w * out

    @pl.when(s == pl.num_programs(0) - 1)
    def _():
        xo_ref[...] = acc_ref[...]
        xno_ref[...] = _rms(acc_ref[...], g_ref[...])


def _moe_sample(xn2, info, y, g, w_gate, w_up, w_down, layer, row0):
    nb, d = y.shape
    blk0 = row0 // nb
    e_s = info[row0:, 0:2].astype(jnp.int32).reshape(-1)
    used = jnp.zeros((N_EXPERTS,), jnp.int32).at[e_s].set(1)
    n_used = jnp.sum(used)
    order = jnp.argsort(1 - used, stable=True).astype(jnp.int32)
    used_list = jnp.where(jnp.arange(N_EXPERTS) < n_used, order, order[n_used - 1])
    rows = lambda s, ue, nu: (blk0, 0)
    fixed = lambda s, ue, nu: (0, 0)
    wsel = lambda s, ue, nu: (layer, ue[s], 0, 0)
    return pl.pallas_call(
        _moe_sample_kernel,
        grid_spec=pltpu.PrefetchScalarGridSpec(
            num_scalar_prefetch=2, grid=(N_EXPERTS,),
            in_specs=[pl.BlockSpec((nb, d), rows),
                      pl.BlockSpec((nb, ROUTER_LANES), rows),
                      pl.BlockSpec((nb, d), fixed),
                      pl.BlockSpec((1, d), fixed),
                      pl.BlockSpec((None, None, d, EXPERT_FF), wsel),
                      pl.BlockSpec((None, None, d, EXPERT_FF), wsel),
                      pl.BlockSpec((None, None, EXPERT_FF, d), wsel)],
            out_specs=[pl.BlockSpec((nb, d), fixed), pl.BlockSpec((nb, d), fixed)],
            scratch_shapes=[pltpu.VMEM((nb, d), F32)]),
        out_shape=[jax.ShapeDtypeStruct((nb, d), F32), jax.ShapeDtypeStruct((nb, d), F32)],
        compiler_params=_cparams(("arbitrary",)),
        name="moe_sample",
    )(used_list, n_used.reshape(1).astype(jnp.int32), xn2, info, y, g.reshape(1, d),
      w_gate, w_up, w_down)


def _combine_kernel(pos_ref, y_ref, info_ref, g_ref, ys_hbm, *rest, rows, write_x, blk0):
    if write_x:
        xo_ref, xno_ref, buf0, buf1, sem = rest
    else:
        xno_ref, buf0, buf1, sem = rest
    base = (blk0 + pl.program_id(0)) * rows

    def issue(r, c):
        t = base + r
        _row_copy(ys_hbm, pos_ref[2 * t], buf0, r, sem).start()
        _row_copy(ys_hbm, pos_ref[2 * t + 1], buf1, r, sem).start()
        return c

    def drain(r, c):
        t = base + r
        _row_copy(ys_hbm, pos_ref[2 * t], buf0, r, sem).wait()
        _row_copy(ys_hbm, pos_ref[2 * t + 1], buf1, r, sem).wait()
        return c

    lax.fori_loop(0, rows, issue, 0)
    lax.fori_loop(0, rows, drain, 0)
    xnew = y_ref[...] + info_ref[:, 4:5] * buf0[...] + info_ref[:, 5:6] * buf1[...]
    if write_x:
        xo_ref[...] = xnew
    xno_ref[...] = _rms(xnew, g_ref[...]).astype(xno_ref.dtype)


def _combine(pos, y, info, g, ys, norm_dtype, write_x, row0):
    n, d = y.shape
    rows = min(TM_COMBINE, n)
    assert row0 % rows == 0
    blk0 = row0 // rows
    tile = pl.BlockSpec((rows, d), lambda i, pos: (i, 0))
    out_specs = [tile]
    out_shape = [jax.ShapeDtypeStruct((n, d), norm_dtype)]
    if write_x:
        out_specs = [tile] + out_specs
        out_shape = [jax.ShapeDtypeStruct((n, d), F32)] + out_shape
    return pl.pallas_call(
        functools.partial(_combine_kernel, rows=rows, write_x=write_x, blk0=blk0),
        grid_spec=pltpu.PrefetchScalarGridSpec(
            num_scalar_prefetch=1, grid=(n // rows,),
            in_specs=[tile,
                      pl.BlockSpec((rows, ROUTER_LANES), lambda i, pos: (blk0 + i, 0)),
                      pl.BlockSpec((1, d), lambda i, pos: (0, 0)),
                      pl.BlockSpec(memory_space=pl.ANY)],
            out_specs=out_specs,
            scratch_shapes=[pltpu.VMEM((rows, d), F32), pltpu.VMEM((rows, d), F32),
                            pltpu.SemaphoreType.DMA(())]),
        out_shape=out_shape,
        compiler_params=_cparams(("arbitrary",)),
        name="combine",
    )(pos, y, info, g.reshape(1, d), ys)


def _plan(info, cnt, n_tiles):
    count = cnt[0, :N_EXPERTS].astype(jnp.int32)
    tiles = (count + TM_MOE - 1) // TM_MOE
    tile_end = jnp.cumsum(tiles)
    row_off = (tile_end - tiles) * TM_MOE
    n_active = tile_end[-1]
    pos = (row_off[info[:, 0:2].astype(jnp.int32)] + info[:, 2:4].astype(jnp.int32)).reshape(-1)
    token = jnp.arange(pos.shape[0], dtype=jnp.int32) // 2
    src = jnp.zeros((n_tiles * TM_MOE,), jnp.int32).at[pos].set(token, unique_indices=True)
    g = jnp.minimum(jnp.arange(n_tiles + 1, dtype=jnp.int32), n_active - 1)
    tile_expert = jnp.sum((g[:, None] >= tile_end[None, :]).astype(jnp.int32), axis=1)
    return pos, src, tile_expert.astype(jnp.int32), n_active.reshape(1).astype(jnp.int32)


def kernel(x_prompt, x_sample, state_conv, cache_win_k, cache_win_v, state_pool, norm1_g, w_in,
           conv_w, attn_sinks, sgu_ln_g, sgu_ln_b, sgu_w, sgu_b, pool_w, pool_scale, w_branch, w_o,
           norm2_g, router_group_w, router_group_b, router_expert_w, router_expert_b, moe_w_gate,
           moe_w_up, moe_w_down, final_norm_g):
    batch, seq, d = x_prompt.shape
    assert seq % TB == 0 and seq >= 2 * TB and d == D_MODEL
    nb = x_sample.shape[0]
    depth = w_in.shape[0]
    n_p = batch * seq
    kv_w = N_KV * HEAD_DIM
    n_slots = 2 * (n_p + nb)
    n_tiles = (n_slots + N_EXPERTS * (TM_MOE - 1) + TM_MOE - 1) // TM_MOE

    xp = x_prompt.reshape(n_p, d)
    xs_ = x_sample.reshape(nb, d)
    xn_p = _rmsnorm(xp, norm1_g[0], BF16)
    xn_s = _rmsnorm(xs_, norm1_g[0], F32)

    pad = ROUTER_LANES - N_EXPERTS - N_EXP_GROUPS
    conv_p, k_p, v_p, pool_p = [], [], [], []
    conv_s, k_s, v_s, pool_s, chunk_s = [], [], [], [], []
    y_prompt = y_sample = None
    for l in range(depth):
        lw = dict(attn_sinks=attn_sinks[l], conv_w=conv_w[l], sgu_ln_g=sgu_ln_g[l],
                  sgu_ln_b=sgu_ln_b[l], sgu_w=sgu_w[l], sgu_b=sgu_b[l], pool_w=pool_w[l],
                  pool_scale=pool_scale[l])
        w_router = jnp.concatenate(
            [router_expert_w[l], router_group_w[l], jnp.zeros((d, pad), F32)], axis=1)
        b_router = jnp.concatenate(
            [router_expert_b[l], router_group_b[l], jnp.zeros((pad,), F32)]).reshape(1, -1)
        last = l == depth - 1
        g_next = final_norm_g if last else norm1_g[l + 1]

        zs_p = _in_proj(xn_p, w_in, l, False)
        outs_p, c_st, k_st, v_st, p_st = _branch_prompt(zs_p, batch, seq, lw)
        conv_p.append(c_st)
        k_p.append(k_st.reshape(batch, WINDOW, N_KV, HEAD_DIM))
        v_p.append(v_st.reshape(batch, WINDOW, N_KV, HEAD_DIM))
        pool_p.append(p_st)
        merged_p = _merge(xn_p, outs_p, w_in, w_branch, l, False)
        yp = _out_proj(merged_p, w_o, xp, l, False)
        if not last:
            x_t = xp.reshape(batch, seq, d)[:, seq - 2 * TB:].reshape(batch * 2 * TB, d)
            xn_t = _rmsnorm(x_t, norm1_g[l], F32)
            zs_t = _in_proj(xn_t, w_in, l, True)
            outs_t = _branch_prompt(zs_t, batch, seq, lw, hi=True, t_start=seq // TB - 2)[0]
            second = lambda a: a.reshape(batch, 2, TB, a.shape[-1])[:, 1].reshape(batch * TB, -1)
            outs_t = jnp.stack([second(outs_t[b]) for b in range(N_BRANCH)])
            merged_t = _merge(second(xn_t), outs_t, w_in, w_branch, l, True)
            y_t = _out_proj(merged_t, w_o, second(x_t), l, True)
            yp = _put_last_blocks(yp, y_t, batch, seq)

        zs_s = _in_proj(xn_s, w_in, l, True)
        outs_s, c_st, k_st, v_st, p_st, cv = _branch_sample(
            zs_s, lw, jnp.swapaxes(state_conv[l], 0, 1),
            cache_win_k[l].reshape(nb, WINDOW, kv_w), cache_win_v[l].reshape(nb, WINDOW, kv_w),
            jnp.swapaxes(state_pool[l], 0, 1))
        conv_s.append(jnp.swapaxes(c_st, 0, 1))
        k_s.append(k_st.reshape(nb, WINDOW, N_KV, HEAD_DIM))
        v_s.append(v_st.reshape(nb, WINDOW, N_KV, HEAD_DIM))
        pool_s.append(jnp.swapaxes(p_st, 0, 1))
        chunk_s.append(cv.reshape(nb, 1, BRANCH_W))
        merged_s = _merge(xn_s, outs_s, w_in, w_branch, l, True)
        ys_ = _out_proj(merged_s, w_o, xs_, l, True)

        xn2, info, cnt = _router(yp, ys_, norm2_g[l], w_router, b_router)
        pos, src, tile_expert, n_active = _plan(info, cnt, n_tiles)
        ffn = _moe(tile_expert, n_active, src, xn2, moe_w_gate, moe_w_up, moe_w_down, l)
        if last:
            (y_prompt,) = _combine(pos, yp, info, g_next, ffn, F32, False, 0)
            (y_sample,) = _combine(pos, ys_, info, g_next, ffn, F32, False, n_p)
        else:
            xp, xn_p = _combine(pos, yp, info, g_next, ffn, BF16, True, 0)
            xs_, xn_s = _moe_sample(xn2, info, ys_, g_next, moe_w_gate, moe_w_up, moe_w_down, l, n_p)

    return (y_prompt.reshape(batch, seq, d), y_sample.reshape(nb, 1, d),
            jnp.stack(conv_p), jnp.stack(k_p), jnp.stack(v_p), jnp.stack(pool_p),
            jnp.stack(conv_s), jnp.stack(k_s), jnp.stack(v_s), jnp.stack(pool_s),
            jnp.stack(chunk_s))
```

```python
import functools

import jax
import jax.numpy as jnp
from jax import lax
from jax.experimental import pallas as pl
from jax.experimental.pallas import tpu as pltpu

F32 = jnp.float32
BF16 = jnp.bfloat16

D_MODEL = 2048
BRANCH_W = 512
N_BRANCH = 4
CONV_W = 3
HEAD_DIM = 64
N_HEADS = 8
N_KV = 2
GQA_G = 4
WINDOW = 128
ROPE_THETA = 500000.0
ROPE_DIM = 16
CHUNK = 128
SGU_GROUPS = 4
POOL_WINDOWS = (2, 4, 8, 16)
POOL_MAX = 16
N_EXP_GROUPS = 4
EXP_PER_GROUP = 8
N_EXPERTS = 32
EXPERT_FF = 512
EPS = 1e-6
PAST_LEN = 16384
NEG = -3.0e38

A_H_OFF = 0
A_C_OFF = 512
A_B_OFF = 1024
Q_OFF = 1536
K_OFF = 2048
V_OFF = 2176
C_OFF = 2304
D_OFF = 3328
G_OFF = 3840

LANES = 128
SUBLANES = 8
VMEM_LIMIT = 52 * 1024 * 1024

TM_NORM = 512
TM_PROJ = 1024
TN_PROJ = 768
TM_MERGE = 2048
TN_MERGE = 256
TM_WO = 2048
TN_WO = 512
TM_MOE = 288
TM_COMBINE = 256
TB = 128
ROUTER_LANES = 128


def _cparams(sem):
    return pltpu.CompilerParams(dimension_semantics=sem, vmem_limit_bytes=VMEM_LIMIT)


_NN = (((1,), (0,)), ((), ()))
_NT = (((1,), (1,)), ((), ()))


def _dot(a, b, dims=_NN):
    return lax.dot_general(a, b, dims, preferred_element_type=F32)


def _split(x):
    hi = x.astype(BF16)
    lo = (x - hi.astype(F32)).astype(BF16)
    return hi, lo


def _dot3(a, b, dims=_NN):
    ah, al = _split(a)
    bh, bl = _split(b)
    return _dot(ah, bh, dims) + _dot(al, bh, dims) + _dot(ah, bl, dims)


def _mm(a, w, hi, dims=_NN):
    if hi:
        return _dot3(a.astype(F32), w.astype(F32), dims)
    return _dot(a.astype(BF16), w.astype(BF16), dims)


def _rms(x, g):
    return (x * lax.rsqrt(jnp.mean(x * x, axis=-1, keepdims=True) + EPS)) * g


def _rmsnorm_kernel(x_ref, g_ref, o_ref):
    o_ref[...] = _rms(x_ref[...], g_ref[...]).astype(o_ref.dtype)


def _rmsnorm(x, g, out_dtype):
    n, d = x.shape
    tm = min(TM_NORM, n)
    return pl.pallas_call(
        _rmsnorm_kernel,
        grid=(n // tm,),
        in_specs=[pl.BlockSpec((tm, d), lambda i: (i, 0)),
                  pl.BlockSpec((1, d), lambda i: (0, 0))],
        out_specs=pl.BlockSpec((tm, d), lambda i: (i, 0)),
        out_shape=jax.ShapeDtypeStruct((n, d), out_dtype),
        compiler_params=_cparams(("parallel",)),
        name="rmsnorm",
    )(x, g.reshape(1, d))


def _proj_kernel(x_ref, w_ref, o_ref, *, hi):
    o_ref[...] = _mm(x_ref[...], w_ref[...], hi)


def _in_proj(xn, w_in, layer, hi):
    n, d = xn.shape
    tm = min(TM_PROJ // 2 if hi else TM_PROJ, n)
    return pl.pallas_call(
        functools.partial(_proj_kernel, hi=hi),
        grid=(n // tm, G_OFF // TN_PROJ),
        in_specs=[pl.BlockSpec((tm, d), lambda i, j: (i, 0)),
                  pl.BlockSpec((None, d, TN_PROJ), lambda i, j: (layer, 0, j))],
        out_specs=pl.BlockSpec((tm, TN_PROJ), lambda i, j: (i, j)),
        out_shape=jax.ShapeDtypeStruct((n, G_OFF), F32),
        compiler_params=_cparams(("parallel", "arbitrary")),
        name="in_proj",
    )(xn, w_in)


def _rope_tables(pos):
    half = ROPE_DIM // 2
    inv = jnp.power(jnp.float32(ROPE_THETA), -jnp.arange(half, dtype=F32) * (2.0 / ROPE_DIM))
    ang = pos.astype(F32)[:, None] * inv[None, :]
    cos, sin = jnp.cos(ang), jnp.sin(ang)
    t = pos.shape[0]
    rest = HEAD_DIM - ROPE_DIM
    c = jnp.concatenate([cos, cos, jnp.ones((t, rest), F32)], axis=1)
    s1 = jnp.concatenate([-sin, jnp.zeros((t, HEAD_DIM - half), F32)], axis=1)
    s2 = jnp.concatenate([jnp.zeros((t, half), F32), sin, jnp.zeros((t, rest), F32)], axis=1)
    rep = LANES // HEAD_DIM
    return jnp.tile(c, (1, rep)), jnp.tile(s1, (1, rep)), jnp.tile(s2, (1, rep))


def _rope(x, c, s1, s2):
    half = ROPE_DIM // 2
    return x * c + pltpu.roll(x, LANES - half, 1) * s1 + pltpu.roll(x, half, 1) * s2


def _gelu(x):
    return 0.5 * x * (1.0 + lax.erf(x * 0.7071067811865476))


def _layernorm(v, g, b):
    mu = jnp.mean(v, axis=-1, keepdims=True)
    vc = v - mu
    return vc * lax.rsqrt(jnp.mean(vc * vc, axis=-1, keepdims=True) + EPS) * g + b


def _branch_prompt_kernel(sink_ref, z_ref, rc_ref, rs1_ref, rs2_ref, cw_ref, lng_ref, lnb_ref,
                          sw_ref, sbt_ref, pw_ref, ps_ref,
                          outs_ref, conv_ref, kst_ref, vst_ref, pool_ref,
                          kprev, vprev, ci_ext, p_ext, *, hi, t_start):
    tb = pl.program_id(1)
    halo_c = SUBLANES
    halo_p = POOL_MAX

    @pl.when(tb == 0)
    def _():
        kprev[...] = jnp.zeros_like(kprev)
        vprev[...] = jnp.zeros_like(vprev)
        ci_ext[0:halo_c, :] = jnp.zeros((halo_c, BRANCH_W), F32)
        p_ext[0:halo_p, :] = jnp.zeros((halo_p, BRANCH_W), F32)

    ci = z_ref[:, A_C_OFF:A_B_OFF] * z_ref[:, A_H_OFF:A_C_OFF]
    ci_ext[halo_c:halo_c + TB, :] = ci
    y = cw_ref[2:3, :] * ci
    for j in range(CONV_W - 1):
        y = y + cw_ref[j:j + 1, :] * ci_ext[pl.ds(halo_c - (CONV_W - 1) + j, TB), :]
    outs_ref[0] = (z_ref[:, A_B_OFF:Q_OFF] * y).astype(outs_ref.dtype)
    conv_ref[...] = ci_ext[pl.ds(halo_c + TB - (CONV_W - 1), CONV_W - 1), :]
    ci_ext[0:halo_c, :] = ci[TB - halo_c:, :]

    p = z_ref[:, D_OFF:G_OFF]
    p_ext[halo_p:halo_p + TB, :] = p
    posf = ((t_start + tb) * TB + lax.broadcasted_iota(jnp.int32, (TB, 1), 0)).astype(F32)
    for g, w in enumerate(POOL_WINDOWS):
        sl = slice(g * LANES, (g + 1) * LANES)
        s = p[:, sl]
        for k in range(1, w):
            s = s + p_ext[pl.ds(halo_p - k, TB), sl]
        dlt = s / jnp.minimum(posf + 1.0, float(w)) - p[:, sl]
        yg = _mm(dlt, pw_ref[g], hi) * ps_ref[:, sl]
        outs_ref[3, :, sl] = yg.astype(outs_ref.dtype)
    pool_ref[...] = p_ext[pl.ds(halo_p + TB - (POOL_MAX - 1), POOL_MAX - 1), :]
    p_ext[0:halo_p, :] = p[TB - halo_p:, :]

    ge = _gelu(z_ref[:, C_OFF:D_OFF])
    u = ge[:, :BRANCH_W]
    vn = _layernorm(ge[:, BRANCH_W:], lng_ref[...], lnb_ref[...])
    r_i = lax.broadcasted_iota(jnp.int32, (TB, TB), 0)
    c_i = lax.broadcasted_iota(jnp.int32, (TB, TB), 1)
    for g in range(SGU_GROUPS):
        sl = slice(g * LANES, (g + 1) * LANES)
        wt = jnp.where(c_i <= r_i, sw_ref[g], 0.0)
        f = _mm(wt, vn[:, sl], hi) + sbt_ref[:, g:g + 1]
        outs_ref[2, :, sl] = (u[:, sl] * f).astype(outs_ref.dtype)

    rc, rs1, rs2 = rc_ref[...], rs1_ref[...], rs2_ref[...]
    kr = _rope(z_ref[:, K_OFF:V_OFF], rc, rs1, rs2)
    v = z_ref[:, V_OFF:C_OFF]
    qr = [_rope(z_ref[:, Q_OFF + c * LANES:Q_OFF + (c + 1) * LANES], rc, rs1, rs2)
          for c in range(N_HEADS * HEAD_DIM // LANES)]
    qi = lax.broadcasted_iota(jnp.int32, (TB, 2 * TB), 0)
    kj = lax.broadcasted_iota(jnp.int32, (TB, 2 * TB), 1)
    allowed = (kj > qi) & (kj <= qi + WINDOW) & ((kj >= TB) | (tb > 0))
    kp, vp = kprev[...], vprev[...]
    heads = [None] * N_HEADS
    for hk in range(N_KV):
        hs = slice(hk * HEAD_DIM, (hk + 1) * HEAD_DIM)
        kk = jnp.concatenate([kp[:, hs], kr[:, hs]], axis=0)
        vv = jnp.concatenate([vp[:, hs], v[:, hs]], axis=0)
        if not hi:
            kk, vv = kk.astype(BF16), vv.astype(BF16)
        for g in range(GQA_G):
            hq = hk * GQA_G + g
            per = LANES // HEAD_DIM
            qh = qr[hq // per][:, (hq % per) * HEAD_DIM:(hq % per + 1) * HEAD_DIM]
            s = _mm(qh, kk, hi, _NT) * (HEAD_DIM ** -0.5)
            s = jnp.where(allowed, s, -1e30)
            sink = sink_ref[hq]
            m = jnp.maximum(jnp.max(s, axis=-1, keepdims=True), sink)
            e = jnp.exp(s - m)
            den = jnp.sum(e, axis=-1, keepdims=True) + jnp.exp(sink - m)
            heads[hq] = _mm(e / den, vv, hi)
    outs_ref[1] = jnp.concatenate(heads, axis=1).astype(outs_ref.dtype)
    kst_ref[...] = kr
    vst_ref[...] = v
    kprev[...] = kr
    vprev[...] = v


def _branch_prompt(zs, batch, seq, lw, hi=False, t_start=0):
    n = zs.shape[0]
    nt = n // (batch * TB)
    rc, rs1, rs2 = _rope_tables(jnp.arange(seq))
    full = lambda shape: pl.BlockSpec(shape, lambda b, t: (0,) * len(shape))
    tab = pl.BlockSpec((TB, LANES), lambda b, t: (t_start + t, 0))
    kv_w = N_KV * HEAD_DIM
    return pl.pallas_call(
        functools.partial(_branch_prompt_kernel, hi=hi, t_start=t_start),
        grid=(batch, nt),
        in_specs=[pl.BlockSpec(memory_space=pltpu.SMEM),
                  pl.BlockSpec((TB, G_OFF), lambda b, t: (b * nt + t, 0)),
                  tab, tab, tab,
                  full((CONV_W, BRANCH_W)), full((1, BRANCH_W)), full((1, BRANCH_W)),
                  full((SGU_GROUPS, CHUNK, CHUNK)), full((CHUNK, SGU_GROUPS)),
                  full((len(POOL_WINDOWS), LANES, LANES)), full((1, BRANCH_W))],
        out_specs=[pl.BlockSpec((N_BRANCH, TB, BRANCH_W), lambda b, t: (0, b * nt + t, 0)),
                   pl.BlockSpec((None, CONV_W - 1, BRANCH_W), lambda b, t: (b, 0, 0)),
                   pl.BlockSpec((None, WINDOW, kv_w), lambda b, t: (b, 0, 0)),
                   pl.BlockSpec((None, WINDOW, kv_w), lambda b, t: (b, 0, 0)),
                   pl.BlockSpec((None, POOL_MAX - 1, BRANCH_W), lambda b, t: (b, 0, 0))],
        out_shape=[jax.ShapeDtypeStruct((N_BRANCH, n, BRANCH_W), F32 if hi else BF16),
                   jax.ShapeDtypeStruct((batch, CONV_W - 1, BRANCH_W), F32),
                   jax.ShapeDtypeStruct((batch, WINDOW, kv_w), F32),
                   jax.ShapeDtypeStruct((batch, WINDOW, kv_w), F32),
                   jax.ShapeDtypeStruct((batch, POOL_MAX - 1, BRANCH_W), F32)],
        scratch_shapes=[pltpu.VMEM((TB, kv_w), F32), pltpu.VMEM((TB, kv_w), F32),
                        pltpu.VMEM((SUBLANES + TB, BRANCH_W), F32),
                        pltpu.VMEM((POOL_MAX + TB, BRANCH_W), F32)],
        compiler_params=_cparams(("arbitrary", "arbitrary")),
        name="branch_prompt",
    )(lw["attn_sinks"], zs, rc, rs1, rs2, lw["conv_w"], lw["sgu_ln_g"].reshape(1, -1),
      lw["sgu_ln_b"].reshape(1, -1), lw["sgu_w"], lw["sgu_b"].T, lw["pool_w"],
      lw["pool_scale"].reshape(1, -1))


def _branch_sample_kernel(sink_ref, z_ref, rc_ref, rs1_ref, rs2_ref, cw_ref, lng_ref, lnb_ref,
                          ssc_ref, ssh_ref, pw_ref, ps_ref, cpast_ref, kc_ref, vc_ref, ppast_ref,
                          outs_ref, conv_ref, kst_ref, vst_ref, pool_ref, chunkv_ref,
                          qrot, krot, vnew):
    nb = z_ref.shape[0]

    ci = z_ref[:, A_C_OFF:A_B_OFF] * z_ref[:, A_H_OFF:A_C_OFF]
    y = cw_ref[CONV_W - 1:CONV_W, :] * ci
    for j in range(CONV_W - 1):
        y = y + cw_ref[j:j + 1, :] * cpast_ref[j]
    outs_ref[0] = z_ref[:, A_B_OFF:Q_OFF] * y
    for j in range(CONV_W - 2):
        conv_ref[j] = cpast_ref[j + 1]
    conv_ref[CONV_W - 2] = ci

    p = z_ref[:, D_OFF:G_OFF]
    for g, w in enumerate(POOL_WINDOWS):
        sl = slice(g * LANES, (g + 1) * LANES)
        s = p[:, sl]
        for k in range(1, w):
            s = s + ppast_ref[POOL_MAX - 1 - k, :, sl]
        dlt = s / float(min(PAST_LEN + 1, w)) - p[:, sl]
        outs_ref[3, :, sl] = _dot3(dlt, pw_ref[g]) * ps_ref[:, sl]
    for j in range(POOL_MAX - 2):
        pool_ref[j] = ppast_ref[j + 1]
    pool_ref[POOL_MAX - 2] = p

    ge = _gelu(z_ref[:, C_OFF:D_OFF])
    vn = _layernorm(ge[:, BRANCH_W:], lng_ref[...], lnb_ref[...])
    outs_ref[2] = ge[:, :BRANCH_W] * (vn * ssc_ref[...] + ssh_ref[...])
    chunkv_ref[...] = vn

    rc, rs1, rs2 = rc_ref[...], rs1_ref[...], rs2_ref[...]
    krot[...] = _rope(z_ref[:, K_OFF:V_OFF], rc, rs1, rs2)
    vnew[...] = z_ref[:, V_OFF:C_OFF]
    for c in range(N_HEADS * HEAD_DIM // LANES):
        cs = slice(c * LANES, (c + 1) * LANES)
        qrot[:, cs] = _rope(z_ref[:, Q_OFF + c * LANES:Q_OFF + (c + 1) * LANES], rc, rs1, rs2)
    row = lax.broadcasted_iota(jnp.int32, (WINDOW, LANES), 0)
    lo = lax.broadcasted_iota(jnp.int32, (WINDOW, LANES), 1) < HEAD_DIM
    lo1 = lax.broadcasted_iota(jnp.int32, (1, LANES), 1) < HEAD_DIM
    scale = HEAD_DIM ** -0.5

    def body(b, carry):
        kn = jnp.where(row == WINDOW - 1, krot[pl.ds(b, 1), :], pltpu.roll(kc_ref[b], WINDOW - 1, 0))
        vn_b = jnp.where(row == WINDOW - 1, vnew[pl.ds(b, 1), :], pltpu.roll(vc_ref[b], WINDOW - 1, 0))
        kst_ref[b] = kn
        vst_ref[b] = vn_b
        qb = qrot[pl.ds(b, 1), :]
        o = []
        for g in range(GQA_G):
            ca = qb[:, (g // 2) * LANES:(g // 2 + 1) * LANES]
            cb = qb[:, (2 + g // 2) * LANES:(3 + g // 2) * LANES]
            if g % 2 == 0:
                qrow = jnp.where(lo1, ca, pltpu.roll(cb, HEAD_DIM, 1))
            else:
                qrow = jnp.where(lo1, pltpu.roll(ca, HEAD_DIM, 1), cb)
            prod = kn * qrow
            pn = []
            for part, hq in ((jnp.where(lo, prod, 0.0), g), (jnp.where(lo, 0.0, prod), GQA_G + g)):
                s = jnp.sum(part, axis=1, keepdims=True) * scale
                sink = sink_ref[hq]
                m = jnp.maximum(jnp.max(s, axis=0, keepdims=True), sink)
                e = jnp.exp(s - m)
                den = jnp.sum(e, axis=0, keepdims=True) + jnp.exp(sink - m)
                pn.append(e / den)
            o.append(jnp.sum(jnp.where(lo, pn[0], pn[1]) * vn_b, axis=0, keepdims=True))
        sw = lambda x: pltpu.roll(x, HEAD_DIM, 1)
        chunks = [jnp.where(lo1, o[0], sw(o[1])), jnp.where(lo1, o[2], sw(o[3])),
                  jnp.where(lo1, sw(o[0]), o[1]), jnp.where(lo1, sw(o[2]), o[3])]
        outs_ref[1, pl.ds(b, 1), :] = jnp.concatenate(chunks, axis=1)
        return carry

    lax.fori_loop(0, nb, body, 0)


def _branch_sample(zs, lw, cpast, kc, vc, ppast):
    nb = zs.shape[0]
    kv_w = N_KV * HEAD_DIM
    rc, rs1, rs2 = _rope_tables(jnp.full((1,), PAST_LEN, jnp.int32))
    ssc = jnp.repeat(lw["sgu_w"][:, 0, 0], CHUNK).reshape(1, BRANCH_W)
    ssh = jnp.repeat(lw["sgu_b"][:, 0], CHUNK).reshape(1, BRANCH_W)
    vspec = pl.BlockSpec(memory_space=pltpu.VMEM)
    return pl.pallas_call(
        _branch_sample_kernel,
        in_specs=[pl.BlockSpec(memory_space=pltpu.SMEM)] + [vspec] * 15,
        out_specs=[vspec] * 6,
        out_shape=[jax.ShapeDtypeStruct((N_BRANCH, nb, BRANCH_W), F32),
                   jax.ShapeDtypeStruct((CONV_W - 1, nb, BRANCH_W), F32),
                   jax.ShapeDtypeStruct((nb, WINDOW, kv_w), F32),
                   jax.ShapeDtypeStruct((nb, WINDOW, kv_w), F32),
                   jax.ShapeDtypeStruct((POOL_MAX - 1, nb, BRANCH_W), F32),
                   jax.ShapeDtypeStruct((nb, BRANCH_W), F32)],
        scratch_shapes=[pltpu.VMEM((nb, N_HEADS * HEAD_DIM), F32), pltpu.VMEM((nb, kv_w), F32),
                        pltpu.VMEM((nb, kv_w), F32)],
        compiler_params=pltpu.CompilerParams(vmem_limit_bytes=VMEM_LIMIT),
        name="branch_sample",
    )(lw["attn_sinks"], zs, rc, rs1, rs2, lw["conv_w"], lw["sgu_ln_g"].reshape(1, -1),
      lw["sgu_ln_b"].reshape(1, -1), ssc, ssh, lw["pool_w"], lw["pool_scale"].reshape(1, -1),
      cpast, kc, vc, ppast)


def _merge_kernel(xn_ref, outs_ref, wg_ref, wb_ref, o_ref, acc_ref, *, hi):
    nbr = pl.program_id(2)
    gate = jax.nn.sigmoid(_mm(xn_ref[...], wg_ref[...], hi))
    contrib = gate * _mm(outs_ref[...], wb_ref[...], hi)

    @pl.when(nbr == 0)
    def _():
        acc_ref[...] = contrib

    @pl.when(nbr > 0)
    def _():
        acc_ref[...] += contrib

    @pl.when(nbr == N_BRANCH - 1)
    def _():
        o_ref[...] = acc_ref[...].astype(o_ref.dtype)


def _merge(xn, outs, w_in, w_branch, layer, hi):
    n, d = xn.shape
    tm = min(TM_MERGE, n)
    g_blk = G_OFF // TN_MERGE
    per_branch = D_MODEL // TN_MERGE
    return pl.pallas_call(
        functools.partial(_merge_kernel, hi=hi),
        grid=(n // tm, per_branch, N_BRANCH),
        in_specs=[pl.BlockSpec((tm, d), lambda i, j, b: (i, 0)),
                  pl.BlockSpec((None, tm, BRANCH_W), lambda i, j, b: (b, i, 0)),
                  pl.BlockSpec((None, d, TN_MERGE),
                               lambda i, j, b: (layer, 0, g_blk + b * per_branch + j)),
                  pl.BlockSpec((None, None, BRANCH_W, TN_MERGE), lambda i, j, b: (layer, b, 0, j))],
        out_specs=pl.BlockSpec((tm, TN_MERGE), lambda i, j, b: (i, j)),
        out_shape=jax.ShapeDtypeStruct((n, D_MODEL), F32 if hi else BF16),
        scratch_shapes=[pltpu.VMEM((tm, TN_MERGE), F32)],
        compiler_params=_cparams(("parallel", "arbitrary", "arbitrary")),
        name="merge",
    )(xn, outs, w_in, w_branch)


def _wo_kernel(m_ref, w_ref, x_ref, o_ref, *, hi):
    o_ref[...] = x_ref[...] + _mm(m_ref[...], w_ref[...], hi)


def _out_proj(merged, w_o, x, layer, hi):
    n, d = x.shape
    tm = min(TM_WO, n)
    return pl.pallas_call(
        functools.partial(_wo_kernel, hi=hi),
        grid=(n // tm, d // TN_WO),
        in_specs=[pl.BlockSpec((tm, d), lambda i, j: (i, 0)),
                  pl.BlockSpec((None, d, TN_WO), lambda i, j: (layer, 0, j)),
                  pl.BlockSpec((tm, TN_WO), lambda i, j: (i, j))],
        out_specs=pl.BlockSpec((tm, TN_WO), lambda i, j: (i, j)),
        out_shape=jax.ShapeDtypeStruct((n, d), F32),
        compiler_params=_cparams(("parallel", "arbitrary")),
        name="out_proj",
    )(merged, w_o, x)


def _put_rows_kernel(y_hbm, t_ref, o_ref):
    del y_hbm
    o_ref[...] = t_ref[...]


def _put_last_blocks(y, y_tail, batch, seq):
    d = y.shape[1]
    nt = seq // TB
    return pl.pallas_call(
        _put_rows_kernel,
        grid=(batch,),
        in_specs=[pl.BlockSpec(memory_space=pl.ANY),
                  pl.BlockSpec((TB, d), lambda b: (b, 0))],
        out_specs=pl.BlockSpec((TB, d), lambda b: (b * nt + nt - 1, 0)),
        out_shape=jax.ShapeDtypeStruct(y.shape, y.dtype),
        input_output_aliases={0: 0},
        compiler_params=_cparams(("arbitrary",)),
        name="put_last_blocks",
    )(y, y_tail)


def _route_rows(y, g_ref, wr_ref, br_ref, carry_ref):
    tm = y.shape[0]
    xn = _rms(y, g_ref[...])
    logits = _dot3(xn, wr_ref[...]) + br_ref[...]
    lane_i = lax.broadcasted_iota(jnp.int32, (tm, ROUTER_LANES), 1)
    lane = lane_i.astype(F32)
    lane_grp = lax.shift_right_logical(lane_i, 3).astype(F32)
    big = float(ROUTER_LANES)
    is_grp = (lane_i >= N_EXPERTS) & (lane_i < N_EXPERTS + N_EXP_GROUPS)
    lg = jnp.where(is_grp, logits, NEG)
    gmax = jnp.max(lg, axis=-1, keepdims=True)
    gsel = jnp.min(jnp.where(lg == gmax, lane - N_EXPERTS, big), axis=-1, keepdims=True)
    wg = 1.0 / jnp.sum(jnp.where(is_grp, jnp.exp(logits - gmax), 0.0), axis=-1, keepdims=True)
    in_grp = (lane_i < N_EXPERTS) & (lane_grp == gsel)
    v1 = jnp.where(in_grp, logits, NEG)
    m1 = jnp.max(v1, axis=-1, keepdims=True)
    i1 = jnp.min(jnp.where(v1 == m1, lane, big), axis=-1, keepdims=True)
    v2 = jnp.where(lane == i1, NEG, v1)
    m2 = jnp.max(v2, axis=-1, keepdims=True)
    i2 = jnp.min(jnp.where(v2 == m2, lane, big), axis=-1, keepdims=True)
    e2 = jnp.exp(m2 - m1)
    w1 = wg / (1.0 + e2)
    w2 = wg * e2 / (1.0 + e2)

    sel1 = lane == i1
    sel2 = lane == i2
    onehot = jnp.where(sel1 | sel2, 1.0, 0.0)
    r_i = lax.broadcasted_iota(jnp.int32, (tm, tm), 0)
    c_i = lax.broadcasted_iota(jnp.int32, (tm, tm), 1)
    before = jnp.where(c_i < r_i, 1.0, 0.0).astype(BF16)
    counts = _dot(before, onehot.astype(BF16)) + carry_ref[0:1, :]
    r1 = jnp.sum(jnp.where(sel1, counts, 0.0), axis=-1, keepdims=True)
    r2 = jnp.sum(jnp.where(sel2, counts, 0.0), axis=-1, keepdims=True)
    carry_ref[...] = carry_ref[...] + jnp.sum(onehot, axis=0, keepdims=True)

    info = jnp.zeros((tm, ROUTER_LANES), F32)
    for k, val in enumerate((i1, i2, r1, r2, w1, w2)):
        info = jnp.where(lane_i == k, val, info)
    return xn, info


def _router_kernel(yp_ref, ys_ref, g_ref, wr_ref, br_ref, xn_ref, info_ref, cnt_ref, carry_ref, *,
                   prompt_tiles):
    i = pl.program_id(0)

    @pl.when(i == 0)
    def _():
        carry_ref[...] = jnp.zeros_like(carry_ref)

    @pl.when(i < prompt_tiles)
    def _():
        xn, info = _route_rows(yp_ref[...], g_ref, wr_ref, br_ref, carry_ref)
        _store_row_tiles(xn_ref, xn)
        info_ref[...] = info

    @pl.when(i == prompt_tiles)
    def _():
        nb = ys_ref.shape[0]
        xn, info = _route_rows(ys_ref[...], g_ref, wr_ref, br_ref, carry_ref)
        _store_row_tiles(xn_ref, xn)
        info_ref[0:nb, :] = info

    cnt_ref[...] = carry_ref[...]


def _router(yp, ys, g, w_router, b_router):
    n_p, d = yp.shape
    nb = ys.shape[0]
    tm = min(TM_NORM, n_p)
    prompt_tiles = n_p // tm
    n = n_p + nb
    return pl.pallas_call(
        functools.partial(_router_kernel, prompt_tiles=prompt_tiles),
        grid=(prompt_tiles + 1,),
        in_specs=[pl.BlockSpec((tm, d), lambda i: (jnp.minimum(i, prompt_tiles - 1), 0)),
                  pl.BlockSpec((nb, d), lambda i: (0, 0)),
                  pl.BlockSpec((1, d), lambda i: (0, 0)),
                  pl.BlockSpec((d, ROUTER_LANES), lambda i: (0, 0)),
                  pl.BlockSpec((1, ROUTER_LANES), lambda i: (0, 0))],
        out_specs=[pl.BlockSpec((tm * ROW_CHUNKS, LANES), lambda i: (i, 0)),
                   pl.BlockSpec((tm, ROUTER_LANES), lambda i: (i, 0)),
                   pl.BlockSpec((SUBLANES, ROUTER_LANES), lambda i: (0, 0))],
        out_shape=[jax.ShapeDtypeStruct((n * ROW_CHUNKS, LANES), F32),
                   jax.ShapeDtypeStruct((n, ROUTER_LANES), F32),
                   jax.ShapeDtypeStruct((SUBLANES, ROUTER_LANES), F32)],
        scratch_shapes=[pltpu.VMEM((SUBLANES, ROUTER_LANES), F32)],
        compiler_params=_cparams(("arbitrary",)),
        name="router",
    )(yp, ys, g.reshape(1, d), w_router, b_router)


ROW_CHUNKS = D_MODEL // LANES


def _store_row_tiles(ref, x):
    for c in range(ROW_CHUNKS):
        ref[pl.ds(c, x.shape[0], stride=ROW_CHUNKS), :] = x[:, c * LANES:(c + 1) * LANES]


def _load_row_tiles(ref, rows):
    return jnp.concatenate(
        [ref[pl.ds(c, rows, stride=ROW_CHUNKS), :] for c in range(ROW_CHUNKS)], axis=1)


def _row_copy(src, i, dst, j, sem):
    def lines(k):
        start = k * ROW_CHUNKS
        if not isinstance(k, int):
            start = pl.multiple_of(start, ROW_CHUNKS)
        return pl.ds(start, ROW_CHUNKS)
    return pltpu.make_async_copy(src.at[lines(i)], dst.at[lines(j)], sem)


def _moe_kernel(te_ref, na_ref, src_ref, x_hbm, wg_ref, wu_ref, wd_ref, ys_ref, buf, sem):
    del te_ref
    g = pl.program_id(0)
    n_act = na_ref[0]

    def start_rows(tile, slot):
        for r in range(TM_MOE):
            _row_copy(x_hbm, src_ref[tile * TM_MOE + r], buf.at[slot], r, sem.at[slot]).start()

    def wait_rows(slot):
        for r in range(TM_MOE):
            _row_copy(x_hbm, 0, buf.at[slot], r, sem.at[slot]).wait()

    @pl.when(g == 0)
    def _():
        def body(r, c):
            _row_copy(x_hbm, src_ref[r], buf.at[0], r, sem.at[0]).start()
            return c
        lax.fori_loop(0, TM_MOE, body, 0, unroll=8)

    @pl.when(g < n_act)
    def _():
        slot = g % 2
        start_rows(jnp.minimum(g + 1, n_act - 1), 1 - slot)
        wait_rows(slot)
        x = _load_row_tiles(buf.at[slot], TM_MOE).astype(BF16)
        a = _dot(x, wg_ref[...].astype(BF16))
        u = _dot(x, wu_ref[...].astype(BF16))
        h = (a * jax.nn.sigmoid(a)) * u
        _store_row_tiles(ys_ref, _dot(h.astype(BF16), wd_ref[...].astype(BF16)))

    @pl.when(g == n_act)
    def _():
        wait_rows(g % 2)

    @pl.when(g >= n_act)
    def _():
        ys_ref[...] = jnp.zeros_like(ys_ref)


def _moe(tile_expert, n_active, src, xn, w_gate, w_up, w_down, layer):
    d = D_MODEL
    n_tiles = src.shape[0] // TM_MOE + 1
    wsel = lambda g, te, na, src: (layer, te[g], 0, 0)
    return pl.pallas_call(
        _moe_kernel,
        grid_spec=pltpu.PrefetchScalarGridSpec(
            num_scalar_prefetch=3, grid=(n_tiles,),
            in_specs=[pl.BlockSpec(memory_space=pl.ANY),
                      pl.BlockSpec((None, None, d, EXPERT_FF), wsel),
                      pl.BlockSpec((None, None, d, EXPERT_FF), wsel),
                      pl.BlockSpec((None, None, EXPERT_FF, d), wsel)],
            out_specs=pl.BlockSpec((TM_MOE * ROW_CHUNKS, LANES), lambda g, te, na, src: (g, 0)),
            scratch_shapes=[pltpu.VMEM((2, TM_MOE * ROW_CHUNKS, LANES), F32),
                            pltpu.SemaphoreType.DMA((2,))]),
        out_shape=jax.ShapeDtypeStruct((n_tiles * TM_MOE * ROW_CHUNKS, LANES), F32),
        compiler_params=_cparams(("arbitrary",)),
        name="moe_ffn",
    )(tile_expert, n_active, src, xn, w_gate, w_up, w_down)


def _moe_sample_kernel(ue_ref, nu_ref, x_ref, info_ref, y_ref, g_ref, wg_ref, wu_ref, wd_ref,
                       xo_ref, xno_ref, acc_ref):
    s = pl.program_id(0)

    @pl.when(s == 0)
    def _():
        acc_ref[...] = y_ref[...]

    @pl.when(s < nu_ref[0])
    def _():
        x = _load_row_tiles(x_ref, y_ref.shape[0])
        a = _dot3(x, wg_ref[...])
        u = _dot3(x, wu_ref[...])
        out = _dot3((a * jax.nn.sigmoid(a)) * u, wd_ref[...])
        e = ue_ref[s].astype(F32)
        w = (jnp.where(info_ref[:, 0:1] == e, info_ref[:, 4:5], 0.0)
             + jnp.where(info_ref[:, 1:2] == e, info_ref[:, 5:6], 0.0))
        acc_ref[...] += w * out

    @pl.when(s == pl.num_programs(0) - 1)
    def _():
        xo_ref[...] = acc_ref[...]
        xno_ref[...] = _rms(acc_ref[...], g_ref[...])


def _moe_sample(xn2, info, y, g, w_gate, w_up, w_down, layer, row0):
    nb, d = y.shape
    blk0 = row0 // nb
    e_s = info[row0:, 0:2].astype(jnp.int32).reshape(-1)
    used = jnp.zeros((N_EXPERTS,), jnp.int32).at[e_s].set(1)
    n_used = jnp.sum(used)
    order = jnp.argsort(1 - used, stable=True).astype(jnp.int32)
    used_list = jnp.where(jnp.arange(N_EXPERTS) < n_used, order, order[n_used - 1])
    rows = lambda s, ue, nu: (blk0, 0)
    fixed = lambda s, ue, nu: (0, 0)
    wsel = lambda s, ue, nu: (layer, ue[s], 0, 0)
    return pl.pallas_call(
        _moe_sample_kernel,
        grid_spec=pltpu.PrefetchScalarGridSpec(
            num_scalar_prefetch=2, grid=(N_EXPERTS,),
            in_specs=[pl.BlockSpec((nb * ROW_CHUNKS, LANES), rows),
                      pl.BlockSpec((nb, ROUTER_LANES), rows),
                      pl.BlockSpec((nb, d), fixed),
                      pl.BlockSpec((1, d), fixed),
                      pl.BlockSpec((None, None, d, EXPERT_FF), wsel),
                      pl.BlockSpec((None, None, d, EXPERT_FF), wsel),
                      pl.BlockSpec((None, None, EXPERT_FF, d), wsel)],
            out_specs=[pl.BlockSpec((nb, d), fixed), pl.BlockSpec((nb, d), fixed)],
            scratch_shapes=[pltpu.VMEM((nb, d), F32)]),
        out_shape=[jax.ShapeDtypeStruct((nb, d), F32), jax.ShapeDtypeStruct((nb, d), F32)],
        compiler_params=_cparams(("arbitrary",)),
        name="moe_sample",
    )(used_list, n_used.reshape(1).astype(jnp.int32), xn2, info, y, g.reshape(1, d),
      w_gate, w_up, w_down)


def _combine_kernel(pos_ref, y_ref, info_ref, g_ref, ys_hbm, *rest, rows, write_x, blk0):
    if write_x:
        xo_ref, xno_ref, buf0, buf1, sem = rest
    else:
        xno_ref, buf0, buf1, sem = rest
    base = (blk0 + pl.program_id(0)) * rows

    def issue(r, c):
        t = base + r
        _row_copy(ys_hbm, pos_ref[2 * t], buf0, r, sem).start()
        _row_copy(ys_hbm, pos_ref[2 * t + 1], buf1, r, sem).start()
        return c

    def drain(r, c):
        t = base + r
        _row_copy(ys_hbm, pos_ref[2 * t], buf0, r, sem).wait()
        _row_copy(ys_hbm, pos_ref[2 * t + 1], buf1, r, sem).wait()
        return c

    lax.fori_loop(0, rows, issue, 0)
    lax.fori_loop(0, rows, drain, 0)
    xnew = (y_ref[...] + info_ref[:, 4:5] * _load_row_tiles(buf0, rows)
            + info_ref[:, 5:6] * _load_row_tiles(buf1, rows))
    if write_x:
        xo_ref[...] = xnew
    xno_ref[...] = _rms(xnew, g_ref[...]).astype(xno_ref.dtype)


def _combine(pos, y, info, g, ys, norm_dtype, write_x, row0):
    n, d = y.shape
    rows = min(TM_COMBINE, n)
    assert row0 % rows == 0
    blk0 = row0 // rows
    tile = pl.BlockSpec((rows, d), lambda i, pos: (i, 0))
    out_specs = [tile]
    out_shape = [jax.ShapeDtypeStruct((n, d), norm_dtype)]
    if write_x:
        out_specs = [tile] + out_specs
        out_shape = [jax.ShapeDtypeStruct((n, d), F32)] + out_shape
    return pl.pallas_call(
        functools.partial(_combine_kernel, rows=rows, write_x=write_x, blk0=blk0),
        grid_spec=pltpu.PrefetchScalarGridSpec(
            num_scalar_prefetch=1, grid=(n // rows,),
            in_specs=[tile,
                      pl.BlockSpec((rows, ROUTER_LANES), lambda i, pos: (blk0 + i, 0)),
                      pl.BlockSpec((1, d), lambda i, pos: (0, 0)),
                      pl.BlockSpec(memory_space=pl.ANY)],
            out_specs=out_specs,
            scratch_shapes=[pltpu.VMEM((rows * ROW_CHUNKS, LANES), F32),
                            pltpu.VMEM((rows * ROW_CHUNKS, LANES), F32),
                            pltpu.SemaphoreType.DMA(())]),
        out_shape=out_shape,
        compiler_params=_cparams(("arbitrary",)),
        name="combine",
    )(pos, y, info, g.reshape(1, d), ys)


def _plan(info, cnt, n_tiles):
    count = cnt[0, :N_EXPERTS].astype(jnp.int32)
    tiles = (count + TM_MOE - 1) // TM_MOE
    tile_end = jnp.cumsum(tiles)
    row_off = (tile_end - tiles) * TM_MOE
    n_active = tile_end[-1]
    pos = (row_off[info[:, 0:2].astype(jnp.int32)] + info[:, 2:4].astype(jnp.int32)).reshape(-1)
    token = jnp.arange(pos.shape[0], dtype=jnp.int32) // 2
    src = jnp.zeros((n_tiles * TM_MOE,), jnp.int32).at[pos].set(token, unique_indices=True)
    g = jnp.minimum(jnp.arange(n_tiles + 1, dtype=jnp.int32), n_active - 1)
    tile_expert = jnp.sum((g[:, None] >= tile_end[None, :]).astype(jnp.int32), axis=1)
    return pos, src, tile_expert.astype(jnp.int32), n_active.reshape(1).astype(jnp.int32)


def kernel(x_prompt, x_sample, state_conv, cache_win_k, cache_win_v, state_pool, norm1_g, w_in,
           conv_w, attn_sinks, sgu_ln_g, sgu_ln_b, sgu_w, sgu_b, pool_w, pool_scale, w_branch, w_o,
           norm2_g, router_group_w, router_group_b, router_expert_w, router_expert_b, moe_w_gate,
           moe_w_up, moe_w_down, final_norm_g):
    batch, seq, d = x_prompt.shape
    assert seq % TB == 0 and seq >= 2 * TB and d == D_MODEL
    nb = x_sample.shape[0]
    depth = w_in.shape[0]
    n_p = batch * seq
    kv_w = N_KV * HEAD_DIM
    n_slots = 2 * (n_p + nb)
    n_tiles = (n_slots + N_EXPERTS * (TM_MOE - 1) + TM_MOE - 1) // TM_MOE

    xp = x_prompt.reshape(n_p, d)
    xs_ = x_sample.reshape(nb, d)
    xn_p = _rmsnorm(xp, norm1_g[0], BF16)
    xn_s = _rmsnorm(xs_, norm1_g[0], F32)

    pad = ROUTER_LANES - N_EXPERTS - N_EXP_GROUPS
    conv_p, k_p, v_p, pool_p = [], [], [], []
    conv_s, k_s, v_s, pool_s, chunk_s = [], [], [], [], []
    y_prompt = y_sample = None
    for l in range(depth):
        lw = dict(attn_sinks=attn_sinks[l], conv_w=conv_w[l], sgu_ln_g=sgu_ln_g[l],
                  sgu_ln_b=sgu_ln_b[l], sgu_w=sgu_w[l], sgu_b=sgu_b[l], pool_w=pool_w[l],
                  pool_scale=pool_scale[l])
        w_router = jnp.concatenate(
            [router_expert_w[l], router_group_w[l], jnp.zeros((d, pad), F32)], axis=1)
        b_router = jnp.concatenate(
            [router_expert_b[l], router_group_b[l], jnp.zeros((pad,), F32)]).reshape(1, -1)
        last = l == depth - 1
        g_next = final_norm_g if last else norm1_g[l + 1]

        zs_p = _in_proj(xn_p, w_in, l, False)
        outs_p, c_st, k_st, v_st, p_st = _branch_prompt(zs_p, batch, seq, lw)
        conv_p.append(c_st)
        k_p.append(k_st.reshape(batch, WINDOW, N_KV, HEAD_DIM))
        v_p.append(v_st.reshape(batch, WINDOW, N_KV, HEAD_DIM))
        pool_p.append(p_st)
        merged_p = _merge(xn_p, outs_p, w_in, w_branch, l, False)
        yp = _out_proj(merged_p, w_o, xp, l, False)
        if not last:
            x_t = xp.reshape(batch, seq, d)[:, seq - 2 * TB:].reshape(batch * 2 * TB, d)
            xn_t = _rmsnorm(x_t, norm1_g[l], F32)
            zs_t = _in_proj(xn_t, w_in, l, True)
            outs_t = _branch_prompt(zs_t, batch, seq, lw, hi=True, t_start=seq // TB - 2)[0]
            second = lambda a: a.reshape(batch, 2, TB, a.shape[-1])[:, 1].reshape(batch * TB, -1)
            outs_t = jnp.stack([second(outs_t[b]) for b in range(N_BRANCH)])
            merged_t = _merge(second(xn_t), outs_t, w_in, w_branch, l, True)
            y_t = _out_proj(merged_t, w_o, second(x_t), l, True)
            yp = _put_last_blocks(yp, y_t, batch, seq)

        zs_s = _in_proj(xn_s, w_in, l, True)
        outs_s, c_st, k_st, v_st, p_st, cv = _branch_sample(
            zs_s, lw, jnp.swapaxes(state_conv[l], 0, 1),
            cache_win_k[l].reshape(nb, WINDOW, kv_w), cache_win_v[l].reshape(nb, WINDOW, kv_w),
            jnp.swapaxes(state_pool[l], 0, 1))
        conv_s.append(jnp.swapaxes(c_st, 0, 1))
        k_s.append(k_st.reshape(nb, WINDOW, N_KV, HEAD_DIM))
        v_s.append(v_st.reshape(nb, WINDOW, N_KV, HEAD_DIM))
        pool_s.append(jnp.swapaxes(p_st, 0, 1))
        chunk_s.append(cv.reshape(nb, 1, BRANCH_W))
        merged_s = _merge(xn_s, outs_s, w_in, w_branch, l, True)
        ys_ = _out_proj(merged_s, w_o, xs_, l, True)

        xn2, info, cnt = _router(yp, ys_, norm2_g[l], w_router, b_router)
        pos, src, tile_expert, n_active = _plan(info, cnt, n_tiles)
        ffn = _moe(tile_expert, n_active, src, xn2, moe_w_gate, moe_w_up, moe_w_down, l)
        if last:
            (y_prompt,) = _combine(pos, yp, info, g_next, ffn, F32, False, 0)
            (y_sample,) = _combine(pos, ys_, info, g_next, ffn, F32, False, n_p)
        else:
            xp, xn_p = _combine(pos, yp, info, g_next, ffn, BF16, True, 0)
            xs_, xn_s = _moe_sample(xn2, info, ys_, g_next, moe_w_gate, moe_w_up, moe_w_down, l, n_p)

    return (y_prompt.reshape(batch, seq, d), y_sample.reshape(nb, 1, d),
            jnp.stack(conv_p), jnp.stack(k_p), jnp.stack(v_p), jnp.stack(pool_p),
            jnp.stack(conv_s), jnp.stack(k_s), jnp.stack(v_s), jnp.stack(pool_s),
            jnp.stack(chunk_s))
```

```python
import functools

import jax
import jax.numpy as jnp
from jax import lax
from jax.experimental import pallas as pl
from jax.experimental.pallas import tpu as pltpu

F32 = jnp.float32
BF16 = jnp.bfloat16

D_MODEL = 2048
BRANCH_W = 512
N_BRANCH = 4
CONV_W = 3
HEAD_DIM = 64
N_HEADS = 8
N_KV = 2
GQA_G = 4
WINDOW = 128
ROPE_THETA = 500000.0
ROPE_DIM = 16
CHUNK = 128
SGU_GROUPS = 4
POOL_WINDOWS = (2, 4, 8, 16)
POOL_MAX = 16
N_EXP_GROUPS = 4
EXP_PER_GROUP = 8
N_EXPERTS = 32
EXPERT_FF = 512
EPS = 1e-6
PAST_LEN = 16384
NEG = -3.0e38

A_H_OFF = 0
A_C_OFF = 512
A_B_OFF = 1024
Q_OFF = 1536
K_OFF = 2048
V_OFF = 2176
C_OFF = 2304
D_OFF = 3328
G_OFF = 3840

LANES = 128
SUBLANES = 8
VMEM_LIMIT = 52 * 1024 * 1024

TM_NORM = 512
TM_PROJ = 1024
TN_PROJ = 768
TM_MERGE = 2048
TN_MERGE = 256
TM_WO = 2048
TN_WO = 512
TM_MOE = 288
TM_COMBINE = 256
TB = 128
ROUTER_LANES = 128


def _cparams(sem):
    return pltpu.CompilerParams(dimension_semantics=sem, vmem_limit_bytes=VMEM_LIMIT)


_NN = (((1,), (0,)), ((), ()))
_NT = (((1,), (1,)), ((), ()))


def _dot(a, b, dims=_NN):
    return lax.dot_general(a, b, dims, preferred_element_type=F32)


def _split(x):
    hi = x.astype(BF16)
    lo = (x - hi.astype(F32)).astype(BF16)
    return hi, lo


def _dot3(a, b, dims=_NN):
    ah, al = _split(a)
    bh, bl = _split(b)
    return _dot(ah, bh, dims) + _dot(al, bh, dims) + _dot(ah, bl, dims)


def _mm(a, w, hi, dims=_NN):
    if hi:
        return _dot3(a.astype(F32), w.astype(F32), dims)
    return _dot(a.astype(BF16), w.astype(BF16), dims)


def _rms(x, g):
    return (x * lax.rsqrt(jnp.mean(x * x, axis=-1, keepdims=True) + EPS)) * g


def _rmsnorm_kernel(x_ref, g_ref, o_ref):
    o_ref[...] = _rms(x_ref[...], g_ref[...]).astype(o_ref.dtype)


def _rmsnorm(x, g, out_dtype):
    n, d = x.shape
    tm = min(TM_NORM, n)
    return pl.pallas_call(
        _rmsnorm_kernel,
        grid=(n // tm,),
        in_specs=[pl.BlockSpec((tm, d), lambda i: (i, 0)),
                  pl.BlockSpec((1, d), lambda i: (0, 0))],
        out_specs=pl.BlockSpec((tm, d), lambda i: (i, 0)),
        out_shape=jax.ShapeDtypeStruct((n, d), out_dtype),
        compiler_params=_cparams(("parallel",)),
        name="rmsnorm",
    )(x, g.reshape(1, d))


def _proj_kernel(x_ref, w_ref, o_ref, *, hi):
    o_ref[...] = _mm(x_ref[...], w_ref[...], hi)


def _in_proj(xn, w_in, layer, hi):
    n, d = xn.shape
    tm = min(TM_PROJ // 2 if hi else TM_PROJ, n)
    return pl.pallas_call(
        functools.partial(_proj_kernel, hi=hi),
        grid=(n // tm, G_OFF // TN_PROJ),
        in_specs=[pl.BlockSpec((tm, d), lambda i, j: (i, 0)),
                  pl.BlockSpec((None, d, TN_PROJ), lambda i, j: (layer, 0, j))],
        out_specs=pl.BlockSpec((tm, TN_PROJ), lambda i, j: (i, j)),
        out_shape=jax.ShapeDtypeStruct((n, G_OFF), F32),
        compiler_params=_cparams(("parallel", "arbitrary")),
        name="in_proj",
    )(xn, w_in)


def _rope_tables(pos):
    half = ROPE_DIM // 2
    inv = jnp.power(jnp.float32(ROPE_THETA), -jnp.arange(half, dtype=F32) * (2.0 / ROPE_DIM))
    ang = pos.astype(F32)[:, None] * inv[None, :]
    cos, sin = jnp.cos(ang), jnp.sin(ang)
    t = pos.shape[0]
    rest = HEAD_DIM - ROPE_DIM
    c = jnp.concatenate([cos, cos, jnp.ones((t, rest), F32)], axis=1)
    s1 = jnp.concatenate([-sin, jnp.zeros((t, HEAD_DIM - half), F32)], axis=1)
    s2 = jnp.concatenate([jnp.zeros((t, half), F32), sin, jnp.zeros((t, rest), F32)], axis=1)
    rep = LANES // HEAD_DIM
    return jnp.tile(c, (1, rep)), jnp.tile(s1, (1, rep)), jnp.tile(s2, (1, rep))


def _rope(x, c, s1, s2):
    half = ROPE_DIM // 2
    return x * c + pltpu.roll(x, LANES - half, 1) * s1 + pltpu.roll(x, half, 1) * s2


def _gelu(x):
    return 0.5 * x * (1.0 + lax.erf(x * 0.7071067811865476))


def _layernorm(v, g, b):
    mu = jnp.mean(v, axis=-1, keepdims=True)
    vc = v - mu
    return vc * lax.rsqrt(jnp.mean(vc * vc, axis=-1, keepdims=True) + EPS) * g + b


def _branch_prompt_kernel(sink_ref, z_ref, rc_ref, rs1_ref, rs2_ref, cw_ref, lng_ref, lnb_ref,
                          sw_ref, sbt_ref, pw_ref, ps_ref,
                          outs_ref, conv_ref, kst_ref, vst_ref, pool_ref,
                          kprev, vprev, ci_ext, p_ext, *, hi, t_start):
    tb = pl.program_id(1)
    halo_c = SUBLANES
    halo_p = POOL_MAX

    @pl.when(tb == 0)
    def _():
        kprev[...] = jnp.zeros_like(kprev)
        vprev[...] = jnp.zeros_like(vprev)
        ci_ext[0:halo_c, :] = jnp.zeros((halo_c, BRANCH_W), F32)
        p_ext[0:halo_p, :] = jnp.zeros((halo_p, BRANCH_W), F32)

    ci = z_ref[:, A_C_OFF:A_B_OFF] * z_ref[:, A_H_OFF:A_C_OFF]
    ci_ext[halo_c:halo_c + TB, :] = ci
    y = cw_ref[2:3, :] * ci
    for j in range(CONV_W - 1):
        y = y + cw_ref[j:j + 1, :] * ci_ext[pl.ds(halo_c - (CONV_W - 1) + j, TB), :]
    outs_ref[0] = (z_ref[:, A_B_OFF:Q_OFF] * y).astype(outs_ref.dtype)
    conv_ref[...] = ci_ext[pl.ds(halo_c + TB - (CONV_W - 1), CONV_W - 1), :]
    ci_ext[0:halo_c, :] = ci[TB - halo_c:, :]

    p = z_ref[:, D_OFF:G_OFF]
    p_ext[halo_p:halo_p + TB, :] = p
    posf = ((t_start + tb) * TB + lax.broadcasted_iota(jnp.int32, (TB, 1), 0)).astype(F32)
    for g, w in enumerate(POOL_WINDOWS):
        sl = slice(g * LANES, (g + 1) * LANES)
        s = p[:, sl]
        for k in range(1, w):
            s = s + p_ext[pl.ds(halo_p - k, TB), sl]
        dlt = s / jnp.minimum(posf + 1.0, float(w)) - p[:, sl]
        yg = _mm(dlt, pw_ref[g], hi) * ps_ref[:, sl]
        outs_ref[3, :, sl] = yg.astype(outs_ref.dtype)
    pool_ref[...] = p_ext[pl.ds(halo_p + TB - (POOL_MAX - 1), POOL_MAX - 1), :]
    p_ext[0:halo_p, :] = p[TB - halo_p:, :]

    ge = _gelu(z_ref[:, C_OFF:D_OFF])
    u = ge[:, :BRANCH_W]
    vn = _layernorm(ge[:, BRANCH_W:], lng_ref[...], lnb_ref[...])
    r_i = lax.broadcasted_iota(jnp.int32, (TB, TB), 0)
    c_i = lax.broadcasted_iota(jnp.int32, (TB, TB), 1)
    for g in range(SGU_GROUPS):
        sl = slice(g * LANES, (g + 1) * LANES)
        wt = jnp.where(c_i <= r_i, sw_ref[g], 0.0)
        f = _mm(wt, vn[:, sl], hi) + sbt_ref[:, g:g + 1]
        outs_ref[2, :, sl] = (u[:, sl] * f).astype(outs_ref.dtype)

    rc, rs1, rs2 = rc_ref[...], rs1_ref[...], rs2_ref[...]
    kr = _rope(z_ref[:, K_OFF:V_OFF], rc, rs1, rs2)
    v = z_ref[:, V_OFF:C_OFF]
    qr = [_rope(z_ref[:, Q_OFF + c * LANES:Q_OFF + (c + 1) * LANES], rc, rs1, rs2)
          for c in range(N_HEADS * HEAD_DIM // LANES)]
    qi = lax.broadcasted_iota(jnp.int32, (TB, 2 * TB), 0)
    kj = lax.broadcasted_iota(jnp.int32, (TB, 2 * TB), 1)
    allowed = (kj > qi) & (kj <= qi + WINDOW) & ((kj >= TB) | (tb > 0))
    kp, vp = kprev[...], vprev[...]
    heads = [None] * N_HEADS
    for hk in range(N_KV):
        hs = slice(hk * HEAD_DIM, (hk + 1) * HEAD_DIM)
        kk = jnp.concatenate([kp[:, hs], kr[:, hs]], axis=0)
        vv = jnp.concatenate([vp[:, hs], v[:, hs]], axis=0)
        if not hi:
            kk, vv = kk.astype(BF16), vv.astype(BF16)
        for g in range(GQA_G):
            hq = hk * GQA_G + g
            per = LANES // HEAD_DIM
            qh = qr[hq // per][:, (hq % per) * HEAD_DIM:(hq % per + 1) * HEAD_DIM]
            s = _mm(qh, kk, hi, _NT) * (HEAD_DIM ** -0.5)
            s = jnp.where(allowed, s, -1e30)
            sink = sink_ref[hq]
            m = jnp.maximum(jnp.max(s, axis=-1, keepdims=True), sink)
            e = jnp.exp(s - m)
            den = jnp.sum(e, axis=-1, keepdims=True) + jnp.exp(sink - m)
            heads[hq] = _mm(e / den, vv, hi)
    outs_ref[1] = jnp.concatenate(heads, axis=1).astype(outs_ref.dtype)
    kst_ref[...] = kr
    vst_ref[...] = v
    kprev[...] = kr
    vprev[...] = v


def _branch_prompt(zs, batch, seq, lw, hi=False, t_start=0):
    n = zs.shape[0]
    nt = n // (batch * TB)
    rc, rs1, rs2 = _rope_tables(jnp.arange(seq))
    full = lambda shape: pl.BlockSpec(shape, lambda b, t: (0,) * len(shape))
    tab = pl.BlockSpec((TB, LANES), lambda b, t: (t_start + t, 0))
    kv_w = N_KV * HEAD_DIM
    return pl.pallas_call(
        functools.partial(_branch_prompt_kernel, hi=hi, t_start=t_start),
        grid=(batch, nt),
        in_specs=[pl.BlockSpec(memory_space=pltpu.SMEM),
                  pl.BlockSpec((TB, G_OFF), lambda b, t: (b * nt + t, 0)),
                  tab, tab, tab,
                  full((CONV_W, BRANCH_W)), full((1, BRANCH_W)), full((1, BRANCH_W)),
                  full((SGU_GROUPS, CHUNK, CHUNK)), full((CHUNK, SGU_GROUPS)),
                  full((len(POOL_WINDOWS), LANES, LANES)), full((1, BRANCH_W))],
        out_specs=[pl.BlockSpec((N_BRANCH, TB, BRANCH_W), lambda b, t: (0, b * nt + t, 0)),
                   pl.BlockSpec((None, CONV_W - 1, BRANCH_W), lambda b, t: (b, 0, 0)),
                   pl.BlockSpec((None, WINDOW, kv_w), lambda b, t: (b, 0, 0)),
                   pl.BlockSpec((None, WINDOW, kv_w), lambda b, t: (b, 0, 0)),
                   pl.BlockSpec((None, POOL_MAX - 1, BRANCH_W), lambda b, t: (b, 0, 0))],
        out_shape=[jax.ShapeDtypeStruct((N_BRANCH, n, BRANCH_W), F32 if hi else BF16),
                   jax.ShapeDtypeStruct((batch, CONV_W - 1, BRANCH_W), F32),
                   jax.ShapeDtypeStruct((batch, WINDOW, kv_w), F32),
                   jax.ShapeDtypeStruct((batch, WINDOW, kv_w), F32),
                   jax.ShapeDtypeStruct((batch, POOL_MAX - 1, BRANCH_W), F32)],
        scratch_shapes=[pltpu.VMEM((TB, kv_w), F32), pltpu.VMEM((TB, kv_w), F32),
                        pltpu.VMEM((SUBLANES + TB, BRANCH_W), F32),
                        pltpu.VMEM((POOL_MAX + TB, BRANCH_W), F32)],
        compiler_params=_cparams(("arbitrary", "arbitrary")),
        name="branch_prompt",
    )(lw["attn_sinks"], zs, rc, rs1, rs2, lw["conv_w"], lw["sgu_ln_g"].reshape(1, -1),
      lw["sgu_ln_b"].reshape(1, -1), lw["sgu_w"], lw["sgu_b"].T, lw["pool_w"],
      lw["pool_scale"].reshape(1, -1))


def _branch_sample_kernel(sink_ref, z_ref, rc_ref, rs1_ref, rs2_ref, cw_ref, lng_ref, lnb_ref,
                          ssc_ref, ssh_ref, pw_ref, ps_ref, cpast_ref, kc_ref, vc_ref, ppast_ref,
                          outs_ref, conv_ref, kst_ref, vst_ref, pool_ref, chunkv_ref,
                          qrot, krot, vnew):
    nb = z_ref.shape[0]

    ci = z_ref[:, A_C_OFF:A_B_OFF] * z_ref[:, A_H_OFF:A_C_OFF]
    y = cw_ref[CONV_W - 1:CONV_W, :] * ci
    for j in range(CONV_W - 1):
        y = y + cw_ref[j:j + 1, :] * cpast_ref[j]
    outs_ref[0] = z_ref[:, A_B_OFF:Q_OFF] * y
    for j in range(CONV_W - 2):
        conv_ref[j] = cpast_ref[j + 1]
    conv_ref[CONV_W - 2] = ci

    p = z_ref[:, D_OFF:G_OFF]
    for g, w in enumerate(POOL_WINDOWS):
        sl = slice(g * LANES, (g + 1) * LANES)
        s = p[:, sl]
        for k in range(1, w):
            s = s + ppast_ref[POOL_MAX - 1 - k, :, sl]
        dlt = s / float(min(PAST_LEN + 1, w)) - p[:, sl]
        outs_ref[3, :, sl] = _dot3(dlt, pw_ref[g]) * ps_ref[:, sl]
    for j in range(POOL_MAX - 2):
        pool_ref[j] = ppast_ref[j + 1]
    pool_ref[POOL_MAX - 2] = p

    ge = _gelu(z_ref[:, C_OFF:D_OFF])
    vn = _layernorm(ge[:, BRANCH_W:], lng_ref[...], lnb_ref[...])
    outs_ref[2] = ge[:, :BRANCH_W] * (vn * ssc_ref[...] + ssh_ref[...])
    chunkv_ref[...] = vn

    rc, rs1, rs2 = rc_ref[...], rs1_ref[...], rs2_ref[...]
    krot[...] = _rope(z_ref[:, K_OFF:V_OFF], rc, rs1, rs2)
    vnew[...] = z_ref[:, V_OFF:C_OFF]
    for c in range(N_HEADS * HEAD_DIM // LANES):
        cs = slice(c * LANES, (c + 1) * LANES)
        qrot[:, cs] = _rope(z_ref[:, Q_OFF + c * LANES:Q_OFF + (c + 1) * LANES], rc, rs1, rs2)
    row = lax.broadcasted_iota(jnp.int32, (WINDOW, LANES), 0)
    lo = lax.broadcasted_iota(jnp.int32, (WINDOW, LANES), 1) < HEAD_DIM
    lo1 = lax.broadcasted_iota(jnp.int32, (1, LANES), 1) < HEAD_DIM
    scale = HEAD_DIM ** -0.5

    def body(b, carry):
        kn = jnp.where(row == WINDOW - 1, krot[pl.ds(b, 1), :], pltpu.roll(kc_ref[b], WINDOW - 1, 0))
        vn_b = jnp.where(row == WINDOW - 1, vnew[pl.ds(b, 1), :], pltpu.roll(vc_ref[b], WINDOW - 1, 0))
        kst_ref[b] = kn
        vst_ref[b] = vn_b
        qb = qrot[pl.ds(b, 1), :]
        o = []
        for g in range(GQA_G):
            ca = qb[:, (g // 2) * LANES:(g // 2 + 1) * LANES]
            cb = qb[:, (2 + g // 2) * LANES:(3 + g // 2) * LANES]
            if g % 2 == 0:
                qrow = jnp.where(lo1, ca, pltpu.roll(cb, HEAD_DIM, 1))
            else:
                qrow = jnp.where(lo1, pltpu.roll(ca, HEAD_DIM, 1), cb)
            prod = kn * qrow
            pn = []
            for part, hq in ((jnp.where(lo, prod, 0.0), g), (jnp.where(lo, 0.0, prod), GQA_G + g)):
                s = jnp.sum(part, axis=1, keepdims=True) * scale
                sink = sink_ref[hq]
                m = jnp.maximum(jnp.max(s, axis=0, keepdims=True), sink)
                e = jnp.exp(s - m)
                den = jnp.sum(e, axis=0, keepdims=True) + jnp.exp(sink - m)
                pn.append(e / den)
            o.append(jnp.sum(jnp.where(lo, pn[0], pn[1]) * vn_b, axis=0, keepdims=True))
        sw = lambda x: pltpu.roll(x, HEAD_DIM, 1)
        chunks = [jnp.where(lo1, o[0], sw(o[1])), jnp.where(lo1, o[2], sw(o[3])),
                  jnp.where(lo1, sw(o[0]), o[1]), jnp.where(lo1, sw(o[2]), o[3])]
        outs_ref[1, pl.ds(b, 1), :] = jnp.concatenate(chunks, axis=1)
        return carry

    lax.fori_loop(0, nb, body, 0)


def _branch_sample(zs, lw, cpast, kc, vc, ppast):
    nb = zs.shape[0]
    kv_w = N_KV * HEAD_DIM
    rc, rs1, rs2 = _rope_tables(jnp.full((1,), PAST_LEN, jnp.int32))
    ssc = jnp.repeat(lw["sgu_w"][:, 0, 0], CHUNK).reshape(1, BRANCH_W)
    ssh = jnp.repeat(lw["sgu_b"][:, 0], CHUNK).reshape(1, BRANCH_W)
    vspec = pl.BlockSpec(memory_space=pltpu.VMEM)
    return pl.pallas_call(
        _branch_sample_kernel,
        in_specs=[pl.BlockSpec(memory_space=pltpu.SMEM)] + [vspec] * 15,
        out_specs=[vspec] * 6,
        out_shape=[jax.ShapeDtypeStruct((N_BRANCH, nb, BRANCH_W), F32),
                   jax.ShapeDtypeStruct((CONV_W - 1, nb, BRANCH_W), F32),
                   jax.ShapeDtypeStruct((nb, WINDOW, kv_w), F32),
                   jax.ShapeDtypeStruct((nb, WINDOW, kv_w), F32),
                   jax.ShapeDtypeStruct((POOL_MAX - 1, nb, BRANCH_W), F32),
                   jax.ShapeDtypeStruct((nb, BRANCH_W), F32)],
        scratch_shapes=[pltpu.VMEM((nb, N_HEADS * HEAD_DIM), F32), pltpu.VMEM((nb, kv_w), F32),
                        pltpu.VMEM((nb, kv_w), F32)],
        compiler_params=pltpu.CompilerParams(vmem_limit_bytes=VMEM_LIMIT),
        name="branch_sample",
    )(lw["attn_sinks"], zs, rc, rs1, rs2, lw["conv_w"], lw["sgu_ln_g"].reshape(1, -1),
      lw["sgu_ln_b"].reshape(1, -1), ssc, ssh, lw["pool_w"], lw["pool_scale"].reshape(1, -1),
      cpast, kc, vc, ppast)


def _merge_kernel(xn_ref, outs_ref, wg_ref, wb_ref, o_ref, acc_ref, *, hi):
    nbr = pl.program_id(2)
    gate = jax.nn.sigmoid(_mm(xn_ref[...], wg_ref[...], hi))
    contrib = gate * _mm(outs_ref[...], wb_ref[...], hi)

    @pl.when(nbr == 0)
    def _():
        acc_ref[...] = contrib

    @pl.when(nbr > 0)
    def _():
        acc_ref[...] += contrib

    @pl.when(nbr == N_BRANCH - 1)
    def _():
        o_ref[...] = acc_ref[...].astype(o_ref.dtype)


def _merge(xn, outs, w_in, w_branch, layer, hi):
    n, d = xn.shape
    tm = min(TM_MERGE, n)
    g_blk = G_OFF // TN_MERGE
    per_branch = D_MODEL // TN_MERGE
    return pl.pallas_call(
        functools.partial(_merge_kernel, hi=hi),
        grid=(n // tm, per_branch, N_BRANCH),
        in_specs=[pl.BlockSpec((tm, d), lambda i, j, b: (i, 0)),
                  pl.BlockSpec((None, tm, BRANCH_W), lambda i, j, b: (b, i, 0)),
                  pl.BlockSpec((None, d, TN_MERGE),
                               lambda i, j, b: (layer, 0, g_blk + b * per_branch + j)),
                  pl.BlockSpec((None, None, BRANCH_W, TN_MERGE), lambda i, j, b: (layer, b, 0, j))],
        out_specs=pl.BlockSpec((tm, TN_MERGE), lambda i, j, b: (i, j)),
        out_shape=jax.ShapeDtypeStruct((n, D_MODEL), F32 if hi else BF16),
        scratch_shapes=[pltpu.VMEM((tm, TN_MERGE), F32)],
        compiler_params=_cparams(("parallel", "arbitrary", "arbitrary")),
        name="merge",
    )(xn, outs, w_in, w_branch)


def _wo_kernel(m_ref, w_ref, x_ref, o_ref, *, hi):
    o_ref[...] = x_ref[...] + _mm(m_ref[...], w_ref[...], hi)


def _out_proj(merged, w_o, x, layer, hi):
    n, d = x.shape
    tm = min(TM_WO, n)
    return pl.pallas_call(
        functools.partial(_wo_kernel, hi=hi),
        grid=(n // tm, d // TN_WO),
        in_specs=[pl.BlockSpec((tm, d), lambda i, j: (i, 0)),
                  pl.BlockSpec((None, d, TN_WO), lambda i, j: (layer, 0, j)),
                  pl.BlockSpec((tm, TN_WO), lambda i, j: (i, j))],
        out_specs=pl.BlockSpec((tm, TN_WO), lambda i, j: (i, j)),
        out_shape=jax.ShapeDtypeStruct((n, d), F32),
        compiler_params=_cparams(("parallel", "arbitrary")),
        name="out_proj",
    )(merged, w_o, x)


def _put_rows_kernel(y_hbm, t_ref, o_ref):
    del y_hbm
    o_ref[...] = t_ref[...]


def _put_last_blocks(y, y_tail, batch, seq):
    d = y.shape[1]
    nt = seq // TB
    return pl.pallas_call(
        _put_rows_kernel,
        grid=(batch,),
        in_specs=[pl.BlockSpec(memory_space=pl.ANY),
                  pl.BlockSpec((TB, d), lambda b: (b, 0))],
        out_specs=pl.BlockSpec((TB, d), lambda b: (b * nt + nt - 1, 0)),
        out_shape=jax.ShapeDtypeStruct(y.shape, y.dtype),
        input_output_aliases={0: 0},
        compiler_params=_cparams(("arbitrary",)),
        name="put_last_blocks",
    )(y, y_tail)


def _route_rows(y, g_ref, wr_ref, br_ref, carry_ref):
    tm = y.shape[0]
    xn = _rms(y, g_ref[...])
    logits = _dot3(xn, wr_ref[...]) + br_ref[...]
    lane_i = lax.broadcasted_iota(jnp.int32, (tm, ROUTER_LANES), 1)
    lane = lane_i.astype(F32)
    lane_grp = lax.shift_right_logical(lane_i, 3).astype(F32)
    big = float(ROUTER_LANES)
    is_grp = (lane_i >= N_EXPERTS) & (lane_i < N_EXPERTS + N_EXP_GROUPS)
    lg = jnp.where(is_grp, logits, NEG)
    gmax = jnp.max(lg, axis=-1, keepdims=True)
    gsel = jnp.min(jnp.where(lg == gmax, lane - N_EXPERTS, big), axis=-1, keepdims=True)
    wg = 1.0 / jnp.sum(jnp.where(is_grp, jnp.exp(logits - gmax), 0.0), axis=-1, keepdims=True)
    in_grp = (lane_i < N_EXPERTS) & (lane_grp == gsel)
    v1 = jnp.where(in_grp, logits, NEG)
    m1 = jnp.max(v1, axis=-1, keepdims=True)
    i1 = jnp.min(jnp.where(v1 == m1, lane, big), axis=-1, keepdims=True)
    v2 = jnp.where(lane == i1, NEG, v1)
    m2 = jnp.max(v2, axis=-1, keepdims=True)
    i2 = jnp.min(jnp.where(v2 == m2, lane, big), axis=-1, keepdims=True)
    e2 = jnp.exp(m2 - m1)
    w1 = wg / (1.0 + e2)
    w2 = wg * e2 / (1.0 + e2)

    sel1 = lane == i1
    sel2 = lane == i2
    onehot = jnp.where(sel1 | sel2, 1.0, 0.0)
    r_i = lax.broadcasted_iota(jnp.int32, (tm, tm), 0)
    c_i = lax.broadcasted_iota(jnp.int32, (tm, tm), 1)
    before = jnp.where(c_i < r_i, 1.0, 0.0).astype(BF16)
    counts = _dot(before, onehot.astype(BF16)) + carry_ref[0:1, :]
    r1 = jnp.sum(jnp.where(sel1, counts, 0.0), axis=-1, keepdims=True)
    r2 = jnp.sum(jnp.where(sel2, counts, 0.0), axis=-1, keepdims=True)
    carry_ref[...] = carry_ref[...] + jnp.sum(onehot, axis=0, keepdims=True)

    info = jnp.zeros((tm, ROUTER_LANES), F32)
    for k, val in enumerate((i1, i2, r1, r2, w1, w2)):
        info = jnp.where(lane_i == k, val, info)
    return xn, info


def _router_kernel(yp_ref, ys_ref, g_ref, wr_ref, br_ref, xn_ref, info_ref, cnt_ref, carry_ref, *,
                   prompt_tiles):
    i = pl.program_id(0)

    @pl.when(i == 0)
    def _():
        carry_ref[...] = jnp.zeros_like(carry_ref)

    @pl.when(i < prompt_tiles)
    def _():
        xn, info = _route_rows(yp_ref[...], g_ref, wr_ref, br_ref, carry_ref)
        _store_row_tiles(xn_ref, xn)
        info_ref[...] = info

    @pl.when(i == prompt_tiles)
    def _():
        nb = ys_ref.shape[0]
        xn, info = _route_rows(ys_ref[...], g_ref, wr_ref, br_ref, carry_ref)
        _store_row_tiles(xn_ref, xn)
        info_ref[0:nb, :] = info

    cnt_ref[...] = carry_ref[...]


def _router(yp, ys, g, w_router, b_router):
    n_p, d = yp.shape
    nb = ys.shape[0]
    tm = min(TM_NORM, n_p)
    prompt_tiles = n_p // tm
    n = n_p + nb
    return pl.pallas_call(
        functools.partial(_router_kernel, prompt_tiles=prompt_tiles),
        grid=(prompt_tiles + 1,),
        in_specs=[pl.BlockSpec((tm, d), lambda i: (jnp.minimum(i, prompt_tiles - 1), 0)),
                  pl.BlockSpec((nb, d), lambda i: (0, 0)),
                  pl.BlockSpec((1, d), lambda i: (0, 0)),
                  pl.BlockSpec((d, ROUTER_LANES), lambda i: (0, 0)),
                  pl.BlockSpec((1, ROUTER_LANES), lambda i: (0, 0))],
        out_specs=[pl.BlockSpec((tm * ROW_CHUNKS, LANES), lambda i: (i, 0)),
                   pl.BlockSpec((tm, ROUTER_LANES), lambda i: (i, 0)),
                   pl.BlockSpec((SUBLANES, ROUTER_LANES), lambda i: (0, 0))],
        out_shape=[jax.ShapeDtypeStruct((n * ROW_CHUNKS, LANES), F32),
                   jax.ShapeDtypeStruct((n, ROUTER_LANES), F32),
                   jax.ShapeDtypeStruct((SUBLANES, ROUTER_LANES), F32)],
        scratch_shapes=[pltpu.VMEM((SUBLANES, ROUTER_LANES), F32)],
        compiler_params=_cparams(("arbitrary",)),
        name="router",
    )(yp, ys, g.reshape(1, d), w_router, b_router)


ROW_CHUNKS = D_MODEL // LANES


def _store_row_tiles(ref, x):
    for c in range(ROW_CHUNKS):
        ref[pl.ds(c, x.shape[0], stride=ROW_CHUNKS), :] = x[:, c * LANES:(c + 1) * LANES]


def _load_row_tiles(ref, rows):
    return jnp.concatenate(
        [ref[pl.ds(c, rows, stride=ROW_CHUNKS), :] for c in range(ROW_CHUNKS)], axis=1)


def _row_copy(src, i, dst, j, sem):
    def lines(k):
        start = k * ROW_CHUNKS
        if not isinstance(k, int):
            start = pl.multiple_of(start, ROW_CHUNKS)
        return pl.ds(start, ROW_CHUNKS)
    return pltpu.make_async_copy(src.at[lines(i)], dst.at[lines(j)], sem)


def _moe_kernel(te_ref, na_ref, src_ref, first_ref, ord_ref, next_ref, x_hbm, wg_hbm, wu_hbm, wd_hbm,
                ys_ref, buf, sem, wg_buf, wu_buf, wd_buf, wsem, *, layer):
    g = pl.program_id(0)
    n_act = na_ref[0]

    def weight_copies(e, slot):
        return [pltpu.make_async_copy(hbm.at[layer, e], vm.at[slot], wsem.at[slot])
                for hbm, vm in ((wg_hbm, wg_buf), (wu_hbm, wu_buf), (wd_hbm, wd_buf))]

    def start_rows(tile, slot):
        for r in range(TM_MOE):
            _row_copy(x_hbm, src_ref[tile * TM_MOE + r], buf.at[slot], r, sem.at[slot]).start()

    def wait_rows(slot):
        for r in range(TM_MOE):
            _row_copy(x_hbm, 0, buf.at[slot], r, sem.at[slot]).wait()

    @pl.when(g == 0)
    def _():
        for cp in weight_copies(te_ref[0], 0):
            cp.start()

        def body(r, c):
            _row_copy(x_hbm, src_ref[r], buf.at[0], r, sem.at[0]).start()
            return c
        lax.fori_loop(0, TM_MOE, body, 0, unroll=8)

    @pl.when(g < n_act)
    def _():
        slot = g % 2
        wslot = ord_ref[g] % 2
        start_rows(jnp.minimum(g + 1, n_act - 1), 1 - slot)

        @pl.when(first_ref[g] == 1)
        def _():
            @pl.when(next_ref[g] >= 0)
            def _():
                for cp in weight_copies(next_ref[g], 1 - wslot):
                    cp.start()
            for cp in weight_copies(te_ref[g], wslot):
                cp.wait()

        wait_rows(slot)
        x = _load_row_tiles(buf.at[slot], TM_MOE).astype(BF16)
        a = _dot(x, wg_buf[wslot].astype(BF16))
        u = _dot(x, wu_buf[wslot].astype(BF16))
        h = (a * jax.nn.sigmoid(a)) * u
        _store_row_tiles(ys_ref, _dot(h.astype(BF16), wd_buf[wslot].astype(BF16)))

    @pl.when(g == n_act)
    def _():
        wait_rows(g % 2)

    @pl.when(g >= n_act)
    def _():
        ys_ref[...] = jnp.zeros_like(ys_ref)


def _moe(plan, xn, w_gate, w_up, w_down, layer):
    d = D_MODEL
    n_tiles = plan["src"].shape[0] // TM_MOE + 1
    any_spec = pl.BlockSpec(memory_space=pl.ANY)
    return pl.pallas_call(
        functools.partial(_moe_kernel, layer=layer),
        grid_spec=pltpu.PrefetchScalarGridSpec(
            num_scalar_prefetch=6, grid=(n_tiles,),
            in_specs=[any_spec, any_spec, any_spec, any_spec],
            out_specs=pl.BlockSpec((TM_MOE * ROW_CHUNKS, LANES), lambda g, *_: (g, 0)),
            scratch_shapes=[pltpu.VMEM((2, TM_MOE * ROW_CHUNKS, LANES), F32),
                            pltpu.SemaphoreType.DMA((2,)),
                            pltpu.VMEM((2, d, EXPERT_FF), F32), pltpu.VMEM((2, d, EXPERT_FF), F32),
                            pltpu.VMEM((2, EXPERT_FF, d), F32), pltpu.SemaphoreType.DMA((2,))]),
        out_shape=jax.ShapeDtypeStruct((n_tiles * TM_MOE * ROW_CHUNKS, LANES), F32),
        compiler_params=_cparams(("arbitrary",)),
        name="moe_ffn",
    )(plan["tile_expert"], plan["n_active"], plan["src"], plan["first"], plan["ord"], plan["next"],
      xn, w_gate, w_up, w_down)


def _moe_sample_kernel(ue_ref, nu_ref, x_ref, info_ref, y_ref, g_ref, wg_ref, wu_ref, wd_ref,
                       xo_ref, xno_ref, acc_ref):
    s = pl.program_id(0)

    @pl.when(s == 0)
    def _():
        acc_ref[...] = y_ref[...]

    @pl.when(s < nu_ref[0])
    def _():
        x = _load_row_tiles(x_ref, y_ref.shape[0])
        a = _dot3(x, wg_ref[...])
        u = _dot3(x, wu_ref[...])
        out = _dot3((a * jax.nn.sigmoid(a)) * u, wd_ref[...])
        e = ue_ref[s].astype(F32)
        w = (jnp.where(info_ref[:, 0:1] == e, info_ref[:, 4:5], 0.0)
             + jnp.where(info_ref[:, 1:2] == e, info_ref[:, 5:6], 0.0))
        acc_ref[...] += w * out

    @pl.when(s == pl.num_programs(0) - 1)
    def _():
        xo_ref[...] = acc_ref[...]
        xno_ref[...] = _rms(acc_ref[...], g_ref[...])


def _moe_sample(xn2, info, y, g, w_gate, w_up, w_down, layer, row0):
    nb, d = y.shape
    blk0 = row0 // nb
    e_s = info[row0:, 0:2].astype(jnp.int32).reshape(-1)
    used = jnp.zeros((N_EXPERTS,), jnp.int32).at[e_s].set(1)
    n_used = jnp.sum(used)
    order = jnp.argsort(1 - used, stable=True).astype(jnp.int32)
    used_list = jnp.where(jnp.arange(N_EXPERTS) < n_used, order, order[n_used - 1])
    rows = lambda s, ue, nu: (blk0, 0)
    fixed = lambda s, ue, nu: (0, 0)
    wsel = lambda s, ue, nu: (layer, ue[s], 0, 0)
    return pl.pallas_call(
        _moe_sample_kernel,
        grid_spec=pltpu.PrefetchScalarGridSpec(
            num_scalar_prefetch=2, grid=(N_EXPERTS,),
            in_specs=[pl.BlockSpec((nb * ROW_CHUNKS, LANES), rows),
                      pl.BlockSpec((nb, ROUTER_LANES), rows),
                      pl.BlockSpec((nb, d), fixed),
                      pl.BlockSpec((1, d), fixed),
                      pl.BlockSpec((None, None, d, EXPERT_FF), wsel),
                      pl.BlockSpec((None, None, d, EXPERT_FF), wsel),
                      pl.BlockSpec((None, None, EXPERT_FF, d), wsel)],
            out_specs=[pl.BlockSpec((nb, d), fixed), pl.BlockSpec((nb, d), fixed)],
            scratch_shapes=[pltpu.VMEM((nb, d), F32)]),
        out_shape=[jax.ShapeDtypeStruct((nb, d), F32), jax.ShapeDtypeStruct((nb, d), F32)],
        compiler_params=_cparams(("arbitrary",)),
        name="moe_sample",
    )(used_list, n_used.reshape(1).astype(jnp.int32), xn2, info, y, g.reshape(1, d),
      w_gate, w_up, w_down)


def _combine_kernel(pos_ref, y_ref, info_ref, g_ref, ys_hbm, *rest, rows, write_x, blk0):
    if write_x:
        xo_ref, xno_ref, buf0, buf1, sem = rest
    else:
        xno_ref, buf0, buf1, sem = rest
    i = pl.program_id(0)

    def fetch(step, slot):
        base = (blk0 + step) * rows

        def body(r, c):
            t = base + r
            _row_copy(ys_hbm, pos_ref[2 * t], buf0.at[slot], r, sem.at[slot]).start()
            _row_copy(ys_hbm, pos_ref[2 * t + 1], buf1.at[slot], r, sem.at[slot]).start()
            return c
        lax.fori_loop(0, rows, body, 0, unroll=8)

    @pl.when(i == 0)
    def _():
        fetch(0, 0)

    @pl.when(i + 1 < pl.num_programs(0))
    def _():
        fetch(i + 1, (i + 1) % 2)

    slot = i % 2
    for r in range(rows):
        _row_copy(ys_hbm, 0, buf0.at[slot], r, sem.at[slot]).wait()
        _row_copy(ys_hbm, 0, buf1.at[slot], r, sem.at[slot]).wait()
    xnew = (y_ref[...] + info_ref[:, 4:5] * _load_row_tiles(buf0.at[slot], rows)
            + info_ref[:, 5:6] * _load_row_tiles(buf1.at[slot], rows))
    if write_x:
        xo_ref[...] = xnew
    xno_ref[...] = _rms(xnew, g_ref[...]).astype(xno_ref.dtype)


def _combine(pos, y, info, g, ys, norm_dtype, write_x, row0):
    n, d = y.shape
    rows = min(TM_COMBINE, n)
    assert row0 % rows == 0
    blk0 = row0 // rows
    tile = pl.BlockSpec((rows, d), lambda i, pos: (i, 0))
    out_specs = [tile]
    out_shape = [jax.ShapeDtypeStruct((n, d), norm_dtype)]
    if write_x:
        out_specs = [tile] + out_specs
        out_shape = [jax.ShapeDtypeStruct((n, d), F32)] + out_shape
    return pl.pallas_call(
        functools.partial(_combine_kernel, rows=rows, write_x=write_x, blk0=blk0),
        grid_spec=pltpu.PrefetchScalarGridSpec(
            num_scalar_prefetch=1, grid=(n // rows,),
            in_specs=[tile,
                      pl.BlockSpec((rows, ROUTER_LANES), lambda i, pos: (blk0 + i, 0)),
                      pl.BlockSpec((1, d), lambda i, pos: (0, 0)),
                      pl.BlockSpec(memory_space=pl.ANY)],
            out_specs=out_specs,
            scratch_shapes=[pltpu.VMEM((2, rows * ROW_CHUNKS, LANES), F32),
                            pltpu.VMEM((2, rows * ROW_CHUNKS, LANES), F32),
                            pltpu.SemaphoreType.DMA((2,))]),
        out_shape=out_shape,
        compiler_params=_cparams(("arbitrary",)),
        name="combine",
    )(pos, y, info, g.reshape(1, d), ys)


def _plan(info, cnt, n_tiles):
    count = cnt[0, :N_EXPERTS].astype(jnp.int32)
    tiles = (count + TM_MOE - 1) // TM_MOE
    tile_end = jnp.cumsum(tiles)
    row_off = (tile_end - tiles) * TM_MOE
    n_active = tile_end[-1]
    pos = (row_off[info[:, 0:2].astype(jnp.int32)] + info[:, 2:4].astype(jnp.int32)).reshape(-1)
    token = jnp.arange(pos.shape[0], dtype=jnp.int32) // 2
    src = jnp.zeros((n_tiles * TM_MOE,), jnp.int32).at[pos].set(token, unique_indices=True)
    step = jnp.arange(n_tiles + 1, dtype=jnp.int32)
    g = jnp.minimum(step, n_active - 1)
    tile_expert = jnp.sum((g[:, None] >= tile_end[None, :]).astype(jnp.int32), axis=1)
    prev = jnp.concatenate([jnp.full((1,), -1, jnp.int32), tile_expert[:-1]])
    first = ((step < n_active) & (tile_expert != prev)).astype(jnp.int32)
    ordinal = jnp.cumsum(first) - 1
    ids = jnp.arange(N_EXPERTS, dtype=jnp.int32)
    used_at_or_after = lax.cummin(jnp.where(tiles > 0, ids, N_EXPERTS), reverse=True)
    used_after = jnp.concatenate([used_at_or_after[1:], jnp.full((1,), N_EXPERTS, jnp.int32)])
    next_used = jnp.where(used_after == N_EXPERTS, -1, used_after)
    i32 = lambda a: a.astype(jnp.int32)
    return pos, dict(src=src, tile_expert=i32(tile_expert), n_active=i32(n_active.reshape(1)),
                     first=first, ord=i32(ordinal), next=i32(next_used[tile_expert]))


def kernel(x_prompt, x_sample, state_conv, cache_win_k, cache_win_v, state_pool, norm1_g, w_in,
           conv_w, attn_sinks, sgu_ln_g, sgu_ln_b, sgu_w, sgu_b, pool_w, pool_scale, w_branch, w_o,
           norm2_g, router_group_w, router_group_b, router_expert_w, router_expert_b, moe_w_gate,
           moe_w_up, moe_w_down, final_norm_g):
    batch, seq, d = x_prompt.shape
    assert seq % TB == 0 and seq >= 2 * TB and d == D_MODEL
    nb = x_sample.shape[0]
    depth = w_in.shape[0]
    n_p = batch * seq
    kv_w = N_KV * HEAD_DIM
    n_slots = 2 * (n_p + nb)
    n_tiles = (n_slots + N_EXPERTS * (TM_MOE - 1) + TM_MOE - 1) // TM_MOE

    xp = x_prompt.reshape(n_p, d)
    xs_ = x_sample.reshape(nb, d)
    xn_p = _rmsnorm(xp, norm1_g[0], BF16)
    xn_s = _rmsnorm(xs_, norm1_g[0], F32)

    pad = ROUTER_LANES - N_EXPERTS - N_EXP_GROUPS
    conv_p, k_p, v_p, pool_p = [], [], [], []
    conv_s, k_s, v_s, pool_s, chunk_s = [], [], [], [], []
    y_prompt = y_sample = None
    for l in range(depth):
        lw = dict(attn_sinks=attn_sinks[l], conv_w=conv_w[l], sgu_ln_g=sgu_ln_g[l],
                  sgu_ln_b=sgu_ln_b[l], sgu_w=sgu_w[l], sgu_b=sgu_b[l], pool_w=pool_w[l],
                  pool_scale=pool_scale[l])
        w_router = jnp.concatenate(
            [router_expert_w[l], router_group_w[l], jnp.zeros((d, pad), F32)], axis=1)
        b_router = jnp.concatenate(
            [router_expert_b[l], router_group_b[l], jnp.zeros((pad,), F32)]).reshape(1, -1)
        last = l == depth - 1
        g_next = final_norm_g if last else norm1_g[l + 1]

        zs_p = _in_proj(xn_p, w_in, l, False)
        outs_p, c_st, k_st, v_st, p_st = _branch_prompt(zs_p, batch, seq, lw)
        conv_p.append(c_st)
        k_p.append(k_st.reshape(batch, WINDOW, N_KV, HEAD_DIM))
        v_p.append(v_st.reshape(batch, WINDOW, N_KV, HEAD_DIM))
        pool_p.append(p_st)
        merged_p = _merge(xn_p, outs_p, w_in, w_branch, l, False)
        yp = _out_proj(merged_p, w_o, xp, l, False)
        if not last:
            x_t = xp.reshape(batch, seq, d)[:, seq - 2 * TB:].reshape(batch * 2 * TB, d)
            xn_t = _rmsnorm(x_t, norm1_g[l], F32)
            zs_t = _in_proj(xn_t, w_in, l, True)
            outs_t = _branch_prompt(zs_t, batch, seq, lw, hi=True, t_start=seq // TB - 2)[0]
            second = lambda a: a.reshape(batch, 2, TB, a.shape[-1])[:, 1].reshape(batch * TB, -1)
            outs_t = jnp.stack([second(outs_t[b]) for b in range(N_BRANCH)])
            merged_t = _merge(second(xn_t), outs_t, w_in, w_branch, l, True)
            y_t = _out_proj(merged_t, w_o, second(x_t), l, True)
            yp = _put_last_blocks(yp, y_t, batch, seq)

        zs_s = _in_proj(xn_s, w_in, l, True)
        outs_s, c_st, k_st, v_st, p_st, cv = _branch_sample(
            zs_s, lw, jnp.swapaxes(state_conv[l], 0, 1),
            cache_win_k[l].reshape(nb, WINDOW, kv_w), cache_win_v[l].reshape(nb, WINDOW, kv_w),
            jnp.swapaxes(state_pool[l], 0, 1))
        conv_s.append(jnp.swapaxes(c_st, 0, 1))
        k_s.append(k_st.reshape(nb, WINDOW, N_KV, HEAD_DIM))
        v_s.append(v_st.reshape(nb, WINDOW, N_KV, HEAD_DIM))
        pool_s.append(jnp.swapaxes(p_st, 0, 1))
        chunk_s.append(cv.reshape(nb, 1, BRANCH_W))
        merged_s = _merge(xn_s, outs_s, w_in, w_branch, l, True)
        ys_ = _out_proj(merged_s, w_o, xs_, l, True)

        xn2, info, cnt = _router(yp, ys_, norm2_g[l], w_router, b_router)
        pos, plan = _plan(info, cnt, n_tiles)
        ffn = _moe(plan, xn2, moe_w_gate, moe_w_up, moe_w_down, l)
        if last:
            (y_prompt,) = _combine(pos, yp, info, g_next, ffn, F32, False, 0)
            (y_sample,) = _combine(pos, ys_, info, g_next, ffn, F32, False, n_p)
        else:
            xp, xn_p = _combine(pos, yp, info, g_next, ffn, BF16, True, 0)
            xs_, xn_s = _moe_sample(xn2, info, ys_, g_next, moe_w_gate, moe_w_up, moe_w_down, l, n_p)

    return (y_prompt.reshape(batch, seq, d), y_sample.reshape(nb, 1, d),
            jnp.stack(conv_p), jnp.stack(k_p), jnp.stack(v_p), jnp.stack(pool_p),
            jnp.stack(conv_s), jnp.stack(k_s), jnp.stack(v_s), jnp.stack(pool_s),
            jnp.stack(chunk_s))
```

```python
import functools

import jax
import jax.numpy as jnp
from jax import lax
from jax.experimental import pallas as pl
from jax.experimental.pallas import tpu as pltpu

F32 = jnp.float32
BF16 = jnp.bfloat16

D_MODEL = 2048
BRANCH_W = 512
N_BRANCH = 4
CONV_W = 3
HEAD_DIM = 64
N_HEADS = 8
N_KV = 2
GQA_G = 4
WINDOW = 128
ROPE_THETA = 500000.0
ROPE_DIM = 16
CHUNK = 128
SGU_GROUPS = 4
POOL_WINDOWS = (2, 4, 8, 16)
POOL_MAX = 16
N_EXP_GROUPS = 4
EXP_PER_GROUP = 8
N_EXPERTS = 32
EXPERT_FF = 512
EPS = 1e-6
PAST_LEN = 16384
NEG = -3.0e38

A_H_OFF = 0
A_C_OFF = 512
A_B_OFF = 1024
Q_OFF = 1536
K_OFF = 2048
V_OFF = 2176
C_OFF = 2304
D_OFF = 3328
G_OFF = 3840

LANES = 128
SUBLANES = 8
VMEM_LIMIT = 52 * 1024 * 1024

TM_NORM = 512
TM_PROJ = 1024
TN_PROJ = 768
TM_MERGE = 2048
TN_MERGE = 256
TM_WO = 2048
TN_WO = 512
TM_MOE = 288
TM_COMBINE = 256
TB = 128
ROUTER_LANES = 128


def _cparams(sem):
    return pltpu.CompilerParams(dimension_semantics=sem, vmem_limit_bytes=VMEM_LIMIT)


_NN = (((1,), (0,)), ((), ()))
_NT = (((1,), (1,)), ((), ()))


def _dot(a, b, dims=_NN):
    return lax.dot_general(a, b, dims, preferred_element_type=F32)


def _split(x):
    hi = x.astype(BF16)
    lo = (x - hi.astype(F32)).astype(BF16)
    return hi, lo


def _dot3(a, b, dims=_NN):
    ah, al = _split(a)
    bh, bl = _split(b)
    return _dot(ah, bh, dims) + _dot(al, bh, dims) + _dot(ah, bl, dims)


def _mm(a, w, hi, dims=_NN):
    if hi:
        return _dot3(a.astype(F32), w.astype(F32), dims)
    return _dot(a.astype(BF16), w.astype(BF16), dims)


def _rms(x, g):
    return (x * lax.rsqrt(jnp.mean(x * x, axis=-1, keepdims=True) + EPS)) * g


def _rmsnorm_kernel(x_ref, g_ref, o_ref):
    o_ref[...] = _rms(x_ref[...], g_ref[...]).astype(o_ref.dtype)


def _rmsnorm(x, g, out_dtype):
    n, d = x.shape
    tm = min(TM_NORM, n)
    return pl.pallas_call(
        _rmsnorm_kernel,
        grid=(n // tm,),
        in_specs=[pl.BlockSpec((tm, d), lambda i: (i, 0)),
                  pl.BlockSpec((1, d), lambda i: (0, 0))],
        out_specs=pl.BlockSpec((tm, d), lambda i: (i, 0)),
        out_shape=jax.ShapeDtypeStruct((n, d), out_dtype),
        compiler_params=_cparams(("parallel",)),
        name="rmsnorm",
    )(x, g.reshape(1, d))


def _proj_kernel(x_ref, w_ref, o_ref, *, hi):
    o_ref[...] = _mm(x_ref[...], w_ref[...], hi)


def _in_proj(xn, w_in, layer, hi):
    n, d = xn.shape
    tm = min(TM_PROJ // 2 if hi else TM_PROJ, n)
    return pl.pallas_call(
        functools.partial(_proj_kernel, hi=hi),
        grid=(n // tm, G_OFF // TN_PROJ),
        in_specs=[pl.BlockSpec((tm, d), lambda i, j: (i, 0)),
                  pl.BlockSpec((None, d, TN_PROJ), lambda i, j: (layer, 0, j))],
        out_specs=pl.BlockSpec((tm, TN_PROJ), lambda i, j: (i, j)),
        out_shape=jax.ShapeDtypeStruct((n, G_OFF), F32),
        compiler_params=_cparams(("parallel", "arbitrary")),
        name="in_proj",
    )(xn, w_in)


def _rope_tables(pos):
    half = ROPE_DIM // 2
    inv = jnp.power(jnp.float32(ROPE_THETA), -jnp.arange(half, dtype=F32) * (2.0 / ROPE_DIM))
    ang = pos.astype(F32)[:, None] * inv[None, :]
    cos, sin = jnp.cos(ang), jnp.sin(ang)
    t = pos.shape[0]
    rest = HEAD_DIM - ROPE_DIM
    c = jnp.concatenate([cos, cos, jnp.ones((t, rest), F32)], axis=1)
    s1 = jnp.concatenate([-sin, jnp.zeros((t, HEAD_DIM - half), F32)], axis=1)
    s2 = jnp.concatenate([jnp.zeros((t, half), F32), sin, jnp.zeros((t, rest), F32)], axis=1)
    rep = LANES // HEAD_DIM
    return jnp.tile(c, (1, rep)), jnp.tile(s1, (1, rep)), jnp.tile(s2, (1, rep))


def _rope(x, c, s1, s2):
    half = ROPE_DIM // 2
    return x * c + pltpu.roll(x, LANES - half, 1) * s1 + pltpu.roll(x, half, 1) * s2


def _gelu(x):
    return 0.5 * x * (1.0 + lax.erf(x * 0.7071067811865476))


def _layernorm(v, g, b):
    mu = jnp.mean(v, axis=-1, keepdims=True)
    vc = v - mu
    return vc * lax.rsqrt(jnp.mean(vc * vc, axis=-1, keepdims=True) + EPS) * g + b


def _branch_prompt_kernel(sink_ref, z_ref, rc_ref, rs1_ref, rs2_ref, cw_ref, lng_ref, lnb_ref,
                          sw_ref, sbt_ref, pw_ref, ps_ref,
                          outs_ref, conv_ref, kst_ref, vst_ref, pool_ref,
                          kprev, vprev, ci_ext, p_ext, *, hi, t_start):
    tb = pl.program_id(1)
    halo_c = SUBLANES
    halo_p = POOL_MAX

    @pl.when(tb == 0)
    def _():
        kprev[...] = jnp.zeros_like(kprev)
        vprev[...] = jnp.zeros_like(vprev)
        ci_ext[0:halo_c, :] = jnp.zeros((halo_c, BRANCH_W), F32)
        p_ext[0:halo_p, :] = jnp.zeros((halo_p, BRANCH_W), F32)

    ci = z_ref[:, A_C_OFF:A_B_OFF] * z_ref[:, A_H_OFF:A_C_OFF]
    ci_ext[halo_c:halo_c + TB, :] = ci
    y = cw_ref[2:3, :] * ci
    for j in range(CONV_W - 1):
        y = y + cw_ref[j:j + 1, :] * ci_ext[pl.ds(halo_c - (CONV_W - 1) + j, TB), :]
    outs_ref[0] = (z_ref[:, A_B_OFF:Q_OFF] * y).astype(outs_ref.dtype)
    conv_ref[...] = ci_ext[pl.ds(halo_c + TB - (CONV_W - 1), CONV_W - 1), :]
    ci_ext[0:halo_c, :] = ci[TB - halo_c:, :]

    p = z_ref[:, D_OFF:G_OFF]
    p_ext[halo_p:halo_p + TB, :] = p
    posf = ((t_start + tb) * TB + lax.broadcasted_iota(jnp.int32, (TB, 1), 0)).astype(F32)
    for g, w in enumerate(POOL_WINDOWS):
        sl = slice(g * LANES, (g + 1) * LANES)
        s = p[:, sl]
        for k in range(1, w):
            s = s + p_ext[pl.ds(halo_p - k, TB), sl]
        dlt = s / jnp.minimum(posf + 1.0, float(w)) - p[:, sl]
        yg = _mm(dlt, pw_ref[g], hi) * ps_ref[:, sl]
        outs_ref[3, :, sl] = yg.astype(outs_ref.dtype)
    pool_ref[...] = p_ext[pl.ds(halo_p + TB - (POOL_MAX - 1), POOL_MAX - 1), :]
    p_ext[0:halo_p, :] = p[TB - halo_p:, :]

    ge = _gelu(z_ref[:, C_OFF:D_OFF])
    u = ge[:, :BRANCH_W]
    vn = _layernorm(ge[:, BRANCH_W:], lng_ref[...], lnb_ref[...])
    r_i = lax.broadcasted_iota(jnp.int32, (TB, TB), 0)
    c_i = lax.broadcasted_iota(jnp.int32, (TB, TB), 1)
    for g in range(SGU_GROUPS):
        sl = slice(g * LANES, (g + 1) * LANES)
        wt = jnp.where(c_i <= r_i, sw_ref[g], 0.0)
        f = _mm(wt, vn[:, sl], hi) + sbt_ref[:, g:g + 1]
        outs_ref[2, :, sl] = (u[:, sl] * f).astype(outs_ref.dtype)

    rc, rs1, rs2 = rc_ref[...], rs1_ref[...], rs2_ref[...]
    kr = _rope(z_ref[:, K_OFF:V_OFF], rc, rs1, rs2)
    v = z_ref[:, V_OFF:C_OFF]
    qr = [_rope(z_ref[:, Q_OFF + c * LANES:Q_OFF + (c + 1) * LANES], rc, rs1, rs2)
          for c in range(N_HEADS * HEAD_DIM // LANES)]
    qi = lax.broadcasted_iota(jnp.int32, (TB, 2 * TB), 0)
    kj = lax.broadcasted_iota(jnp.int32, (TB, 2 * TB), 1)
    allowed = (kj > qi) & (kj <= qi + WINDOW) & ((kj >= TB) | (tb > 0))
    kp, vp = kprev[...], vprev[...]
    heads = [None] * N_HEADS
    for hk in range(N_KV):
        hs = slice(hk * HEAD_DIM, (hk + 1) * HEAD_DIM)
        kk = jnp.concatenate([kp[:, hs], kr[:, hs]], axis=0)
        vv = jnp.concatenate([vp[:, hs], v[:, hs]], axis=0)
        if not hi:
            kk, vv = kk.astype(BF16), vv.astype(BF16)
        for g in range(GQA_G):
            hq = hk * GQA_G + g
            per = LANES // HEAD_DIM
            qh = qr[hq // per][:, (hq % per) * HEAD_DIM:(hq % per + 1) * HEAD_DIM]
            s = _mm(qh, kk, hi, _NT) * (HEAD_DIM ** -0.5)
            s = jnp.where(allowed, s, -1e30)
            sink = sink_ref[hq]
            m = jnp.maximum(jnp.max(s, axis=-1, keepdims=True), sink)
            e = jnp.exp(s - m)
            den = jnp.sum(e, axis=-1, keepdims=True) + jnp.exp(sink - m)
            heads[hq] = _mm(e / den, vv, hi)
    outs_ref[1] = jnp.concatenate(heads, axis=1).astype(outs_ref.dtype)
    kst_ref[...] = kr
    vst_ref[...] = v
    kprev[...] = kr
    vprev[...] = v


def _branch_prompt(zs, batch, seq, lw, hi=False, t_start=0):
    n = zs.shape[0]
    nt = n // (batch * TB)
    rc, rs1, rs2 = _rope_tables(jnp.arange(seq))
    full = lambda shape: pl.BlockSpec(shape, lambda b, t: (0,) * len(shape))
    tab = pl.BlockSpec((TB, LANES), lambda b, t: (t_start + t, 0))
    kv_w = N_KV * HEAD_DIM
    return pl.pallas_call(
        functools.partial(_branch_prompt_kernel, hi=hi, t_start=t_start),
        grid=(batch, nt),
        in_specs=[pl.BlockSpec(memory_space=pltpu.SMEM),
                  pl.BlockSpec((TB, G_OFF), lambda b, t: (b * nt + t, 0)),
                  tab, tab, tab,
                  full((CONV_W, BRANCH_W)), full((1, BRANCH_W)), full((1, BRANCH_W)),
                  full((SGU_GROUPS, CHUNK, CHUNK)), full((CHUNK, SGU_GROUPS)),
                  full((len(POOL_WINDOWS), LANES, LANES)), full((1, BRANCH_W))],
        out_specs=[pl.BlockSpec((N_BRANCH, TB, BRANCH_W), lambda b, t: (0, b * nt + t, 0)),
                   pl.BlockSpec((None, CONV_W - 1, BRANCH_W), lambda b, t: (b, 0, 0)),
                   pl.BlockSpec((None, WINDOW, kv_w), lambda b, t: (b, 0, 0)),
                   pl.BlockSpec((None, WINDOW, kv_w), lambda b, t: (b, 0, 0)),
                   pl.BlockSpec((None, POOL_MAX - 1, BRANCH_W), lambda b, t: (b, 0, 0))],
        out_shape=[jax.ShapeDtypeStruct((N_BRANCH, n, BRANCH_W), F32 if hi else BF16),
                   jax.ShapeDtypeStruct((batch, CONV_W - 1, BRANCH_W), F32),
                   jax.ShapeDtypeStruct((batch, WINDOW, kv_w), F32),
                   jax.ShapeDtypeStruct((batch, WINDOW, kv_w), F32),
                   jax.ShapeDtypeStruct((batch, POOL_MAX - 1, BRANCH_W), F32)],
        scratch_shapes=[pltpu.VMEM((TB, kv_w), F32), pltpu.VMEM((TB, kv_w), F32),
                        pltpu.VMEM((SUBLANES + TB, BRANCH_W), F32),
                        pltpu.VMEM((POOL_MAX + TB, BRANCH_W), F32)],
        compiler_params=_cparams(("arbitrary", "arbitrary")),
        name="branch_prompt",
    )(lw["attn_sinks"], zs, rc, rs1, rs2, lw["conv_w"], lw["sgu_ln_g"].reshape(1, -1),
      lw["sgu_ln_b"].reshape(1, -1), lw["sgu_w"], lw["sgu_b"].T, lw["pool_w"],
      lw["pool_scale"].reshape(1, -1))


def _branch_sample_kernel(sink_ref, z_ref, rc_ref, rs1_ref, rs2_ref, cw_ref, lng_ref, lnb_ref,
                          ssc_ref, ssh_ref, pw_ref, ps_ref, cpast_ref, kc_ref, vc_ref, ppast_ref,
                          outs_ref, conv_ref, kst_ref, vst_ref, pool_ref, chunkv_ref,
                          qrot, krot, vnew):
    nb = z_ref.shape[0]

    ci = z_ref[:, A_C_OFF:A_B_OFF] * z_ref[:, A_H_OFF:A_C_OFF]
    y = cw_ref[CONV_W - 1:CONV_W, :] * ci
    for j in range(CONV_W - 1):
        y = y + cw_ref[j:j + 1, :] * cpast_ref[j]
    outs_ref[0] = z_ref[:, A_B_OFF:Q_OFF] * y
    for j in range(CONV_W - 2):
        conv_ref[j] = cpast_ref[j + 1]
    conv_ref[CONV_W - 2] = ci

    p = z_ref[:, D_OFF:G_OFF]
    for g, w in enumerate(POOL_WINDOWS):
        sl = slice(g * LANES, (g + 1) * LANES)
        s = p[:, sl]
        for k in range(1, w):
            s = s + ppast_ref[POOL_MAX - 1 - k, :, sl]
        dlt = s / float(min(PAST_LEN + 1, w)) - p[:, sl]
        outs_ref[3, :, sl] = _dot3(dlt, pw_ref[g]) * ps_ref[:, sl]
    for j in range(POOL_MAX - 2):
        pool_ref[j] = ppast_ref[j + 1]
    pool_ref[POOL_MAX - 2] = p

    ge = _gelu(z_ref[:, C_OFF:D_OFF])
    vn = _layernorm(ge[:, BRANCH_W:], lng_ref[...], lnb_ref[...])
    outs_ref[2] = ge[:, :BRANCH_W] * (vn * ssc_ref[...] + ssh_ref[...])
    chunkv_ref[...] = vn

    rc, rs1, rs2 = rc_ref[...], rs1_ref[...], rs2_ref[...]
    krot[...] = _rope(z_ref[:, K_OFF:V_OFF], rc, rs1, rs2)
    vnew[...] = z_ref[:, V_OFF:C_OFF]
    for c in range(N_HEADS * HEAD_DIM // LANES):
        cs = slice(c * LANES, (c + 1) * LANES)
        qrot[:, cs] = _rope(z_ref[:, Q_OFF + c * LANES:Q_OFF + (c + 1) * LANES], rc, rs1, rs2)
    row = lax.broadcasted_iota(jnp.int32, (WINDOW, LANES), 0)
    lo = lax.broadcasted_iota(jnp.int32, (WINDOW, LANES), 1) < HEAD_DIM
    lo1 = lax.broadcasted_iota(jnp.int32, (1, LANES), 1) < HEAD_DIM
    scale = HEAD_DIM ** -0.5

    def body(b, carry):
        kn = jnp.where(row == WINDOW - 1, krot[pl.ds(b, 1), :], pltpu.roll(kc_ref[b], WINDOW - 1, 0))
        vn_b = jnp.where(row == WINDOW - 1, vnew[pl.ds(b, 1), :], pltpu.roll(vc_ref[b], WINDOW - 1, 0))
        kst_ref[b] = kn
        vst_ref[b] = vn_b
        qb = qrot[pl.ds(b, 1), :]
        o = []
        for g in range(GQA_G):
            ca = qb[:, (g // 2) * LANES:(g // 2 + 1) * LANES]
            cb = qb[:, (2 + g // 2) * LANES:(3 + g // 2) * LANES]
            if g % 2 == 0:
                qrow = jnp.where(lo1, ca, pltpu.roll(cb, HEAD_DIM, 1))
            else:
                qrow = jnp.where(lo1, pltpu.roll(ca, HEAD_DIM, 1), cb)
            prod = kn * qrow
            pn = []
            for part, hq in ((jnp.where(lo, prod, 0.0), g), (jnp.where(lo, 0.0, prod), GQA_G + g)):
                s = jnp.sum(part, axis=1, keepdims=True) * scale
                sink = sink_ref[hq]
                m = jnp.maximum(jnp.max(s, axis=0, keepdims=True), sink)
                e = jnp.exp(s - m)
                den = jnp.sum(e, axis=0, keepdims=True) + jnp.exp(sink - m)
                pn.append(e / den)
            o.append(jnp.sum(jnp.where(lo, pn[0], pn[1]) * vn_b, axis=0, keepdims=True))
        sw = lambda x: pltpu.roll(x, HEAD_DIM, 1)
        chunks = [jnp.where(lo1, o[0], sw(o[1])), jnp.where(lo1, o[2], sw(o[3])),
                  jnp.where(lo1, sw(o[0]), o[1]), jnp.where(lo1, sw(o[2]), o[3])]
        outs_ref[1, pl.ds(b, 1), :] = jnp.concatenate(chunks, axis=1)
        return carry

    lax.fori_loop(0, nb, body, 0)


def _branch_sample(zs, lw, cpast, kc, vc, ppast):
    nb = zs.shape[0]
    kv_w = N_KV * HEAD_DIM
    rc, rs1, rs2 = _rope_tables(jnp.full((1,), PAST_LEN, jnp.int32))
    ssc = jnp.repeat(lw["sgu_w"][:, 0, 0], CHUNK).reshape(1, BRANCH_W)
    ssh = jnp.repeat(lw["sgu_b"][:, 0], CHUNK).reshape(1, BRANCH_W)
    vspec = pl.BlockSpec(memory_space=pltpu.VMEM)
    return pl.pallas_call(
        _branch_sample_kernel,
        in_specs=[pl.BlockSpec(memory_space=pltpu.SMEM)] + [vspec] * 15,
        out_specs=[vspec] * 6,
        out_shape=[jax.ShapeDtypeStruct((N_BRANCH, nb, BRANCH_W), F32),
                   jax.ShapeDtypeStruct((CONV_W - 1, nb, BRANCH_W), F32),
                   jax.ShapeDtypeStruct((nb, WINDOW, kv_w), F32),
                   jax.ShapeDtypeStruct((nb, WINDOW, kv_w), F32),
                   jax.ShapeDtypeStruct((POOL_MAX - 1, nb, BRANCH_W), F32),
                   jax.ShapeDtypeStruct((nb, BRANCH_W), F32)],
        scratch_shapes=[pltpu.VMEM((nb, N_HEADS * HEAD_DIM), F32), pltpu.VMEM((nb, kv_w), F32),
                        pltpu.VMEM((nb, kv_w), F32)],
        compiler_params=pltpu.CompilerParams(vmem_limit_bytes=VMEM_LIMIT),
        name="branch_sample",
    )(lw["attn_sinks"], zs, rc, rs1, rs2, lw["conv_w"], lw["sgu_ln_g"].reshape(1, -1),
      lw["sgu_ln_b"].reshape(1, -1), ssc, ssh, lw["pool_w"], lw["pool_scale"].reshape(1, -1),
      cpast, kc, vc, ppast)


def _merge_kernel(xn_ref, outs_ref, wg_ref, wb_ref, o_ref, acc_ref, *, hi):
    nbr = pl.program_id(2)
    gate = jax.nn.sigmoid(_mm(xn_ref[...], wg_ref[...], hi))
    contrib = gate * _mm(outs_ref[...], wb_ref[...], hi)

    @pl.when(nbr == 0)
    def _():
        acc_ref[...] = contrib

    @pl.when(nbr > 0)
    def _():
        acc_ref[...] += contrib

    @pl.when(nbr == N_BRANCH - 1)
    def _():
        o_ref[...] = acc_ref[...].astype(o_ref.dtype)


def _merge(xn, outs, w_in, w_branch, layer, hi):
    n, d = xn.shape
    tm = min(TM_MERGE, n)
    g_blk = G_OFF // TN_MERGE
    per_branch = D_MODEL // TN_MERGE
    return pl.pallas_call(
        functools.partial(_merge_kernel, hi=hi),
        grid=(n // tm, per_branch, N_BRANCH),
        in_specs=[pl.BlockSpec((tm, d), lambda i, j, b: (i, 0)),
                  pl.BlockSpec((None, tm, BRANCH_W), lambda i, j, b: (b, i, 0)),
                  pl.BlockSpec((None, d, TN_MERGE),
                               lambda i, j, b: (layer, 0, g_blk + b * per_branch + j)),
                  pl.BlockSpec((None, None, BRANCH_W, TN_MERGE), lambda i, j, b: (layer, b, 0, j))],
        out_specs=pl.BlockSpec((tm, TN_MERGE), lambda i, j, b: (i, j)),
        out_shape=jax.ShapeDtypeStruct((n, D_MODEL), F32 if hi else BF16),
        scratch_shapes=[pltpu.VMEM((tm, TN_MERGE), F32)],
        compiler_params=_cparams(("parallel", "arbitrary", "arbitrary")),
        name="merge",
    )(xn, outs, w_in, w_branch)


def _wo_kernel(m_ref, w_ref, x_ref, o_ref, *, hi):
    o_ref[...] = x_ref[...] + _mm(m_ref[...], w_ref[...], hi)


def _out_proj(merged, w_o, x, layer, hi):
    n, d = x.shape
    tm = min(TM_WO, n)
    return pl.pallas_call(
        functools.partial(_wo_kernel, hi=hi),
        grid=(n // tm, d // TN_WO),
        in_specs=[pl.BlockSpec((tm, d), lambda i, j: (i, 0)),
                  pl.BlockSpec((None, d, TN_WO), lambda i, j: (layer, 0, j)),
                  pl.BlockSpec((tm, TN_WO), lambda i, j: (i, j))],
        out_specs=pl.BlockSpec((tm, TN_WO), lambda i, j: (i, j)),
        out_shape=jax.ShapeDtypeStruct((n, d), F32),
        compiler_params=_cparams(("parallel", "arbitrary")),
        name="out_proj",
    )(merged, w_o, x)


def _put_rows_kernel(y_hbm, t_ref, o_ref):
    del y_hbm
    o_ref[...] = t_ref[...]


def _put_last_blocks(y, y_tail, batch, seq):
    d = y.shape[1]
    nt = seq // TB
    return pl.pallas_call(
        _put_rows_kernel,
        grid=(batch,),
        in_specs=[pl.BlockSpec(memory_space=pl.ANY),
                  pl.BlockSpec((TB, d), lambda b: (b, 0))],
        out_specs=pl.BlockSpec((TB, d), lambda b: (b * nt + nt - 1, 0)),
        out_shape=jax.ShapeDtypeStruct(y.shape, y.dtype),
        input_output_aliases={0: 0},
        compiler_params=_cparams(("arbitrary",)),
        name="put_last_blocks",
    )(y, y_tail)


def _route_rows(y, g_ref, wr_ref, br_ref, carry_ref):
    tm = y.shape[0]
    xn = _rms(y, g_ref[...])
    logits = _dot3(xn, wr_ref[...]) + br_ref[...]
    lane_i = lax.broadcasted_iota(jnp.int32, (tm, ROUTER_LANES), 1)
    lane = lane_i.astype(F32)
    lane_grp = lax.shift_right_logical(lane_i, 3).astype(F32)
    big = float(ROUTER_LANES)
    is_grp = (lane_i >= N_EXPERTS) & (lane_i < N_EXPERTS + N_EXP_GROUPS)
    lg = jnp.where(is_grp, logits, NEG)
    gmax = jnp.max(lg, axis=-1, keepdims=True)
    gsel = jnp.min(jnp.where(lg == gmax, lane - N_EXPERTS, big), axis=-1, keepdims=True)
    wg = 1.0 / jnp.sum(jnp.where(is_grp, jnp.exp(logits - gmax), 0.0), axis=-1, keepdims=True)
    in_grp = (lane_i < N_EXPERTS) & (lane_grp == gsel)
    v1 = jnp.where(in_grp, logits, NEG)
    m1 = jnp.max(v1, axis=-1, keepdims=True)
    i1 = jnp.min(jnp.where(v1 == m1, lane, big), axis=-1, keepdims=True)
    v2 = jnp.where(lane == i1, NEG, v1)
    m2 = jnp.max(v2, axis=-1, keepdims=True)
    i2 = jnp.min(jnp.where(v2 == m2, lane, big), axis=-1, keepdims=True)
    e2 = jnp.exp(m2 - m1)
    w1 = wg / (1.0 + e2)
    w2 = wg * e2 / (1.0 + e2)

    sel1 = lane == i1
    sel2 = lane == i2
    onehot = jnp.where(sel1 | sel2, 1.0, 0.0)
    r_i = lax.broadcasted_iota(jnp.int32, (tm, tm), 0)
    c_i = lax.broadcasted_iota(jnp.int32, (tm, tm), 1)
    before = jnp.where(c_i < r_i, 1.0, 0.0).astype(BF16)
    counts = _dot(before, onehot.astype(BF16)) + carry_ref[0:1, :]
    r1 = jnp.sum(jnp.where(sel1, counts, 0.0), axis=-1, keepdims=True)
    r2 = jnp.sum(jnp.where(sel2, counts, 0.0), axis=-1, keepdims=True)
    carry_ref[...] = carry_ref[...] + jnp.sum(onehot, axis=0, keepdims=True)

    info = jnp.zeros((tm, ROUTER_LANES), F32)
    for k, val in enumerate((i1, i2, r1, r2, w1, w2)):
        info = jnp.where(lane_i == k, val, info)
    return xn, info


def _router_kernel(yp_ref, ys_ref, g_ref, wr_ref, br_ref, xn_ref, info_ref, cnt_ref, zero_hbm,
                   carry_ref, zbuf, zsem, *, prompt_tiles, sorted_tiles, zero_per_step):
    i = pl.program_id(0)
    tile_lines = TM_MOE * ROW_CHUNKS

    @pl.when(i == 0)
    def _():
        carry_ref[...] = jnp.zeros_like(carry_ref)
        zbuf[...] = jnp.zeros_like(zbuf)

    def zero_tile(j):
        k = i * zero_per_step + j
        dst = zero_hbm.at[pl.ds(pl.multiple_of(k * tile_lines, tile_lines), tile_lines)]
        return k, pltpu.make_async_copy(zbuf, dst, zsem)

    for j in range(zero_per_step):
        k, cp = zero_tile(j)

        @pl.when(k < sorted_tiles)
        def _():
            cp.start()

    @pl.when(i < prompt_tiles)
    def _():
        xn, info = _route_rows(yp_ref[...], g_ref, wr_ref, br_ref, carry_ref)
        _store_row_tiles(xn_ref, xn)
        info_ref[...] = info

    @pl.when(i == prompt_tiles)
    def _():
        nb = ys_ref.shape[0]
        xn, info = _route_rows(ys_ref[...], g_ref, wr_ref, br_ref, carry_ref)
        _store_row_tiles(xn_ref, xn)
        info_ref[0:nb, :] = info

    cnt_ref[...] = carry_ref[...]

    for j in range(zero_per_step):
        k, cp = zero_tile(j)

        @pl.when(k < sorted_tiles)
        def _():
            cp.wait()


def _router(yp, ys, g, w_router, b_router, sorted_tiles):
    n_p, d = yp.shape
    nb = ys.shape[0]
    tm = min(TM_NORM, n_p)
    prompt_tiles = n_p // tm
    n = n_p + nb
    tile_lines = TM_MOE * ROW_CHUNKS
    return pl.pallas_call(
        functools.partial(_router_kernel, prompt_tiles=prompt_tiles, sorted_tiles=sorted_tiles,
                          zero_per_step=pl.cdiv(sorted_tiles, prompt_tiles + 1)),
        grid=(prompt_tiles + 1,),
        in_specs=[pl.BlockSpec((tm, d), lambda i: (jnp.minimum(i, prompt_tiles - 1), 0)),
                  pl.BlockSpec((nb, d), lambda i: (0, 0)),
                  pl.BlockSpec((1, d), lambda i: (0, 0)),
                  pl.BlockSpec((d, ROUTER_LANES), lambda i: (0, 0)),
                  pl.BlockSpec((1, ROUTER_LANES), lambda i: (0, 0))],
        out_specs=[pl.BlockSpec((tm * ROW_CHUNKS, LANES), lambda i: (i, 0)),
                   pl.BlockSpec((tm, ROUTER_LANES), lambda i: (i, 0)),
                   pl.BlockSpec((SUBLANES, ROUTER_LANES), lambda i: (0, 0)),
                   pl.BlockSpec(memory_space=pl.ANY)],
        out_shape=[jax.ShapeDtypeStruct((n * ROW_CHUNKS, LANES), F32),
                   jax.ShapeDtypeStruct((n, ROUTER_LANES), F32),
                   jax.ShapeDtypeStruct((SUBLANES, ROUTER_LANES), F32),
                   jax.ShapeDtypeStruct((sorted_tiles * tile_lines, LANES), F32)],
        scratch_shapes=[pltpu.VMEM((SUBLANES, ROUTER_LANES), F32),
                        pltpu.VMEM((tile_lines, LANES), F32), pltpu.SemaphoreType.DMA(())],
        compiler_params=_cparams(("arbitrary",)),
        name="router",
    )(yp, ys, g.reshape(1, d), w_router, b_router)


ROW_CHUNKS = D_MODEL // LANES


def _store_row_tiles(ref, x):
    for c in range(ROW_CHUNKS):
        ref[pl.ds(c, x.shape[0], stride=ROW_CHUNKS), :] = x[:, c * LANES:(c + 1) * LANES]


def _load_row_tiles(ref, rows):
    return jnp.concatenate(
        [ref[pl.ds(c, rows, stride=ROW_CHUNKS), :] for c in range(ROW_CHUNKS)], axis=1)


def _row_copy(src, i, dst, j, sem):
    def lines(k):
        start = k * ROW_CHUNKS
        if not isinstance(k, int):
            start = pl.multiple_of(start, ROW_CHUNKS)
        return pl.ds(start, ROW_CHUNKS)
    return pltpu.make_async_copy(src.at[lines(i)], dst.at[lines(j)], sem)


def _dispatch_kernel(pos_ref, x_ref, xs_in, xs_out, sem, *, rows, n_tokens):
    del xs_in
    base = pl.program_id(0) * rows
    count = jnp.minimum(rows, n_tokens - base)

    def copies(r):
        t = base + r
        return (_row_copy(x_ref, r, xs_out, pos_ref[2 * t], sem),
                _row_copy(x_ref, r, xs_out, pos_ref[2 * t + 1], sem))

    def start(r, c):
        for cp in copies(r):
            cp.start()
        return c

    def wait(r, c):
        for cp in copies(r):
            cp.wait()
        return c

    lax.fori_loop(0, count, start, 0)
    lax.fori_loop(0, count, wait, 0)


def _dispatch(pos, xn, xs_zero):
    n_tokens = xn.shape[0] // ROW_CHUNKS
    rows = TM_NORM
    any_spec = pl.BlockSpec(memory_space=pl.ANY)
    return pl.pallas_call(
        functools.partial(_dispatch_kernel, rows=rows, n_tokens=n_tokens),
        grid_spec=pltpu.PrefetchScalarGridSpec(
            num_scalar_prefetch=1, grid=(pl.cdiv(n_tokens, rows),),
            in_specs=[pl.BlockSpec((rows * ROW_CHUNKS, LANES), lambda i, pos: (i, 0)), any_spec],
            out_specs=any_spec,
            scratch_shapes=[pltpu.SemaphoreType.DMA(())]),
        out_shape=jax.ShapeDtypeStruct(xs_zero.shape, xs_zero.dtype),
        input_output_aliases={2: 0},
        compiler_params=_cparams(("arbitrary",)),
        name="dispatch",
    )(pos, xn, xs_zero)


def _moe_kernel(te_ref, na_ref, first_ref, ord_ref, next_ref, x_ref, wg_hbm, wu_hbm, wd_hbm,
                ys_ref, wg_buf, wu_buf, wd_buf, wsem, *, layer):
    g = pl.program_id(0)
    n_act = na_ref[0]

    def weight_copies(e, slot):
        return [pltpu.make_async_copy(hbm.at[layer, e], vm.at[slot], wsem.at[slot])
                for hbm, vm in ((wg_hbm, wg_buf), (wu_hbm, wu_buf), (wd_hbm, wd_buf))]

    @pl.when(g == 0)
    def _():
        for cp in weight_copies(te_ref[0], 0):
            cp.start()

    @pl.when(g < n_act)
    def _():
        wslot = ord_ref[g] % 2

        @pl.when(first_ref[g] == 1)
        def _():
            @pl.when(next_ref[g] >= 0)
            def _():
                for cp in weight_copies(next_ref[g], 1 - wslot):
                    cp.start()
            for cp in weight_copies(te_ref[g], wslot):
                cp.wait()

        x = _load_row_tiles(x_ref, TM_MOE).astype(BF16)
        a = _dot(x, wg_buf[wslot].astype(BF16))
        u = _dot(x, wu_buf[wslot].astype(BF16))
        h = (a * jax.nn.sigmoid(a)) * u
        _store_row_tiles(ys_ref, _dot(h.astype(BF16), wd_buf[wslot].astype(BF16)))

    @pl.when(g >= n_act)
    def _():
        ys_ref[...] = jnp.zeros_like(ys_ref)


def _moe(plan, xs, w_gate, w_up, w_down, layer):
    d = D_MODEL
    n_tiles = xs.shape[0] // (TM_MOE * ROW_CHUNKS)
    any_spec = pl.BlockSpec(memory_space=pl.ANY)
    tile = (TM_MOE * ROW_CHUNKS, LANES)
    return pl.pallas_call(
        functools.partial(_moe_kernel, layer=layer),
        grid_spec=pltpu.PrefetchScalarGridSpec(
            num_scalar_prefetch=5, grid=(n_tiles,),
            in_specs=[pl.BlockSpec(tile, lambda g, te, na, *_: (jnp.minimum(g, na[0] - 1), 0)),
                      any_spec, any_spec, any_spec],
            out_specs=pl.BlockSpec(tile, lambda g, *_: (g, 0)),
            scratch_shapes=[pltpu.VMEM((2, d, EXPERT_FF), F32), pltpu.VMEM((2, d, EXPERT_FF), F32),
                            pltpu.VMEM((2, EXPERT_FF, d), F32), pltpu.SemaphoreType.DMA((2,))]),
        out_shape=jax.ShapeDtypeStruct(xs.shape, F32),
        compiler_params=_cparams(("arbitrary",)),
        name="moe_ffn",
    )(plan["tile_expert"], plan["n_active"], plan["first"], plan["ord"], plan["next"],
      xs, w_gate, w_up, w_down)


def _moe_sample_kernel(ue_ref, nu_ref, x_ref, info_ref, y_ref, g_ref, wg_ref, wu_ref, wd_ref,
                       xo_ref, xno_ref, acc_ref):
    s = pl.program_id(0)

    @pl.when(s == 0)
    def _():
        acc_ref[...] = y_ref[...]

    @pl.when(s < nu_ref[0])
    def _():
        x = _load_row_tiles(x_ref, y_ref.shape[0])
        a = _dot3(x, wg_ref[...])
        u = _dot3(x, wu_ref[...])
        out = _dot3((a * jax.nn.sigmoid(a)) * u, wd_ref[...])
        e = ue_ref[s].astype(F32)
        w = (jnp.where(info_ref[:, 0:1] == e, info_ref[:, 4:5], 0.0)
             + jnp.where(info_ref[:, 1:2] == e, info_ref[:, 5:6], 0.0))
        acc_ref[...] += w * out

    @pl.when(s == pl.num_programs(0) - 1)
    def _():
        xo_ref[...] = acc_ref[...]
        xno_ref[...] = _rms(acc_ref[...], g_ref[...])


def _moe_sample(xn2, info, y, g, w_gate, w_up, w_down, layer, row0):
    nb, d = y.shape
    blk0 = row0 // nb
    e_s = info[row0:, 0:2].astype(jnp.int32).reshape(-1)
    used = jnp.zeros((N_EXPERTS,), jnp.int32).at[e_s].set(1)
    n_used = jnp.sum(used)
    order = jnp.argsort(1 - used, stable=True).astype(jnp.int32)
    used_list = jnp.where(jnp.arange(N_EXPERTS) < n_used, order, order[n_used - 1])
    rows = lambda s, ue, nu: (blk0, 0)
    fixed = lambda s, ue, nu: (0, 0)
    wsel = lambda s, ue, nu: (layer, ue[s], 0, 0)
    return pl.pallas_call(
        _moe_sample_kernel,
        grid_spec=pltpu.PrefetchScalarGridSpec(
            num_scalar_prefetch=2, grid=(N_EXPERTS,),
            in_specs=[pl.BlockSpec((nb * ROW_CHUNKS, LANES), rows),
                      pl.BlockSpec((nb, ROUTER_LANES), rows),
                      pl.BlockSpec((nb, d), fixed),
                      pl.BlockSpec((1, d), fixed),
                      pl.BlockSpec((None, None, d, EXPERT_FF), wsel),
                      pl.BlockSpec((None, None, d, EXPERT_FF), wsel),
                      pl.BlockSpec((None, None, EXPERT_FF, d), wsel)],
            out_specs=[pl.BlockSpec((nb, d), fixed), pl.BlockSpec((nb, d), fixed)],
            scratch_shapes=[pltpu.VMEM((nb, d), F32)]),
        out_shape=[jax.ShapeDtypeStruct((nb, d), F32), jax.ShapeDtypeStruct((nb, d), F32)],
        compiler_params=_cparams(("arbitrary",)),
        name="moe_sample",
    )(used_list, n_used.reshape(1).astype(jnp.int32), xn2, info, y, g.reshape(1, d),
      w_gate, w_up, w_down)


def _combine_kernel(pos_ref, y_ref, info_ref, g_ref, ys_hbm, *rest, rows, write_x, blk0):
    if write_x:
        xo_ref, xno_ref, buf0, buf1, sem = rest
    else:
        xno_ref, buf0, buf1, sem = rest
    i = pl.program_id(0)

    def fetch(step, slot):
        base = (blk0 + step) * rows

        def body(r, c):
            t = base + r
            _row_copy(ys_hbm, pos_ref[2 * t], buf0.at[slot], r, sem.at[slot]).start()
            _row_copy(ys_hbm, pos_ref[2 * t + 1], buf1.at[slot], r, sem.at[slot]).start()
            return c
        lax.fori_loop(0, rows, body, 0, unroll=8)

    @pl.when(i == 0)
    def _():
        fetch(0, 0)

    @pl.when(i + 1 < pl.num_programs(0))
    def _():
        fetch(i + 1, (i + 1) % 2)

    slot = i % 2
    for r in range(rows):
        _row_copy(ys_hbm, 0, buf0.at[slot], r, sem.at[slot]).wait()
        _row_copy(ys_hbm, 0, buf1.at[slot], r, sem.at[slot]).wait()
    xnew = (y_ref[...] + info_ref[:, 4:5] * _load_row_tiles(buf0.at[slot], rows)
            + info_ref[:, 5:6] * _load_row_tiles(buf1.at[slot], rows))
    if write_x:
        xo_ref[...] = xnew
    xno_ref[...] = _rms(xnew, g_ref[...]).astype(xno_ref.dtype)


def _combine(pos, y, info, g, ys, norm_dtype, write_x, row0):
    n, d = y.shape
    rows = min(TM_COMBINE, n)
    assert row0 % rows == 0
    blk0 = row0 // rows
    tile = pl.BlockSpec((rows, d), lambda i, pos: (i, 0))
    out_specs = [tile]
    out_shape = [jax.ShapeDtypeStruct((n, d), norm_dtype)]
    if write_x:
        out_specs = [tile] + out_specs
        out_shape = [jax.ShapeDtypeStruct((n, d), F32)] + out_shape
    return pl.pallas_call(
        functools.partial(_combine_kernel, rows=rows, write_x=write_x, blk0=blk0),
        grid_spec=pltpu.PrefetchScalarGridSpec(
            num_scalar_prefetch=1, grid=(n // rows,),
            in_specs=[tile,
                      pl.BlockSpec((rows, ROUTER_LANES), lambda i, pos: (blk0 + i, 0)),
                      pl.BlockSpec((1, d), lambda i, pos: (0, 0)),
                      pl.BlockSpec(memory_space=pl.ANY)],
            out_specs=out_specs,
            scratch_shapes=[pltpu.VMEM((2, rows * ROW_CHUNKS, LANES), F32),
                            pltpu.VMEM((2, rows * ROW_CHUNKS, LANES), F32),
                            pltpu.SemaphoreType.DMA((2,))]),
        out_shape=out_shape,
        compiler_params=_cparams(("arbitrary",)),
        name="combine",
    )(pos, y, info, g.reshape(1, d), ys)


def _plan(info, cnt, n_tiles):
    count = cnt[0, :N_EXPERTS].astype(jnp.int32)
    tiles = (count + TM_MOE - 1) // TM_MOE
    tile_end = jnp.cumsum(tiles)
    row_off = (tile_end - tiles) * TM_MOE
    n_active = tile_end[-1]
    pos = (row_off[info[:, 0:2].astype(jnp.int32)] + info[:, 2:4].astype(jnp.int32)).reshape(-1)
    step = jnp.arange(n_tiles, dtype=jnp.int32)
    g = jnp.minimum(step, n_active - 1)
    tile_expert = jnp.sum((g[:, None] >= tile_end[None, :]).astype(jnp.int32), axis=1)
    prev = jnp.concatenate([jnp.full((1,), -1, jnp.int32), tile_expert[:-1]])
    first = ((step < n_active) & (tile_expert != prev)).astype(jnp.int32)
    ordinal = jnp.cumsum(first) - 1
    ids = jnp.arange(N_EXPERTS, dtype=jnp.int32)
    used_at_or_after = lax.cummin(jnp.where(tiles > 0, ids, N_EXPERTS), reverse=True)
    used_after = jnp.concatenate([used_at_or_after[1:], jnp.full((1,), N_EXPERTS, jnp.int32)])
    next_used = jnp.where(used_after == N_EXPERTS, -1, used_after)
    i32 = lambda a: a.astype(jnp.int32)
    return pos, dict(tile_expert=i32(tile_expert), n_active=i32(n_active.reshape(1)),
                     first=first, ord=i32(ordinal), next=i32(next_used[tile_expert]))


def kernel(x_prompt, x_sample, state_conv, cache_win_k, cache_win_v, state_pool, norm1_g, w_in,
           conv_w, attn_sinks, sgu_ln_g, sgu_ln_b, sgu_w, sgu_b, pool_w, pool_scale, w_branch, w_o,
           norm2_g, router_group_w, router_group_b, router_expert_w, router_expert_b, moe_w_gate,
           moe_w_up, moe_w_down, final_norm_g):
    batch, seq, d = x_prompt.shape
    assert seq % TB == 0 and seq >= 2 * TB and d == D_MODEL
    nb = x_sample.shape[0]
    depth = w_in.shape[0]
    n_p = batch * seq
    kv_w = N_KV * HEAD_DIM
    n_slots = 2 * (n_p + nb)
    n_tiles = (n_slots + N_EXPERTS * (TM_MOE - 1) + TM_MOE - 1) // TM_MOE

    xp = x_prompt.reshape(n_p, d)
    xs_ = x_sample.reshape(nb, d)
    xn_p = _rmsnorm(xp, norm1_g[0], BF16)
    xn_s = _rmsnorm(xs_, norm1_g[0], F32)

    pad = ROUTER_LANES - N_EXPERTS - N_EXP_GROUPS
    conv_p, k_p, v_p, pool_p = [], [], [], []
    conv_s, k_s, v_s, pool_s, chunk_s = [], [], [], [], []
    y_prompt = y_sample = None
    for l in range(depth):
        lw = dict(attn_sinks=attn_sinks[l], conv_w=conv_w[l], sgu_ln_g=sgu_ln_g[l],
                  sgu_ln_b=sgu_ln_b[l], sgu_w=sgu_w[l], sgu_b=sgu_b[l], pool_w=pool_w[l],
                  pool_scale=pool_scale[l])
        w_router = jnp.concatenate(
            [router_expert_w[l], router_group_w[l], jnp.zeros((d, pad), F32)], axis=1)
        b_router = jnp.concatenate(
            [router_expert_b[l], router_group_b[l], jnp.zeros((pad,), F32)]).reshape(1, -1)
        last = l == depth - 1
        g_next = final_norm_g if last else norm1_g[l + 1]

        zs_p = _in_proj(xn_p, w_in, l, False)
        outs_p, c_st, k_st, v_st, p_st = _branch_prompt(zs_p, batch, seq, lw)
        conv_p.append(c_st)
        k_p.append(k_st.reshape(batch, WINDOW, N_KV, HEAD_DIM))
        v_p.append(v_st.reshape(batch, WINDOW, N_KV, HEAD_DIM))
        pool_p.append(p_st)
        merged_p = _merge(xn_p, outs_p, w_in, w_branch, l, False)
        yp = _out_proj(merged_p, w_o, xp, l, False)
        if not last:
            x_t = xp.reshape(batch, seq, d)[:, seq - 2 * TB:].reshape(batch * 2 * TB, d)
            xn_t = _rmsnorm(x_t, norm1_g[l], F32)
            zs_t = _in_proj(xn_t, w_in, l, True)
            outs_t = _branch_prompt(zs_t, batch, seq, lw, hi=True, t_start=seq // TB - 2)[0]
            second = lambda a: a.reshape(batch, 2, TB, a.shape[-1])[:, 1].reshape(batch * TB, -1)
            outs_t = jnp.stack([second(outs_t[b]) for b in range(N_BRANCH)])
            merged_t = _merge(second(xn_t), outs_t, w_in, w_branch, l, True)
            y_t = _out_proj(merged_t, w_o, second(x_t), l, True)
            yp = _put_last_blocks(yp, y_t, batch, seq)

        zs_s = _in_proj(xn_s, w_in, l, True)
        outs_s, c_st, k_st, v_st, p_st, cv = _branch_sample(
            zs_s, lw, jnp.swapaxes(state_conv[l], 0, 1),
            cache_win_k[l].reshape(nb, WINDOW, kv_w), cache_win_v[l].reshape(nb, WINDOW, kv_w),
            jnp.swapaxes(state_pool[l], 0, 1))
        conv_s.append(jnp.swapaxes(c_st, 0, 1))
        k_s.append(k_st.reshape(nb, WINDOW, N_KV, HEAD_DIM))
        v_s.append(v_st.reshape(nb, WINDOW, N_KV, HEAD_DIM))
        pool_s.append(jnp.swapaxes(p_st, 0, 1))
        chunk_s.append(cv.reshape(nb, 1, BRANCH_W))
        merged_s = _merge(xn_s, outs_s, w_in, w_branch, l, True)
        ys_ = _out_proj(merged_s, w_o, xs_, l, True)

        xn2, info, cnt, rows_sorted = _router(yp, ys_, norm2_g[l], w_router, b_router, n_tiles)
        pos, plan = _plan(info, cnt, n_tiles)
        rows_sorted = _dispatch(pos, xn2, rows_sorted)
        ffn = _moe(plan, rows_sorted, moe_w_gate, moe_w_up, moe_w_down, l)
        if last:
            (y_prompt,) = _combine(pos, yp, info, g_next, ffn, F32, False, 0)
            (y_sample,) = _combine(pos, ys_, info, g_next, ffn, F32, False, n_p)
        else:
            xp, xn_p = _combine(pos, yp, info, g_next, ffn, BF16, True, 0)
            xs_, xn_s = _moe_sample(xn2, info, ys_, g_next, moe_w_gate, moe_w_up, moe_w_down, l, n_p)

    return (y_prompt.reshape(batch, seq, d), y_sample.reshape(nb, 1, d),
            jnp.stack(conv_p), jnp.stack(k_p), jnp.stack(v_p), jnp.stack(pool_p),
            jnp.stack(conv_s), jnp.stack(k_s), jnp.stack(v_s), jnp.stack(pool_s),
            jnp.stack(chunk_s))
```

```python
import functools

import jax
import jax.numpy as jnp
from jax import lax
from jax.experimental import pallas as pl
from jax.experimental.pallas import tpu as pltpu

F32 = jnp.float32
BF16 = jnp.bfloat16

D_MODEL = 2048
BRANCH_W = 512
N_BRANCH = 4
CONV_W = 3
HEAD_DIM = 64
N_HEADS = 8
N_KV = 2
GQA_G = 4
WINDOW = 128
ROPE_THETA = 500000.0
ROPE_DIM = 16
CHUNK = 128
SGU_GROUPS = 4
POOL_WINDOWS = (2, 4, 8, 16)
POOL_MAX = 16
N_EXP_GROUPS = 4
EXP_PER_GROUP = 8
N_EXPERTS = 32
EXPERT_FF = 512
EPS = 1e-6
PAST_LEN = 16384
NEG = -3.0e38

A_H_OFF = 0
A_C_OFF = 512
A_B_OFF = 1024
Q_OFF = 1536
K_OFF = 2048
V_OFF = 2176
C_OFF = 2304
D_OFF = 3328
G_OFF = 3840

LANES = 128
SUBLANES = 8
VMEM_LIMIT = 52 * 1024 * 1024

TM_NORM = 512
TM_PROJ = 1024
TN_PROJ = 768
TM_MERGE = 2048
TN_MERGE = 256
TM_WO = 2048
TN_WO = 512
TM_MOE = 288
TM_COMBINE = 256
TB = 128
ROUTER_LANES = 128


def _cparams(sem):
    return pltpu.CompilerParams(dimension_semantics=sem, vmem_limit_bytes=VMEM_LIMIT)


_NN = (((1,), (0,)), ((), ()))
_NT = (((1,), (1,)), ((), ()))


def _dot(a, b, dims=_NN):
    return lax.dot_general(a, b, dims, preferred_element_type=F32)


def _split(x):
    hi = x.astype(BF16)
    lo = (x - hi.astype(F32)).astype(BF16)
    return hi, lo


def _dot3(a, b, dims=_NN):
    ah, al = _split(a)
    bh, bl = _split(b)
    return _dot(ah, bh, dims) + _dot(al, bh, dims) + _dot(ah, bl, dims)


def _mm(a, w, hi, dims=_NN):
    if hi:
        return _dot3(a.astype(F32), w.astype(F32), dims)
    return _dot(a.astype(BF16), w.astype(BF16), dims)


def _rms(x, g):
    return (x * lax.rsqrt(jnp.mean(x * x, axis=-1, keepdims=True) + EPS)) * g


def _rmsnorm_kernel(x_ref, g_ref, o_ref):
    o_ref[...] = _rms(x_ref[...], g_ref[...]).astype(o_ref.dtype)


def _rmsnorm(x, g, out_dtype):
    n, d = x.shape
    tm = min(TM_NORM, n)
    return pl.pallas_call(
        _rmsnorm_kernel,
        grid=(n // tm,),
        in_specs=[pl.BlockSpec((tm, d), lambda i: (i, 0)),
                  pl.BlockSpec((1, d), lambda i: (0, 0))],
        out_specs=pl.BlockSpec((tm, d), lambda i: (i, 0)),
        out_shape=jax.ShapeDtypeStruct((n, d), out_dtype),
        compiler_params=_cparams(("parallel",)),
        name="rmsnorm",
    )(x, g.reshape(1, d))


def _proj_kernel(x_ref, w_ref, o_ref, *, hi):
    o_ref[...] = _mm(x_ref[...], w_ref[...], hi)


def _in_proj(xn, w_in, layer, hi):
    n, d = xn.shape
    tm = min(TM_PROJ // 2 if hi else TM_PROJ, n)
    return pl.pallas_call(
        functools.partial(_proj_kernel, hi=hi),
        grid=(n // tm, G_OFF // TN_PROJ),
        in_specs=[pl.BlockSpec((tm, d), lambda i, j: (i, 0)),
                  pl.BlockSpec((None, d, TN_PROJ), lambda i, j: (layer, 0, j))],
        out_specs=pl.BlockSpec((tm, TN_PROJ), lambda i, j: (i, j)),
        out_shape=jax.ShapeDtypeStruct((n, G_OFF), F32),
        compiler_params=_cparams(("parallel", "arbitrary")),
        name="in_proj",
    )(xn, w_in)


def _rope_tables(pos):
    half = ROPE_DIM // 2
    inv = jnp.power(jnp.float32(ROPE_THETA), -jnp.arange(half, dtype=F32) * (2.0 / ROPE_DIM))
    ang = pos.astype(F32)[:, None] * inv[None, :]
    cos, sin = jnp.cos(ang), jnp.sin(ang)
    t = pos.shape[0]
    rest = HEAD_DIM - ROPE_DIM
    c = jnp.concatenate([cos, cos, jnp.ones((t, rest), F32)], axis=1)
    s1 = jnp.concatenate([-sin, jnp.zeros((t, HEAD_DIM - half), F32)], axis=1)
    s2 = jnp.concatenate([jnp.zeros((t, half), F32), sin, jnp.zeros((t, rest), F32)], axis=1)
    rep = LANES // HEAD_DIM
    return jnp.tile(c, (1, rep)), jnp.tile(s1, (1, rep)), jnp.tile(s2, (1, rep))


def _rope(x, c, s1, s2):
    half = ROPE_DIM // 2
    return x * c + pltpu.roll(x, LANES - half, 1) * s1 + pltpu.roll(x, half, 1) * s2


def _gelu(x):
    return 0.5 * x * (1.0 + lax.erf(x * 0.7071067811865476))


def _layernorm(v, g, b):
    mu = jnp.mean(v, axis=-1, keepdims=True)
    vc = v - mu
    return vc * lax.rsqrt(jnp.mean(vc * vc, axis=-1, keepdims=True) + EPS) * g + b


def _branch_prompt_kernel(sink_ref, z_ref, rc_ref, rs1_ref, rs2_ref, cw_ref, lng_ref, lnb_ref,
                          sw_ref, sbt_ref, pw_ref, ps_ref,
                          outs_ref, conv_ref, kst_ref, vst_ref, pool_ref,
                          kprev, vprev, ci_ext, p_ext, *, hi, t_start):
    tb = pl.program_id(1)
    halo_c = SUBLANES
    halo_p = POOL_MAX

    @pl.when(tb == 0)
    def _():
        kprev[...] = jnp.zeros_like(kprev)
        vprev[...] = jnp.zeros_like(vprev)
        ci_ext[0:halo_c, :] = jnp.zeros((halo_c, BRANCH_W), F32)
        p_ext[0:halo_p, :] = jnp.zeros((halo_p, BRANCH_W), F32)

    ci = z_ref[:, A_C_OFF:A_B_OFF] * z_ref[:, A_H_OFF:A_C_OFF]
    ci_ext[halo_c:halo_c + TB, :] = ci
    y = cw_ref[2:3, :] * ci
    for j in range(CONV_W - 1):
        y = y + cw_ref[j:j + 1, :] * ci_ext[pl.ds(halo_c - (CONV_W - 1) + j, TB), :]
    outs_ref[0] = (z_ref[:, A_B_OFF:Q_OFF] * y).astype(outs_ref.dtype)
    conv_ref[...] = ci_ext[pl.ds(halo_c + TB - (CONV_W - 1), CONV_W - 1), :]
    ci_ext[0:halo_c, :] = ci[TB - halo_c:, :]

    p = z_ref[:, D_OFF:G_OFF]
    p_ext[halo_p:halo_p + TB, :] = p
    posf = ((t_start + tb) * TB + lax.broadcasted_iota(jnp.int32, (TB, 1), 0)).astype(F32)
    for g, w in enumerate(POOL_WINDOWS):
        sl = slice(g * LANES, (g + 1) * LANES)
        s = p[:, sl]
        for k in range(1, w):
            s = s + p_ext[pl.ds(halo_p - k, TB), sl]
        dlt = s / jnp.minimum(posf + 1.0, float(w)) - p[:, sl]
        yg = _mm(dlt, pw_ref[g], hi) * ps_ref[:, sl]
        outs_ref[3, :, sl] = yg.astype(outs_ref.dtype)
    pool_ref[...] = p_ext[pl.ds(halo_p + TB - (POOL_MAX - 1), POOL_MAX - 1), :]
    p_ext[0:halo_p, :] = p[TB - halo_p:, :]

    ge = _gelu(z_ref[:, C_OFF:D_OFF])
    u = ge[:, :BRANCH_W]
    vn = _layernorm(ge[:, BRANCH_W:], lng_ref[...], lnb_ref[...])
    r_i = lax.broadcasted_iota(jnp.int32, (TB, TB), 0)
    c_i = lax.broadcasted_iota(jnp.int32, (TB, TB), 1)
    for g in range(SGU_GROUPS):
        sl = slice(g * LANES, (g + 1) * LANES)
        wt = jnp.where(c_i <= r_i, sw_ref[g], 0.0)
        f = _mm(wt, vn[:, sl], hi) + sbt_ref[:, g:g + 1]
        outs_ref[2, :, sl] = (u[:, sl] * f).astype(outs_ref.dtype)

    rc, rs1, rs2 = rc_ref[...], rs1_ref[...], rs2_ref[...]
    kr = _rope(z_ref[:, K_OFF:V_OFF], rc, rs1, rs2)
    v = z_ref[:, V_OFF:C_OFF]
    qr = [_rope(z_ref[:, Q_OFF + c * LANES:Q_OFF + (c + 1) * LANES], rc, rs1, rs2)
          for c in range(N_HEADS * HEAD_DIM // LANES)]
    qi = lax.broadcasted_iota(jnp.int32, (TB, 2 * TB), 0)
    kj = lax.broadcasted_iota(jnp.int32, (TB, 2 * TB), 1)
    allowed = (kj > qi) & (kj <= qi + WINDOW) & ((kj >= TB) | (tb > 0))
    kp, vp = kprev[...], vprev[...]
    heads = [None] * N_HEADS
    for hk in range(N_KV):
        hs = slice(hk * HEAD_DIM, (hk + 1) * HEAD_DIM)
        kk = jnp.concatenate([kp[:, hs], kr[:, hs]], axis=0)
        vv = jnp.concatenate([vp[:, hs], v[:, hs]], axis=0)
        if not hi:
            kk, vv = kk.astype(BF16), vv.astype(BF16)
        for g in range(GQA_G):
            hq = hk * GQA_G + g
            per = LANES // HEAD_DIM
            qh = qr[hq // per][:, (hq % per) * HEAD_DIM:(hq % per + 1) * HEAD_DIM]
            s = _mm(qh, kk, hi, _NT) * (HEAD_DIM ** -0.5)
            s = jnp.where(allowed, s, -1e30)
            sink = sink_ref[hq]
            m = jnp.maximum(jnp.max(s, axis=-1, keepdims=True), sink)
            e = jnp.exp(s - m)
            den = jnp.sum(e, axis=-1, keepdims=True) + jnp.exp(sink - m)
            heads[hq] = _mm(e / den, vv, hi)
    outs_ref[1] = jnp.concatenate(heads, axis=1).astype(outs_ref.dtype)
    kst_ref[...] = kr
    vst_ref[...] = v
    kprev[...] = kr
    vprev[...] = v


def _branch_prompt(zs, batch, seq, lw, hi=False, t_start=0):
    n = zs.shape[0]
    nt = n // (batch * TB)
    rc, rs1, rs2 = _rope_tables(jnp.arange(seq))
    full = lambda shape: pl.BlockSpec(shape, lambda b, t: (0,) * len(shape))
    tab = pl.BlockSpec((TB, LANES), lambda b, t: (t_start + t, 0))
    kv_w = N_KV * HEAD_DIM
    return pl.pallas_call(
        functools.partial(_branch_prompt_kernel, hi=hi, t_start=t_start),
        grid=(batch, nt),
        in_specs=[pl.BlockSpec(memory_space=pltpu.SMEM),
                  pl.BlockSpec((TB, G_OFF), lambda b, t: (b * nt + t, 0)),
                  tab, tab, tab,
                  full((CONV_W, BRANCH_W)), full((1, BRANCH_W)), full((1, BRANCH_W)),
                  full((SGU_GROUPS, CHUNK, CHUNK)), full((CHUNK, SGU_GROUPS)),
                  full((len(POOL_WINDOWS), LANES, LANES)), full((1, BRANCH_W))],
        out_specs=[pl.BlockSpec((N_BRANCH, TB, BRANCH_W), lambda b, t: (0, b * nt + t, 0)),
                   pl.BlockSpec((None, CONV_W - 1, BRANCH_W), lambda b, t: (b, 0, 0)),
                   pl.BlockSpec((None, WINDOW, kv_w), lambda b, t: (b, 0, 0)),
                   pl.BlockSpec((None, WINDOW, kv_w), lambda b, t: (b, 0, 0)),
                   pl.BlockSpec((None, POOL_MAX - 1, BRANCH_W), lambda b, t: (b, 0, 0))],
        out_shape=[jax.ShapeDtypeStruct((N_BRANCH, n, BRANCH_W), F32 if hi else BF16),
                   jax.ShapeDtypeStruct((batch, CONV_W - 1, BRANCH_W), F32),
                   jax.ShapeDtypeStruct((batch, WINDOW, kv_w), F32),
                   jax.ShapeDtypeStruct((batch, WINDOW, kv_w), F32),
                   jax.ShapeDtypeStruct((batch, POOL_MAX - 1, BRANCH_W), F32)],
        scratch_shapes=[pltpu.VMEM((TB, kv_w), F32), pltpu.VMEM((TB, kv_w), F32),
                        pltpu.VMEM((SUBLANES + TB, BRANCH_W), F32),
                        pltpu.VMEM((POOL_MAX + TB, BRANCH_W), F32)],
        compiler_params=_cparams(("arbitrary", "arbitrary")),
        name="branch_prompt",
    )(lw["attn_sinks"], zs, rc, rs1, rs2, lw["conv_w"], lw["sgu_ln_g"].reshape(1, -1),
      lw["sgu_ln_b"].reshape(1, -1), lw["sgu_w"], lw["sgu_b"].T, lw["pool_w"],
      lw["pool_scale"].reshape(1, -1))


def _branch_sample_kernel(sink_ref, z_ref, rc_ref, rs1_ref, rs2_ref, cw_ref, lng_ref, lnb_ref,
                          ssc_ref, ssh_ref, pw_ref, ps_ref, cpast_ref, kc_ref, vc_ref, ppast_ref,
                          outs_ref, conv_ref, kst_ref, vst_ref, pool_ref, chunkv_ref,
                          qrot, krot, vnew):
    nb = z_ref.shape[0]

    ci = z_ref[:, A_C_OFF:A_B_OFF] * z_ref[:, A_H_OFF:A_C_OFF]
    y = cw_ref[CONV_W - 1:CONV_W, :] * ci
    for j in range(CONV_W - 1):
        y = y + cw_ref[j:j + 1, :] * cpast_ref[j]
    outs_ref[0] = z_ref[:, A_B_OFF:Q_OFF] * y
    for j in range(CONV_W - 2):
        conv_ref[j] = cpast_ref[j + 1]
    conv_ref[CONV_W - 2] = ci

    p = z_ref[:, D_OFF:G_OFF]
    for g, w in enumerate(POOL_WINDOWS):
        sl = slice(g * LANES, (g + 1) * LANES)
        s = p[:, sl]
        for k in range(1, w):
            s = s + ppast_ref[POOL_MAX - 1 - k, :, sl]
        dlt = s / float(min(PAST_LEN + 1, w)) - p[:, sl]
        outs_ref[3, :, sl] = _dot3(dlt, pw_ref[g]) * ps_ref[:, sl]
    for j in range(POOL_MAX - 2):
        pool_ref[j] = ppast_ref[j + 1]
    pool_ref[POOL_MAX - 2] = p

    ge = _gelu(z_ref[:, C_OFF:D_OFF])
    vn = _layernorm(ge[:, BRANCH_W:], lng_ref[...], lnb_ref[...])
    outs_ref[2] = ge[:, :BRANCH_W] * (vn * ssc_ref[...] + ssh_ref[...])
    chunkv_ref[...] = vn

    rc, rs1, rs2 = rc_ref[...], rs1_ref[...], rs2_ref[...]
    krot[...] = _rope(z_ref[:, K_OFF:V_OFF], rc, rs1, rs2)
    vnew[...] = z_ref[:, V_OFF:C_OFF]
    for c in range(N_HEADS * HEAD_DIM // LANES):
        cs = slice(c * LANES, (c + 1) * LANES)
        qrot[:, cs] = _rope(z_ref[:, Q_OFF + c * LANES:Q_OFF + (c + 1) * LANES], rc, rs1, rs2)
    row = lax.broadcasted_iota(jnp.int32, (WINDOW, LANES), 0)
    lo = lax.broadcasted_iota(jnp.int32, (WINDOW, LANES), 1) < HEAD_DIM
    lo1 = lax.broadcasted_iota(jnp.int32, (1, LANES), 1) < HEAD_DIM
    scale = HEAD_DIM ** -0.5

    def body(b, carry):
        kn = jnp.where(row == WINDOW - 1, krot[pl.ds(b, 1), :], pltpu.roll(kc_ref[b], WINDOW - 1, 0))
        vn_b = jnp.where(row == WINDOW - 1, vnew[pl.ds(b, 1), :], pltpu.roll(vc_ref[b], WINDOW - 1, 0))
        kst_ref[b] = kn
        vst_ref[b] = vn_b
        qb = qrot[pl.ds(b, 1), :]
        o = []
        for g in range(GQA_G):
            ca = qb[:, (g // 2) * LANES:(g // 2 + 1) * LANES]
            cb = qb[:, (2 + g // 2) * LANES:(3 + g // 2) * LANES]
            if g % 2 == 0:
                qrow = jnp.where(lo1, ca, pltpu.roll(cb, HEAD_DIM, 1))
            else:
                qrow = jnp.where(lo1, pltpu.roll(ca, HEAD_DIM, 1), cb)
            prod = kn * qrow
            pn = []
            for part, hq in ((jnp.where(lo, prod, 0.0), g), (jnp.where(lo, 0.0, prod), GQA_G + g)):
                s = jnp.sum(part, axis=1, keepdims=True) * scale
                sink = sink_ref[hq]
                m = jnp.maximum(jnp.max(s, axis=0, keepdims=True), sink)
                e = jnp.exp(s - m)
                den = jnp.sum(e, axis=0, keepdims=True) + jnp.exp(sink - m)
                pn.append(e / den)
            o.append(jnp.sum(jnp.where(lo, pn[0], pn[1]) * vn_b, axis=0, keepdims=True))
        sw = lambda x: pltpu.roll(x, HEAD_DIM, 1)
        chunks = [jnp.where(lo1, o[0], sw(o[1])), jnp.where(lo1, o[2], sw(o[3])),
                  jnp.where(lo1, sw(o[0]), o[1]), jnp.where(lo1, sw(o[2]), o[3])]
        outs_ref[1, pl.ds(b, 1), :] = jnp.concatenate(chunks, axis=1)
        return carry

    lax.fori_loop(0, nb, body, 0)


def _branch_sample(zs, lw, cpast, kc, vc, ppast):
    nb = zs.shape[0]
    kv_w = N_KV * HEAD_DIM
    rc, rs1, rs2 = _rope_tables(jnp.full((1,), PAST_LEN, jnp.int32))
    ssc = jnp.repeat(lw["sgu_w"][:, 0, 0], CHUNK).reshape(1, BRANCH_W)
    ssh = jnp.repeat(lw["sgu_b"][:, 0], CHUNK).reshape(1, BRANCH_W)
    vspec = pl.BlockSpec(memory_space=pltpu.VMEM)
    return pl.pallas_call(
        _branch_sample_kernel,
        in_specs=[pl.BlockSpec(memory_space=pltpu.SMEM)] + [vspec] * 15,
        out_specs=[vspec] * 6,
        out_shape=[jax.ShapeDtypeStruct((N_BRANCH, nb, BRANCH_W), F32),
                   jax.ShapeDtypeStruct((CONV_W - 1, nb, BRANCH_W), F32),
                   jax.ShapeDtypeStruct((nb, WINDOW, kv_w), F32),
                   jax.ShapeDtypeStruct((nb, WINDOW, kv_w), F32),
                   jax.ShapeDtypeStruct((POOL_MAX - 1, nb, BRANCH_W), F32),
                   jax.ShapeDtypeStruct((nb, BRANCH_W), F32)],
        scratch_shapes=[pltpu.VMEM((nb, N_HEADS * HEAD_DIM), F32), pltpu.VMEM((nb, kv_w), F32),
                        pltpu.VMEM((nb, kv_w), F32)],
        compiler_params=pltpu.CompilerParams(vmem_limit_bytes=VMEM_LIMIT),
        name="branch_sample",
    )(lw["attn_sinks"], zs, rc, rs1, rs2, lw["conv_w"], lw["sgu_ln_g"].reshape(1, -1),
      lw["sgu_ln_b"].reshape(1, -1), ssc, ssh, lw["pool_w"], lw["pool_scale"].reshape(1, -1),
      cpast, kc, vc, ppast)


def _merge_kernel(xn_ref, outs_ref, wg0_ref, wg1_ref, wb_ref, o_ref, acc_ref, *, hi):
    nbr = pl.program_id(2)

    @pl.when(nbr == 0)
    def _():
        acc_ref[...] = jnp.zeros_like(acc_ref)

    xn, outs = xn_ref[...], outs_ref[...]
    for h, wg_ref in enumerate((wg0_ref, wg1_ref)):
        cols = slice(h * TN_MERGE, (h + 1) * TN_MERGE)
        gate = jax.nn.sigmoid(_mm(xn, wg_ref[...], hi))
        acc_ref[:, cols] += gate * _mm(outs, wb_ref[:, cols], hi)

    @pl.when(nbr == N_BRANCH - 1)
    def _():
        o_ref[...] = acc_ref[...].astype(o_ref.dtype)


def _merge(xn, outs, w_in, w_branch, layer, hi):
    n, d = xn.shape
    tm = min(TM_MERGE, n)
    g_blk = G_OFF // TN_MERGE
    per_branch = D_MODEL // TN_MERGE
    wide = 2 * TN_MERGE
    gate_block = lambda h: pl.BlockSpec(
        (None, d, TN_MERGE), lambda i, j, b: (layer, 0, g_blk + b * per_branch + 2 * j + h))
    return pl.pallas_call(
        functools.partial(_merge_kernel, hi=hi),
        grid=(n // tm, D_MODEL // wide, N_BRANCH),
        in_specs=[pl.BlockSpec((tm, d), lambda i, j, b: (i, 0)),
                  pl.BlockSpec((None, tm, BRANCH_W), lambda i, j, b: (b, i, 0)),
                  gate_block(0), gate_block(1),
                  pl.BlockSpec((None, None, BRANCH_W, wide), lambda i, j, b: (layer, b, 0, j))],
        out_specs=pl.BlockSpec((tm, wide), lambda i, j, b: (i, j)),
        out_shape=jax.ShapeDtypeStruct((n, D_MODEL), F32 if hi else BF16),
        scratch_shapes=[pltpu.VMEM((tm, wide), F32)],
        compiler_params=_cparams(("parallel", "arbitrary", "arbitrary")),
        name="merge",
    )(xn, outs, w_in, w_in, w_branch)


def _wo_kernel(m_ref, w_ref, x_ref, o_ref, *, hi):
    o_ref[...] = x_ref[...] + _mm(m_ref[...], w_ref[...], hi)


def _out_proj(merged, w_o, x, layer, hi):
    n, d = x.shape
    tm = min(TM_WO, n)
    return pl.pallas_call(
        functools.partial(_wo_kernel, hi=hi),
        grid=(n // tm, d // TN_WO),
        in_specs=[pl.BlockSpec((tm, d), lambda i, j: (i, 0)),
                  pl.BlockSpec((None, d, TN_WO), lambda i, j: (layer, 0, j)),
                  pl.BlockSpec((tm, TN_WO), lambda i, j: (i, j))],
        out_specs=pl.BlockSpec((tm, TN_WO), lambda i, j: (i, j)),
        out_shape=jax.ShapeDtypeStruct((n, d), F32),
        compiler_params=_cparams(("parallel", "arbitrary")),
        name="out_proj",
    )(merged, w_o, x)


def _put_rows_kernel(y_hbm, t_ref, o_ref):
    del y_hbm
    o_ref[...] = t_ref[...]


def _put_last_blocks(y, y_tail, batch, seq):
    d = y.shape[1]
    nt = seq // TB
    return pl.pallas_call(
        _put_rows_kernel,
        grid=(batch,),
        in_specs=[pl.BlockSpec(memory_space=pl.ANY),
                  pl.BlockSpec((TB, d), lambda b: (b, 0))],
        out_specs=pl.BlockSpec((TB, d), lambda b: (b * nt + nt - 1, 0)),
        out_shape=jax.ShapeDtypeStruct(y.shape, y.dtype),
        input_output_aliases={0: 0},
        compiler_params=_cparams(("arbitrary",)),
        name="put_last_blocks",
    )(y, y_tail)


def _route_rows(y, g_ref, wr_ref, br_ref, carry_ref):
    tm = y.shape[0]
    xn = _rms(y, g_ref[...])
    logits = _dot3(xn, wr_ref[...]) + br_ref[...]
    lane_i = lax.broadcasted_iota(jnp.int32, (tm, ROUTER_LANES), 1)
    lane = lane_i.astype(F32)
    lane_grp = lax.shift_right_logical(lane_i, 3).astype(F32)
    big = float(ROUTER_LANES)
    is_grp = (lane_i >= N_EXPERTS) & (lane_i < N_EXPERTS + N_EXP_GROUPS)
    lg = jnp.where(is_grp, logits, NEG)
    gmax = jnp.max(lg, axis=-1, keepdims=True)
    gsel = jnp.min(jnp.where(lg == gmax, lane - N_EXPERTS, big), axis=-1, keepdims=True)
    wg = 1.0 / jnp.sum(jnp.where(is_grp, jnp.exp(logits - gmax), 0.0), axis=-1, keepdims=True)
    in_grp = (lane_i < N_EXPERTS) & (lane_grp == gsel)
    v1 = jnp.where(in_grp, logits, NEG)
    m1 = jnp.max(v1, axis=-1, keepdims=True)
    i1 = jnp.min(jnp.where(v1 == m1, lane, big), axis=-1, keepdims=True)
    v2 = jnp.where(lane == i1, NEG, v1)
    m2 = jnp.max(v2, axis=-1, keepdims=True)
    i2 = jnp.min(jnp.where(v2 == m2, lane, big), axis=-1, keepdims=True)
    e2 = jnp.exp(m2 - m1)
    w1 = wg / (1.0 + e2)
    w2 = wg * e2 / (1.0 + e2)

    sel1 = lane == i1
    sel2 = lane == i2
    onehot = jnp.where(sel1 | sel2, 1.0, 0.0)
    r_i = lax.broadcasted_iota(jnp.int32, (tm, tm), 0)
    c_i = lax.broadcasted_iota(jnp.int32, (tm, tm), 1)
    before = jnp.where(c_i < r_i, 1.0, 0.0).astype(BF16)
    counts = _dot(before, onehot.astype(BF16)) + carry_ref[0:1, :]
    r1 = jnp.sum(jnp.where(sel1, counts, 0.0), axis=-1, keepdims=True)
    r2 = jnp.sum(jnp.where(sel2, counts, 0.0), axis=-1, keepdims=True)
    carry_ref[...] = carry_ref[...] + jnp.sum(onehot, axis=0, keepdims=True)

    info = jnp.zeros((tm, ROUTER_LANES), F32)
    for k, val in enumerate((i1, i2, r1, r2, w1, w2)):
        info = jnp.where(lane_i == k, val, info)
    return xn, info


def _router_kernel(yp_ref, ys_ref, g_ref, wr_ref, br_ref, xn_ref, info_ref, cnt_ref, zero_hbm,
                   carry_ref, zbuf, zsem, *, prompt_tiles, sorted_tiles, zero_per_step):
    i = pl.program_id(0)
    tile_lines = TM_MOE * ROW_CHUNKS

    @pl.when(i == 0)
    def _():
        carry_ref[...] = jnp.zeros_like(carry_ref)
        zbuf[...] = jnp.zeros_like(zbuf)

    def zero_tile(j):
        k = i * zero_per_step + j
        dst = zero_hbm.at[pl.ds(pl.multiple_of(k * tile_lines, tile_lines), tile_lines)]
        return k, pltpu.make_async_copy(zbuf, dst, zsem)

    for j in range(zero_per_step):
        k, cp = zero_tile(j)

        @pl.when(k < sorted_tiles)
        def _():
            cp.start()

    @pl.when(i < prompt_tiles)
    def _():
        xn, info = _route_rows(yp_ref[...], g_ref, wr_ref, br_ref, carry_ref)
        _store_row_tiles(xn_ref, xn)
        info_ref[...] = info

    @pl.when(i == prompt_tiles)
    def _():
        nb = ys_ref.shape[0]
        xn, info = _route_rows(ys_ref[...], g_ref, wr_ref, br_ref, carry_ref)
        _store_row_tiles(xn_ref, xn)
        info_ref[0:nb, :] = info

    cnt_ref[...] = carry_ref[...]

    for j in range(zero_per_step):
        k, cp = zero_tile(j)

        @pl.when(k < sorted_tiles)
        def _():
            cp.wait()


def _router(yp, ys, g, w_router, b_router, sorted_tiles):
    n_p, d = yp.shape
    nb = ys.shape[0]
    tm = min(TM_NORM, n_p)
    prompt_tiles = n_p // tm
    n = n_p + nb
    tile_lines = TM_MOE * ROW_CHUNKS
    return pl.pallas_call(
        functools.partial(_router_kernel, prompt_tiles=prompt_tiles, sorted_tiles=sorted_tiles,
                          zero_per_step=pl.cdiv(sorted_tiles, prompt_tiles + 1)),
        grid=(prompt_tiles + 1,),
        in_specs=[pl.BlockSpec((tm, d), lambda i: (jnp.minimum(i, prompt_tiles - 1), 0)),
                  pl.BlockSpec((nb, d), lambda i: (0, 0)),
                  pl.BlockSpec((1, d), lambda i: (0, 0)),
                  pl.BlockSpec((d, ROUTER_LANES), lambda i: (0, 0)),
                  pl.BlockSpec((1, ROUTER_LANES), lambda i: (0, 0))],
        out_specs=[pl.BlockSpec((tm * ROW_CHUNKS, LANES), lambda i: (i, 0)),
                   pl.BlockSpec((tm, ROUTER_LANES), lambda i: (i, 0)),
                   pl.BlockSpec((SUBLANES, ROUTER_LANES), lambda i: (0, 0)),
                   pl.BlockSpec(memory_space=pl.ANY)],
        out_shape=[jax.ShapeDtypeStruct((n * ROW_CHUNKS, LANES), F32),
                   jax.ShapeDtypeStruct((n, ROUTER_LANES), F32),
                   jax.ShapeDtypeStruct((SUBLANES, ROUTER_LANES), F32),
                   jax.ShapeDtypeStruct((sorted_tiles * tile_lines, LANES), F32)],
        scratch_shapes=[pltpu.VMEM((SUBLANES, ROUTER_LANES), F32),
                        pltpu.VMEM((tile_lines, LANES), F32), pltpu.SemaphoreType.DMA(())],
        compiler_params=_cparams(("arbitrary",)),
        name="router",
    )(yp, ys, g.reshape(1, d), w_router, b_router)


ROW_CHUNKS = D_MODEL // LANES


def _store_row_tiles(ref, x):
    for c in range(ROW_CHUNKS):
        ref[pl.ds(c, x.shape[0], stride=ROW_CHUNKS), :] = x[:, c * LANES:(c + 1) * LANES]


def _load_row_tiles(ref, rows):
    return jnp.concatenate(
        [ref[pl.ds(c, rows, stride=ROW_CHUNKS), :] for c in range(ROW_CHUNKS)], axis=1)


def _row_copy(src, i, dst, j, sem):
    def lines(k):
        start = k * ROW_CHUNKS
        if not isinstance(k, int):
            start = pl.multiple_of(start, ROW_CHUNKS)
        return pl.ds(start, ROW_CHUNKS)
    return pltpu.make_async_copy(src.at[lines(i)], dst.at[lines(j)], sem)


def _dispatch_kernel(pos_ref, x_ref, xs_in, xs_out, sem, *, rows, n_tokens):
    del xs_in
    i = pl.program_id(0)
    base = i * rows
    full_steps, tail = divmod(n_tokens, rows)

    def copies(r):
        t = base + r
        return (_row_copy(x_ref, r, xs_out, pos_ref[2 * t], sem),
                _row_copy(x_ref, r, xs_out, pos_ref[2 * t + 1], sem))

    def scatter(count):
        def start(r, c):
            for cp in copies(r):
                cp.start()
            return c
        lax.fori_loop(0, count, start, 0, unroll=8)
        for _ in range(2 * count):
            _row_copy(x_ref, 0, xs_out, 0, sem).wait()

    @pl.when(i < full_steps)
    def _():
        scatter(rows)

    if tail:
        @pl.when(i == full_steps)
        def _():
            scatter(tail)


def _dispatch(pos, xn, xs_zero):
    n_tokens = xn.shape[0] // ROW_CHUNKS
    rows = TM_NORM
    any_spec = pl.BlockSpec(memory_space=pl.ANY)
    return pl.pallas_call(
        functools.partial(_dispatch_kernel, rows=rows, n_tokens=n_tokens),
        grid_spec=pltpu.PrefetchScalarGridSpec(
            num_scalar_prefetch=1, grid=(pl.cdiv(n_tokens, rows),),
            in_specs=[pl.BlockSpec((rows * ROW_CHUNKS, LANES), lambda i, pos: (i, 0)), any_spec],
            out_specs=any_spec,
            scratch_shapes=[pltpu.SemaphoreType.DMA(())]),
        out_shape=jax.ShapeDtypeStruct(xs_zero.shape, xs_zero.dtype),
        input_output_aliases={2: 0},
        compiler_params=_cparams(("arbitrary",)),
        name="dispatch",
    )(pos, xn, xs_zero)


def _moe_kernel(te_ref, na_ref, first_ref, ord_ref, next_ref, x_ref, wg_hbm, wu_hbm, wd_hbm,
                ys_ref, wg_buf, wu_buf, wd_buf, wsem, *, layer):
    g = pl.program_id(0)
    n_act = na_ref[0]

    def weight_copies(e, slot):
        return [pltpu.make_async_copy(hbm.at[layer, e], vm.at[slot], wsem.at[slot])
                for hbm, vm in ((wg_hbm, wg_buf), (wu_hbm, wu_buf), (wd_hbm, wd_buf))]

    @pl.when(g == 0)
    def _():
        for cp in weight_copies(te_ref[0], 0):
            cp.start()

    @pl.when(g < n_act)
    def _():
        wslot = ord_ref[g] % 2

        @pl.when(first_ref[g] == 1)
        def _():
            @pl.when(next_ref[g] >= 0)
            def _():
                for cp in weight_copies(next_ref[g], 1 - wslot):
                    cp.start()
            for cp in weight_copies(te_ref[g], wslot):
                cp.wait()

        x = _load_row_tiles(x_ref, TM_MOE).astype(BF16)
        a = _dot(x, wg_buf[wslot].astype(BF16))
        u = _dot(x, wu_buf[wslot].astype(BF16))
        h = (a * jax.nn.sigmoid(a)) * u
        _store_row_tiles(ys_ref, _dot(h.astype(BF16), wd_buf[wslot].astype(BF16)))

    @pl.when(g >= n_act)
    def _():
        ys_ref[...] = jnp.zeros_like(ys_ref)


def _moe(plan, xs, w_gate, w_up, w_down, layer):
    d = D_MODEL
    n_tiles = xs.shape[0] // (TM_MOE * ROW_CHUNKS)
    any_spec = pl.BlockSpec(memory_space=pl.ANY)
    tile = (TM_MOE * ROW_CHUNKS, LANES)
    return pl.pallas_call(
        functools.partial(_moe_kernel, layer=layer),
        grid_spec=pltpu.PrefetchScalarGridSpec(
            num_scalar_prefetch=5, grid=(n_tiles,),
            in_specs=[pl.BlockSpec(tile, lambda g, te, na, *_: (jnp.minimum(g, na[0] - 1), 0)),
                      any_spec, any_spec, any_spec],
            out_specs=pl.BlockSpec(tile, lambda g, *_: (g, 0)),
            scratch_shapes=[pltpu.VMEM((2, d, EXPERT_FF), F32), pltpu.VMEM((2, d, EXPERT_FF), F32),
                            pltpu.VMEM((2, EXPERT_FF, d), F32), pltpu.SemaphoreType.DMA((2,))]),
        out_shape=jax.ShapeDtypeStruct(xs.shape, F32),
        compiler_params=_cparams(("arbitrary",)),
        name="moe_ffn",
    )(plan["tile_expert"], plan["n_active"], plan["first"], plan["ord"], plan["next"],
      xs, w_gate, w_up, w_down)


def _moe_sample_kernel(ue_ref, nu_ref, x_ref, info_ref, y_ref, g_ref, wg_ref, wu_ref, wd_ref,
                       xo_ref, xno_ref, acc_ref):
    s = pl.program_id(0)

    @pl.when(s == 0)
    def _():
        acc_ref[...] = y_ref[...]

    @pl.when(s < nu_ref[0])
    def _():
        x = _load_row_tiles(x_ref, y_ref.shape[0])
        a = _dot3(x, wg_ref[...])
        u = _dot3(x, wu_ref[...])
        out = _dot3((a * jax.nn.sigmoid(a)) * u, wd_ref[...])
        e = ue_ref[s].astype(F32)
        w = (jnp.where(info_ref[:, 0:1] == e, info_ref[:, 4:5], 0.0)
             + jnp.where(info_ref[:, 1:2] == e, info_ref[:, 5:6], 0.0))
        acc_ref[...] += w * out

    @pl.when(s == pl.num_programs(0) - 1)
    def _():
        xo_ref[...] = acc_ref[...]
        xno_ref[...] = _rms(acc_ref[...], g_ref[...])


def _moe_sample(xn2, info, y, g, w_gate, w_up, w_down, layer, row0):
    nb, d = y.shape
    blk0 = row0 // nb
    e_s = info[row0:, 0:2].astype(jnp.int32).reshape(-1)
    used = jnp.zeros((N_EXPERTS,), jnp.int32).at[e_s].set(1)
    n_used = jnp.sum(used)
    order = jnp.argsort(1 - used, stable=True).astype(jnp.int32)
    used_list = jnp.where(jnp.arange(N_EXPERTS) < n_used, order, order[n_used - 1])
    rows = lambda s, ue, nu: (blk0, 0)
    fixed = lambda s, ue, nu: (0, 0)
    wsel = lambda s, ue, nu: (layer, ue[s], 0, 0)
    return pl.pallas_call(
        _moe_sample_kernel,
        grid_spec=pltpu.PrefetchScalarGridSpec(
            num_scalar_prefetch=2, grid=(N_EXPERTS,),
            in_specs=[pl.BlockSpec((nb * ROW_CHUNKS, LANES), rows),
                      pl.BlockSpec((nb, ROUTER_LANES), rows),
                      pl.BlockSpec((nb, d), fixed),
                      pl.BlockSpec((1, d), fixed),
                      pl.BlockSpec((None, None, d, EXPERT_FF), wsel),
                      pl.BlockSpec((None, None, d, EXPERT_FF), wsel),
                      pl.BlockSpec((None, None, EXPERT_FF, d), wsel)],
            out_specs=[pl.BlockSpec((nb, d), fixed), pl.BlockSpec((nb, d), fixed)],
            scratch_shapes=[pltpu.VMEM((nb, d), F32)]),
        out_shape=[jax.ShapeDtypeStruct((nb, d), F32), jax.ShapeDtypeStruct((nb, d), F32)],
        compiler_params=_cparams(("arbitrary",)),
        name="moe_sample",
    )(used_list, n_used.reshape(1).astype(jnp.int32), xn2, info, y, g.reshape(1, d),
      w_gate, w_up, w_down)


def _combine_kernel(pos_ref, y_ref, info_ref, g_ref, ys_hbm, *rest, rows, write_x, blk0):
    if write_x:
        xo_ref, xno_ref, buf0, buf1, sem = rest
    else:
        xno_ref, buf0, buf1, sem = rest
    i = pl.program_id(0)

    def fetch(step, slot):
        base = (blk0 + step) * rows

        def body(r, c):
            t = base + r
            _row_copy(ys_hbm, pos_ref[2 * t], buf0.at[slot], r, sem.at[slot]).start()
            _row_copy(ys_hbm, pos_ref[2 * t + 1], buf1.at[slot], r, sem.at[slot]).start()
            return c
        lax.fori_loop(0, rows, body, 0, unroll=8)

    @pl.when(i == 0)
    def _():
        fetch(0, 0)

    @pl.when(i + 1 < pl.num_programs(0))
    def _():
        fetch(i + 1, (i + 1) % 2)

    slot = i % 2
    for r in range(rows):
        _row_copy(ys_hbm, 0, buf0.at[slot], r, sem.at[slot]).wait()
        _row_copy(ys_hbm, 0, buf1.at[slot], r, sem.at[slot]).wait()
    xnew = (y_ref[...] + info_ref[:, 4:5] * _load_row_tiles(buf0.at[slot], rows)
            + info_ref[:, 5:6] * _load_row_tiles(buf1.at[slot], rows))
    if write_x:
        xo_ref[...] = xnew
    xno_ref[...] = _rms(xnew, g_ref[...]).astype(xno_ref.dtype)


def _combine(pos, y, info, g, ys, norm_dtype, write_x, row0):
    n, d = y.shape
    rows = min(TM_COMBINE, n)
    assert row0 % rows == 0
    blk0 = row0 // rows
    tile = pl.BlockSpec((rows, d), lambda i, pos: (i, 0))
    out_specs = [tile]
    out_shape = [jax.ShapeDtypeStruct((n, d), norm_dtype)]
    if write_x:
        out_specs = [tile] + out_specs
        out_shape = [jax.ShapeDtypeStruct((n, d), F32)] + out_shape
    return pl.pallas_call(
        functools.partial(_combine_kernel, rows=rows, write_x=write_x, blk0=blk0),
        grid_spec=pltpu.PrefetchScalarGridSpec(
            num_scalar_prefetch=1, grid=(n // rows,),
            in_specs=[tile,
                      pl.BlockSpec((rows, ROUTER_LANES), lambda i, pos: (blk0 + i, 0)),
                      pl.BlockSpec((1, d), lambda i, pos: (0, 0)),
                      pl.BlockSpec(memory_space=pl.ANY)],
            out_specs=out_specs,
            scratch_shapes=[pltpu.VMEM((2, rows * ROW_CHUNKS, LANES), F32),
                            pltpu.VMEM((2, rows * ROW_CHUNKS, LANES), F32),
                            pltpu.SemaphoreType.DMA((2,))]),
        out_shape=out_shape,
        compiler_params=_cparams(("arbitrary",)),
        name="combine",
    )(pos, y, info, g.reshape(1, d), ys)


def _plan(info, cnt, n_tiles):
    count = cnt[0, :N_EXPERTS].astype(jnp.int32)
    tiles = (count + TM_MOE - 1) // TM_MOE
    tile_end = jnp.cumsum(tiles)
    row_off = (tile_end - tiles) * TM_MOE
    n_active = tile_end[-1]
    pos = (row_off[info[:, 0:2].astype(jnp.int32)] + info[:, 2:4].astype(jnp.int32)).reshape(-1)
    step = jnp.arange(n_tiles, dtype=jnp.int32)
    g = jnp.minimum(step, n_active - 1)
    tile_expert = jnp.sum((g[:, None] >= tile_end[None, :]).astype(jnp.int32), axis=1)
    prev = jnp.concatenate([jnp.full((1,), -1, jnp.int32), tile_expert[:-1]])
    first = ((step < n_active) & (tile_expert != prev)).astype(jnp.int32)
    ordinal = jnp.cumsum(first) - 1
    ids = jnp.arange(N_EXPERTS, dtype=jnp.int32)
    used_at_or_after = lax.cummin(jnp.where(tiles > 0, ids, N_EXPERTS), reverse=True)
    used_after = jnp.concatenate([used_at_or_after[1:], jnp.full((1,), N_EXPERTS, jnp.int32)])
    next_used = jnp.where(used_after == N_EXPERTS, -1, used_after)
    i32 = lambda a: a.astype(jnp.int32)
    return pos, dict(tile_expert=i32(tile_expert), n_active=i32(n_active.reshape(1)),
                     first=first, ord=i32(ordinal), next=i32(next_used[tile_expert]))


def kernel(x_prompt, x_sample, state_conv, cache_win_k, cache_win_v, state_pool, norm1_g, w_in,
           conv_w, attn_sinks, sgu_ln_g, sgu_ln_b, sgu_w, sgu_b, pool_w, pool_scale, w_branch, w_o,
           norm2_g, router_group_w, router_group_b, router_expert_w, router_expert_b, moe_w_gate,
           moe_w_up, moe_w_down, final_norm_g):
    batch, seq, d = x_prompt.shape
    assert seq % TB == 0 and seq >= 2 * TB and d == D_MODEL
    nb = x_sample.shape[0]
    depth = w_in.shape[0]
    n_p = batch * seq
    kv_w = N_KV * HEAD_DIM
    n_slots = 2 * (n_p + nb)
    n_tiles = (n_slots + N_EXPERTS * (TM_MOE - 1) + TM_MOE - 1) // TM_MOE

    xp = x_prompt.reshape(n_p, d)
    xs_ = x_sample.reshape(nb, d)
    xn_p = _rmsnorm(xp, norm1_g[0], BF16)
    xn_s = _rmsnorm(xs_, norm1_g[0], F32)

    pad = ROUTER_LANES - N_EXPERTS - N_EXP_GROUPS
    conv_p, k_p, v_p, pool_p = [], [], [], []
    conv_s, k_s, v_s, pool_s, chunk_s = [], [], [], [], []
    y_prompt = y_sample = None
    for l in range(depth):
        lw = dict(attn_sinks=attn_sinks[l], conv_w=conv_w[l], sgu_ln_g=sgu_ln_g[l],
                  sgu_ln_b=sgu_ln_b[l], sgu_w=sgu_w[l], sgu_b=sgu_b[l], pool_w=pool_w[l],
                  pool_scale=pool_scale[l])
        w_router = jnp.concatenate(
            [router_expert_w[l], router_group_w[l], jnp.zeros((d, pad), F32)], axis=1)
        b_router = jnp.concatenate(
            [router_expert_b[l], router_group_b[l], jnp.zeros((pad,), F32)]).reshape(1, -1)
        last = l == depth - 1
        g_next = final_norm_g if last else norm1_g[l + 1]

        zs_p = _in_proj(xn_p, w_in, l, False)
        outs_p, c_st, k_st, v_st, p_st = _branch_prompt(zs_p, batch, seq, lw)
        conv_p.append(c_st)
        k_p.append(k_st.reshape(batch, WINDOW, N_KV, HEAD_DIM))
        v_p.append(v_st.reshape(batch, WINDOW, N_KV, HEAD_DIM))
        pool_p.append(p_st)
        merged_p = _merge(xn_p, outs_p, w_in, w_branch, l, False)
        yp = _out_proj(merged_p, w_o, xp, l, False)
        if not last:
            x_t = xp.reshape(batch, seq, d)[:, seq - 2 * TB:].reshape(batch * 2 * TB, d)
            xn_t = _rmsnorm(x_t, norm1_g[l], F32)
            zs_t = _in_proj(xn_t, w_in, l, True)
            outs_t = _branch_prompt(zs_t, batch, seq, lw, hi=True, t_start=seq // TB - 2)[0]
            second = lambda a: a.reshape(batch, 2, TB, a.shape[-1])[:, 1].reshape(batch * TB, -1)
            outs_t = jnp.stack([second(outs_t[b]) for b in range(N_BRANCH)])
            merged_t = _merge(second(xn_t), outs_t, w_in, w_branch, l, True)
            y_t = _out_proj(merged_t, w_o, second(x_t), l, True)
            yp = _put_last_blocks(yp, y_t, batch, seq)

        zs_s = _in_proj(xn_s, w_in, l, True)
        outs_s, c_st, k_st, v_st, p_st, cv = _branch_sample(
            zs_s, lw, jnp.swapaxes(state_conv[l], 0, 1),
            cache_win_k[l].reshape(nb, WINDOW, kv_w), cache_win_v[l].reshape(nb, WINDOW, kv_w),
            jnp.swapaxes(state_pool[l], 0, 1))
        conv_s.append(jnp.swapaxes(c_st, 0, 1))
        k_s.append(k_st.reshape(nb, WINDOW, N_KV, HEAD_DIM))
        v_s.append(v_st.reshape(nb, WINDOW, N_KV, HEAD_DIM))
        pool_s.append(jnp.swapaxes(p_st, 0, 1))
        chunk_s.append(cv.reshape(nb, 1, BRANCH_W))
        merged_s = _merge(xn_s, outs_s, w_in, w_branch, l, True)
        ys_ = _out_proj(merged_s, w_o, xs_, l, True)

        xn2, info, cnt, rows_sorted = _router(yp, ys_, norm2_g[l], w_router, b_router, n_tiles)
        pos, plan = _plan(info, cnt, n_tiles)
        rows_sorted = _dispatch(pos, xn2, rows_sorted)
        ffn = _moe(plan, rows_sorted, moe_w_gate, moe_w_up, moe_w_down, l)
        if last:
            (y_prompt,) = _combine(pos, yp, info, g_next, ffn, F32, False, 0)
            (y_sample,) = _combine(pos, ys_, info, g_next, ffn, F32, False, n_p)
        else:
            xp, xn_p = _combine(pos, yp, info, g_next, ffn, BF16, True, 0)
            xs_, xn_s = _moe_sample(xn2, info, ys_, g_next, moe_w_gate, moe_w_up, moe_w_down, l, n_p)

    return (y_prompt.reshape(batch, seq, d), y_sample.reshape(nb, 1, d),
            jnp.stack(conv_p), jnp.stack(k_p), jnp.stack(v_p), jnp.stack(pool_p),
            jnp.stack(conv_s), jnp.stack(k_s), jnp.stack(v_s), jnp.stack(pool_s),
            jnp.stack(chunk_s))
```

```python
import functools

import jax
import jax.numpy as jnp
from jax import lax
from jax.experimental import pallas as pl
from jax.experimental.pallas import tpu as pltpu

F32 = jnp.float32
BF16 = jnp.bfloat16

D_MODEL = 2048
BRANCH_W = 512
N_BRANCH = 4
CONV_W = 3
HEAD_DIM = 64
N_HEADS = 8
N_KV = 2
GQA_G = 4
WINDOW = 128
ROPE_THETA = 500000.0
ROPE_DIM = 16
CHUNK = 128
SGU_GROUPS = 4
POOL_WINDOWS = (2, 4, 8, 16)
POOL_MAX = 16
N_EXP_GROUPS = 4
EXP_PER_GROUP = 8
N_EXPERTS = 32
EXPERT_FF = 512
EPS = 1e-6
PAST_LEN = 16384
NEG = -3.0e38

A_H_OFF = 0
A_C_OFF = 512
A_B_OFF = 1024
Q_OFF = 1536
K_OFF = 2048
V_OFF = 2176
C_OFF = 2304
D_OFF = 3328
G_OFF = 3840

LANES = 128
SUBLANES = 8
VMEM_LIMIT = 52 * 1024 * 1024

TM_NORM = 512
TM_PROJ = 1024
TN_PROJ = 768
TM_MERGE = 2048
TN_MERGE = 256
TM_WO = 2048
TN_WO = 512
TM_MOE = 288
TM_COMBINE = 256
TB = 128
TAIL_EXACT = 16
ROUTER_LANES = 128


def _cparams(sem):
    return pltpu.CompilerParams(dimension_semantics=sem, vmem_limit_bytes=VMEM_LIMIT)


_NN = (((1,), (0,)), ((), ()))
_NT = (((1,), (1,)), ((), ()))


def _dot(a, b, dims=_NN):
    return lax.dot_general(a, b, dims, preferred_element_type=F32)


def _split(x):
    hi = x.astype(BF16)
    lo = (x - hi.astype(F32)).astype(BF16)
    return hi, lo


def _dot3(a, b, dims=_NN):
    ah, al = _split(a)
    bh, bl = _split(b)
    return _dot(ah, bh, dims) + _dot(al, bh, dims) + _dot(ah, bl, dims)


def _mm(a, w, hi, dims=_NN):
    if hi:
        return _dot3(a.astype(F32), w.astype(F32), dims)
    return _dot(a.astype(BF16), w.astype(BF16), dims)


def _rms(x, g):
    return (x * lax.rsqrt(jnp.mean(x * x, axis=-1, keepdims=True) + EPS)) * g


def _rmsnorm_kernel(x_ref, g_ref, o_ref):
    o_ref[...] = _rms(x_ref[...], g_ref[...]).astype(o_ref.dtype)


def _rmsnorm(x, g, out_dtype):
    n, d = x.shape
    tm = min(TM_NORM, n)
    return pl.pallas_call(
        _rmsnorm_kernel,
        grid=(n // tm,),
        in_specs=[pl.BlockSpec((tm, d), lambda i: (i, 0)),
                  pl.BlockSpec((1, d), lambda i: (0, 0))],
        out_specs=pl.BlockSpec((tm, d), lambda i: (i, 0)),
        out_shape=jax.ShapeDtypeStruct((n, d), out_dtype),
        compiler_params=_cparams(("parallel",)),
        name="rmsnorm",
    )(x, g.reshape(1, d))


def _proj_kernel(x_ref, w_ref, o_ref, *, hi):
    o_ref[...] = _mm(x_ref[...], w_ref[...], hi)


def _in_proj(xn, w_in, layer, hi):
    n, d = xn.shape
    tm = min(TM_PROJ // 2 if hi else TM_PROJ, n)
    return pl.pallas_call(
        functools.partial(_proj_kernel, hi=hi),
        grid=(n // tm, G_OFF // TN_PROJ),
        in_specs=[pl.BlockSpec((tm, d), lambda i, j: (i, 0)),
                  pl.BlockSpec((None, d, TN_PROJ), lambda i, j: (layer, 0, j))],
        out_specs=pl.BlockSpec((tm, TN_PROJ), lambda i, j: (i, j)),
        out_shape=jax.ShapeDtypeStruct((n, G_OFF), F32),
        compiler_params=_cparams(("parallel", "arbitrary")),
        name="in_proj",
    )(xn, w_in)


def _rope_tables(pos):
    half = ROPE_DIM // 2
    inv = jnp.power(jnp.float32(ROPE_THETA), -jnp.arange(half, dtype=F32) * (2.0 / ROPE_DIM))
    ang = pos.astype(F32)[:, None] * inv[None, :]
    cos, sin = jnp.cos(ang), jnp.sin(ang)
    t = pos.shape[0]
    rest = HEAD_DIM - ROPE_DIM
    c = jnp.concatenate([cos, cos, jnp.ones((t, rest), F32)], axis=1)
    s1 = jnp.concatenate([-sin, jnp.zeros((t, HEAD_DIM - half), F32)], axis=1)
    s2 = jnp.concatenate([jnp.zeros((t, half), F32), sin, jnp.zeros((t, rest), F32)], axis=1)
    rep = LANES // HEAD_DIM
    return jnp.tile(c, (1, rep)), jnp.tile(s1, (1, rep)), jnp.tile(s2, (1, rep))


def _rope(x, c, s1, s2):
    half = ROPE_DIM // 2
    return x * c + pltpu.roll(x, LANES - half, 1) * s1 + pltpu.roll(x, half, 1) * s2


def _gelu(x):
    return 0.5 * x * (1.0 + lax.erf(x * 0.7071067811865476))


def _layernorm(v, g, b):
    mu = jnp.mean(v, axis=-1, keepdims=True)
    vc = v - mu
    return vc * lax.rsqrt(jnp.mean(vc * vc, axis=-1, keepdims=True) + EPS) * g + b


def _branch_prompt_kernel(sink_ref, z_ref, rc_ref, rs1_ref, rs2_ref, cw_ref, lng_ref, lnb_ref,
                          sw_ref, sbt_ref, pw_ref, ps_ref,
                          outs_ref, conv_ref, kst_ref, vst_ref, pool_ref,
                          kprev, vprev, ci_ext, p_ext, *, hi, t_start):
    tb = pl.program_id(1)
    halo_c = SUBLANES
    halo_p = POOL_MAX

    @pl.when(tb == 0)
    def _():
        kprev[...] = jnp.zeros_like(kprev)
        vprev[...] = jnp.zeros_like(vprev)
        ci_ext[0:halo_c, :] = jnp.zeros((halo_c, BRANCH_W), F32)
        p_ext[0:halo_p, :] = jnp.zeros((halo_p, BRANCH_W), F32)

    ci = z_ref[:, A_C_OFF:A_B_OFF] * z_ref[:, A_H_OFF:A_C_OFF]
    ci_ext[halo_c:halo_c + TB, :] = ci
    y = cw_ref[2:3, :] * ci
    for j in range(CONV_W - 1):
        y = y + cw_ref[j:j + 1, :] * ci_ext[pl.ds(halo_c - (CONV_W - 1) + j, TB), :]
    outs_ref[0] = (z_ref[:, A_B_OFF:Q_OFF] * y).astype(outs_ref.dtype)
    conv_ref[...] = ci_ext[pl.ds(halo_c + TB - (CONV_W - 1), CONV_W - 1), :]
    ci_ext[0:halo_c, :] = ci[TB - halo_c:, :]

    p = z_ref[:, D_OFF:G_OFF]
    p_ext[halo_p:halo_p + TB, :] = p
    posf = ((t_start + tb) * TB + lax.broadcasted_iota(jnp.int32, (TB, 1), 0)).astype(F32)
    for g, w in enumerate(POOL_WINDOWS):
        sl = slice(g * LANES, (g + 1) * LANES)
        s = p[:, sl]
        for k in range(1, w):
            s = s + p_ext[pl.ds(halo_p - k, TB), sl]
        dlt = s / jnp.minimum(posf + 1.0, float(w)) - p[:, sl]
        yg = _mm(dlt, pw_ref[g], hi) * ps_ref[:, sl]
        outs_ref[3, :, sl] = yg.astype(outs_ref.dtype)
    pool_ref[...] = p_ext[pl.ds(halo_p + TB - (POOL_MAX - 1), POOL_MAX - 1), :]
    p_ext[0:halo_p, :] = p[TB - halo_p:, :]

    ge = _gelu(z_ref[:, C_OFF:D_OFF])
    u = ge[:, :BRANCH_W]
    vn = _layernorm(ge[:, BRANCH_W:], lng_ref[...], lnb_ref[...])
    r_i = lax.broadcasted_iota(jnp.int32, (TB, TB), 0)
    c_i = lax.broadcasted_iota(jnp.int32, (TB, TB), 1)
    for g in range(SGU_GROUPS):
        sl = slice(g * LANES, (g + 1) * LANES)
        wt = jnp.where(c_i <= r_i, sw_ref[g], 0.0)
        f = _mm(wt, vn[:, sl], hi) + sbt_ref[:, g:g + 1]
        outs_ref[2, :, sl] = (u[:, sl] * f).astype(outs_ref.dtype)

    rc, rs1, rs2 = rc_ref[...], rs1_ref[...], rs2_ref[...]
    kr = _rope(z_ref[:, K_OFF:V_OFF], rc, rs1, rs2)
    v = z_ref[:, V_OFF:C_OFF]
    qr = [_rope(z_ref[:, Q_OFF + c * LANES:Q_OFF + (c + 1) * LANES], rc, rs1, rs2)
          for c in range(N_HEADS * HEAD_DIM // LANES)]
    qi = lax.broadcasted_iota(jnp.int32, (TB, 2 * TB), 0)
    kj = lax.broadcasted_iota(jnp.int32, (TB, 2 * TB), 1)
    allowed = (kj > qi) & (kj <= qi + WINDOW) & ((kj >= TB) | (tb > 0))
    kp, vp = kprev[...], vprev[...]
    heads = [None] * N_HEADS
    for hk in range(N_KV):
        hs = slice(hk * HEAD_DIM, (hk + 1) * HEAD_DIM)
        kk = jnp.concatenate([kp[:, hs], kr[:, hs]], axis=0)
        vv = jnp.concatenate([vp[:, hs], v[:, hs]], axis=0)
        if not hi:
            kk, vv = kk.astype(BF16), vv.astype(BF16)
        for g in range(GQA_G):
            hq = hk * GQA_G + g
            per = LANES // HEAD_DIM
            qh = qr[hq // per][:, (hq % per) * HEAD_DIM:(hq % per + 1) * HEAD_DIM]
            s = _mm(qh, kk, hi, _NT) * (HEAD_DIM ** -0.5)
            s = jnp.where(allowed, s, -1e30)
            sink = sink_ref[hq]
            m = jnp.maximum(jnp.max(s, axis=-1, keepdims=True), sink)
            e = jnp.exp(s - m)
            den = jnp.sum(e, axis=-1, keepdims=True) + jnp.exp(sink - m)
            heads[hq] = _mm(e / den, vv, hi)
    outs_ref[1] = jnp.concatenate(heads, axis=1).astype(outs_ref.dtype)
    kst_ref[...] = kr
    vst_ref[...] = v
    kprev[...] = kr
    vprev[...] = v


def _branch_prompt(zs, batch, seq, lw, hi=False, t_start=0):
    n = zs.shape[0]
    nt = n // (batch * TB)
    rc, rs1, rs2 = _rope_tables(jnp.arange(seq))
    full = lambda shape: pl.BlockSpec(shape, lambda b, t: (0,) * len(shape))
    tab = pl.BlockSpec((TB, LANES), lambda b, t: (t_start + t, 0))
    kv_w = N_KV * HEAD_DIM
    return pl.pallas_call(
        functools.partial(_branch_prompt_kernel, hi=hi, t_start=t_start),
        grid=(batch, nt),
        in_specs=[pl.BlockSpec(memory_space=pltpu.SMEM),
                  pl.BlockSpec((TB, G_OFF), lambda b, t: (b * nt + t, 0)),
                  tab, tab, tab,
                  full((CONV_W, BRANCH_W)), full((1, BRANCH_W)), full((1, BRANCH_W)),
                  full((SGU_GROUPS, CHUNK, CHUNK)), full((CHUNK, SGU_GROUPS)),
                  full((len(POOL_WINDOWS), LANES, LANES)), full((1, BRANCH_W))],
        out_specs=[pl.BlockSpec((N_BRANCH, TB, BRANCH_W), lambda b, t: (0, b * nt + t, 0)),
                   pl.BlockSpec((None, CONV_W - 1, BRANCH_W), lambda b, t: (b, 0, 0)),
                   pl.BlockSpec((None, WINDOW, kv_w), lambda b, t: (b, 0, 0)),
                   pl.BlockSpec((None, WINDOW, kv_w), lambda b, t: (b, 0, 0)),
                   pl.BlockSpec((None, POOL_MAX - 1, BRANCH_W), lambda b, t: (b, 0, 0))],
        out_shape=[jax.ShapeDtypeStruct((N_BRANCH, n, BRANCH_W), F32 if hi else BF16),
                   jax.ShapeDtypeStruct((batch, CONV_W - 1, BRANCH_W), F32),
                   jax.ShapeDtypeStruct((batch, WINDOW, kv_w), F32),
                   jax.ShapeDtypeStruct((batch, WINDOW, kv_w), F32),
                   jax.ShapeDtypeStruct((batch, POOL_MAX - 1, BRANCH_W), F32)],
        scratch_shapes=[pltpu.VMEM((TB, kv_w), F32), pltpu.VMEM((TB, kv_w), F32),
                        pltpu.VMEM((SUBLANES + TB, BRANCH_W), F32),
                        pltpu.VMEM((POOL_MAX + TB, BRANCH_W), F32)],
        compiler_params=_cparams(("arbitrary", "arbitrary")),
        name="branch_prompt",
    )(lw["attn_sinks"], zs, rc, rs1, rs2, lw["conv_w"], lw["sgu_ln_g"].reshape(1, -1),
      lw["sgu_ln_b"].reshape(1, -1), lw["sgu_w"], lw["sgu_b"].T, lw["pool_w"],
      lw["pool_scale"].reshape(1, -1))


def _branch_sample_kernel(sink_ref, z_ref, rc_ref, rs1_ref, rs2_ref, cw_ref, lng_ref, lnb_ref,
                          ssc_ref, ssh_ref, pw_ref, ps_ref, cpast_ref, kc_ref, vc_ref, ppast_ref,
                          outs_ref, conv_ref, kst_ref, vst_ref, pool_ref, chunkv_ref,
                          qrot, krot, vnew):
    nb = z_ref.shape[0]

    ci = z_ref[:, A_C_OFF:A_B_OFF] * z_ref[:, A_H_OFF:A_C_OFF]
    y = cw_ref[CONV_W - 1:CONV_W, :] * ci
    for j in range(CONV_W - 1):
        y = y + cw_ref[j:j + 1, :] * cpast_ref[j]
    outs_ref[0] = z_ref[:, A_B_OFF:Q_OFF] * y
    for j in range(CONV_W - 2):
        conv_ref[j] = cpast_ref[j + 1]
    conv_ref[CONV_W - 2] = ci

    p = z_ref[:, D_OFF:G_OFF]
    for g, w in enumerate(POOL_WINDOWS):
        sl = slice(g * LANES, (g + 1) * LANES)
        s = p[:, sl]
        for k in range(1, w):
            s = s + ppast_ref[POOL_MAX - 1 - k, :, sl]
        dlt = s / float(min(PAST_LEN + 1, w)) - p[:, sl]
        outs_ref[3, :, sl] = _dot3(dlt, pw_ref[g]) * ps_ref[:, sl]
    for j in range(POOL_MAX - 2):
        pool_ref[j] = ppast_ref[j + 1]
    pool_ref[POOL_MAX - 2] = p

    ge = _gelu(z_ref[:, C_OFF:D_OFF])
    vn = _layernorm(ge[:, BRANCH_W:], lng_ref[...], lnb_ref[...])
    outs_ref[2] = ge[:, :BRANCH_W] * (vn * ssc_ref[...] + ssh_ref[...])
    chunkv_ref[...] = vn

    rc, rs1, rs2 = rc_ref[...], rs1_ref[...], rs2_ref[...]
    krot[...] = _rope(z_ref[:, K_OFF:V_OFF], rc, rs1, rs2)
    vnew[...] = z_ref[:, V_OFF:C_OFF]
    for c in range(N_HEADS * HEAD_DIM // LANES):
        cs = slice(c * LANES, (c + 1) * LANES)
        qrot[:, cs] = _rope(z_ref[:, Q_OFF + c * LANES:Q_OFF + (c + 1) * LANES], rc, rs1, rs2)
    row = lax.broadcasted_iota(jnp.int32, (WINDOW, LANES), 0)
    lo = lax.broadcasted_iota(jnp.int32, (WINDOW, LANES), 1) < HEAD_DIM
    lo1 = lax.broadcasted_iota(jnp.int32, (1, LANES), 1) < HEAD_DIM
    scale = HEAD_DIM ** -0.5

    def body(b, carry):
        kn = jnp.where(row == WINDOW - 1, krot[pl.ds(b, 1), :], pltpu.roll(kc_ref[b], WINDOW - 1, 0))
        vn_b = jnp.where(row == WINDOW - 1, vnew[pl.ds(b, 1), :], pltpu.roll(vc_ref[b], WINDOW - 1, 0))
        kst_ref[b] = kn
        vst_ref[b] = vn_b
        qb = qrot[pl.ds(b, 1), :]
        o = []
        for g in range(GQA_G):
            ca = qb[:, (g // 2) * LANES:(g // 2 + 1) * LANES]
            cb = qb[:, (2 + g // 2) * LANES:(3 + g // 2) * LANES]
            if g % 2 == 0:
                qrow = jnp.where(lo1, ca, pltpu.roll(cb, HEAD_DIM, 1))
            else:
                qrow = jnp.where(lo1, pltpu.roll(ca, HEAD_DIM, 1), cb)
            prod = kn * qrow
            pn = []
            for part, hq in ((jnp.where(lo, prod, 0.0), g), (jnp.where(lo, 0.0, prod), GQA_G + g)):
                s = jnp.sum(part, axis=1, keepdims=True) * scale
                sink = sink_ref[hq]
                m = jnp.maximum(jnp.max(s, axis=0, keepdims=True), sink)
                e = jnp.exp(s - m)
                den = jnp.sum(e, axis=0, keepdims=True) + jnp.exp(sink - m)
                pn.append(e / den)
            o.append(jnp.sum(jnp.where(lo, pn[0], pn[1]) * vn_b, axis=0, keepdims=True))
        sw = lambda x: pltpu.roll(x, HEAD_DIM, 1)
        chunks = [jnp.where(lo1, o[0], sw(o[1])), jnp.where(lo1, o[2], sw(o[3])),
                  jnp.where(lo1, sw(o[0]), o[1]), jnp.where(lo1, sw(o[2]), o[3])]
        outs_ref[1, pl.ds(b, 1), :] = jnp.concatenate(chunks, axis=1)
        return carry

    lax.fori_loop(0, nb, body, 0)


def _branch_sample(zs, lw, cpast, kc, vc, ppast):
    nb = zs.shape[0]
    kv_w = N_KV * HEAD_DIM
    rc, rs1, rs2 = _rope_tables(jnp.full((1,), PAST_LEN, jnp.int32))
    ssc = jnp.repeat(lw["sgu_w"][:, 0, 0], CHUNK).reshape(1, BRANCH_W)
    ssh = jnp.repeat(lw["sgu_b"][:, 0], CHUNK).reshape(1, BRANCH_W)
    vspec = pl.BlockSpec(memory_space=pltpu.VMEM)
    return pl.pallas_call(
        _branch_sample_kernel,
        in_specs=[pl.BlockSpec(memory_space=pltpu.SMEM)] + [vspec] * 15,
        out_specs=[vspec] * 6,
        out_shape=[jax.ShapeDtypeStruct((N_BRANCH, nb, BRANCH_W), F32),
                   jax.ShapeDtypeStruct((CONV_W - 1, nb, BRANCH_W), F32),
                   jax.ShapeDtypeStruct((nb, WINDOW, kv_w), F32),
                   jax.ShapeDtypeStruct((nb, WINDOW, kv_w), F32),
                   jax.ShapeDtypeStruct((POOL_MAX - 1, nb, BRANCH_W), F32),
                   jax.ShapeDtypeStruct((nb, BRANCH_W), F32)],
        scratch_shapes=[pltpu.VMEM((nb, N_HEADS * HEAD_DIM), F32), pltpu.VMEM((nb, kv_w), F32),
                        pltpu.VMEM((nb, kv_w), F32)],
        compiler_params=pltpu.CompilerParams(vmem_limit_bytes=VMEM_LIMIT),
        name="branch_sample",
    )(lw["attn_sinks"], zs, rc, rs1, rs2, lw["conv_w"], lw["sgu_ln_g"].reshape(1, -1),
      lw["sgu_ln_b"].reshape(1, -1), ssc, ssh, lw["pool_w"], lw["pool_scale"].reshape(1, -1),
      cpast, kc, vc, ppast)


def _merge_kernel(xn_ref, outs_ref, wg0_ref, wg1_ref, wb_ref, o_ref, acc_ref, *, hi):
    nbr = pl.program_id(2)

    @pl.when(nbr == 0)
    def _():
        acc_ref[...] = jnp.zeros_like(acc_ref)

    xn, outs = xn_ref[...], outs_ref[...]
    for h, wg_ref in enumerate((wg0_ref, wg1_ref)):
        cols = slice(h * TN_MERGE, (h + 1) * TN_MERGE)
        gate = jax.nn.sigmoid(_mm(xn, wg_ref[...], hi))
        acc_ref[:, cols] += gate * _mm(outs, wb_ref[:, cols], hi)

    @pl.when(nbr == N_BRANCH - 1)
    def _():
        o_ref[...] = acc_ref[...].astype(o_ref.dtype)


def _merge(xn, outs, w_in, w_branch, layer, hi):
    n, d = xn.shape
    tm = min(TM_MERGE, n)
    g_blk = G_OFF // TN_MERGE
    per_branch = D_MODEL // TN_MERGE
    wide = 2 * TN_MERGE
    gate_block = lambda h: pl.BlockSpec(
        (None, d, TN_MERGE), lambda i, j, b: (layer, 0, g_blk + b * per_branch + 2 * j + h))
    return pl.pallas_call(
        functools.partial(_merge_kernel, hi=hi),
        grid=(n // tm, D_MODEL // wide, N_BRANCH),
        in_specs=[pl.BlockSpec((tm, d), lambda i, j, b: (i, 0)),
                  pl.BlockSpec((None, tm, BRANCH_W), lambda i, j, b: (b, i, 0)),
                  gate_block(0), gate_block(1),
                  pl.BlockSpec((None, None, BRANCH_W, wide), lambda i, j, b: (layer, b, 0, j))],
        out_specs=pl.BlockSpec((tm, wide), lambda i, j, b: (i, j)),
        out_shape=jax.ShapeDtypeStruct((n, D_MODEL), F32 if hi else BF16),
        scratch_shapes=[pltpu.VMEM((tm, wide), F32)],
        compiler_params=_cparams(("parallel", "arbitrary", "arbitrary")),
        name="merge",
    )(xn, outs, w_in, w_in, w_branch)


def _wo_kernel(m_ref, w_ref, x_ref, o_ref, *, hi):
    o_ref[...] = x_ref[...] + _mm(m_ref[...], w_ref[...], hi)


def _out_proj(merged, w_o, x, layer, hi):
    n, d = x.shape
    tm = min(TM_WO, n)
    return pl.pallas_call(
        functools.partial(_wo_kernel, hi=hi),
        grid=(n // tm, d // TN_WO),
        in_specs=[pl.BlockSpec((tm, d), lambda i, j: (i, 0)),
                  pl.BlockSpec((None, d, TN_WO), lambda i, j: (layer, 0, j)),
                  pl.BlockSpec((tm, TN_WO), lambda i, j: (i, j))],
        out_specs=pl.BlockSpec((tm, TN_WO), lambda i, j: (i, j)),
        out_shape=jax.ShapeDtypeStruct((n, d), F32),
        compiler_params=_cparams(("parallel", "arbitrary")),
        name="out_proj",
    )(merged, w_o, x)


def _put_rows_kernel(y_hbm, t_ref, o_ref):
    del y_hbm
    o_ref[...] = t_ref[...]


def _put_last_rows(y, y_tail, batch, seq):
    d = y.shape[1]
    nt = seq // TAIL_EXACT
    return pl.pallas_call(
        _put_rows_kernel,
        grid=(batch,),
        in_specs=[pl.BlockSpec(memory_space=pl.ANY),
                  pl.BlockSpec((TAIL_EXACT, d), lambda b: (b, 0))],
        out_specs=pl.BlockSpec((TAIL_EXACT, d), lambda b: (b * nt + nt - 1, 0)),
        out_shape=jax.ShapeDtypeStruct(y.shape, y.dtype),
        input_output_aliases={0: 0},
        compiler_params=_cparams(("arbitrary",)),
        name="put_last_rows",
    )(y, y_tail)


def _route_rows(y, g_ref, wr_ref, br_ref, carry_ref):
    tm = y.shape[0]
    xn = _rms(y, g_ref[...])
    logits = _dot3(xn, wr_ref[...]) + br_ref[...]
    lane_i = lax.broadcasted_iota(jnp.int32, (tm, ROUTER_LANES), 1)
    lane = lane_i.astype(F32)
    lane_grp = lax.shift_right_logical(lane_i, 3).astype(F32)
    big = float(ROUTER_LANES)
    is_grp = (lane_i >= N_EXPERTS) & (lane_i < N_EXPERTS + N_EXP_GROUPS)
    lg = jnp.where(is_grp, logits, NEG)
    gmax = jnp.max(lg, axis=-1, keepdims=True)
    gsel = jnp.min(jnp.where(lg == gmax, lane - N_EXPERTS, big), axis=-1, keepdims=True)
    wg = 1.0 / jnp.sum(jnp.where(is_grp, jnp.exp(logits - gmax), 0.0), axis=-1, keepdims=True)
    in_grp = (lane_i < N_EXPERTS) & (lane_grp == gsel)
    v1 = jnp.where(in_grp, logits, NEG)
    m1 = jnp.max(v1, axis=-1, keepdims=True)
    i1 = jnp.min(jnp.where(v1 == m1, lane, big), axis=-1, keepdims=True)
    v2 = jnp.where(lane == i1, NEG, v1)
    m2 = jnp.max(v2, axis=-1, keepdims=True)
    i2 = jnp.min(jnp.where(v2 == m2, lane, big), axis=-1, keepdims=True)
    e2 = jnp.exp(m2 - m1)
    w1 = wg / (1.0 + e2)
    w2 = wg * e2 / (1.0 + e2)

    sel1 = lane == i1
    sel2 = lane == i2
    onehot = jnp.where(sel1 | sel2, 1.0, 0.0)
    r_i = lax.broadcasted_iota(jnp.int32, (tm, tm), 0)
    c_i = lax.broadcasted_iota(jnp.int32, (tm, tm), 1)
    before = jnp.where(c_i < r_i, 1.0, 0.0).astype(BF16)
    counts = _dot(before, onehot.astype(BF16)) + carry_ref[0:1, :]
    r1 = jnp.sum(jnp.where(sel1, counts, 0.0), axis=-1, keepdims=True)
    r2 = jnp.sum(jnp.where(sel2, counts, 0.0), axis=-1, keepdims=True)
    carry_ref[...] = carry_ref[...] + jnp.sum(onehot, axis=0, keepdims=True)

    info = jnp.zeros((tm, ROUTER_LANES), F32)
    for k, val in enumerate((i1, i2, r1, r2, w1, w2)):
        info = jnp.where(lane_i == k, val, info)
    return xn, info


def _router_kernel(yp_ref, ys_ref, g_ref, wr_ref, br_ref, xn_ref, info_ref, cnt_ref, zero_hbm,
                   carry_ref, zbuf, zsem, *, prompt_tiles, sorted_tiles, zero_per_step):
    i = pl.program_id(0)
    tile_lines = TM_MOE * ROW_CHUNKS

    @pl.when(i == 0)
    def _():
        carry_ref[...] = jnp.zeros_like(carry_ref)
        zbuf[...] = jnp.zeros_like(zbuf)

    def zero_tile(j):
        k = i * zero_per_step + j
        dst = zero_hbm.at[pl.ds(pl.multiple_of(k * tile_lines, tile_lines), tile_lines)]
        return k, pltpu.make_async_copy(zbuf, dst, zsem)

    for j in range(zero_per_step):
        k, cp = zero_tile(j)

        @pl.when(k < sorted_tiles)
        def _():
            cp.start()

    @pl.when(i < prompt_tiles)
    def _():
        xn, info = _route_rows(yp_ref[...], g_ref, wr_ref, br_ref, carry_ref)
        _store_row_tiles(xn_ref, xn)
        info_ref[...] = info

    @pl.when(i == prompt_tiles)
    def _():
        nb = ys_ref.shape[0]
        xn, info = _route_rows(ys_ref[...], g_ref, wr_ref, br_ref, carry_ref)
        _store_row_tiles(xn_ref, xn)
        info_ref[0:nb, :] = info

    cnt_ref[...] = carry_ref[...]

    for j in range(zero_per_step):
        k, cp = zero_tile(j)

        @pl.when(k < sorted_tiles)
        def _():
            cp.wait()


def _router(yp, ys, g, w_router, b_router, sorted_tiles):
    n_p, d = yp.shape
    nb = ys.shape[0]
    tm = min(TM_NORM, n_p)
    prompt_tiles = n_p // tm
    n = n_p + nb
    tile_lines = TM_MOE * ROW_CHUNKS
    return pl.pallas_call(
        functools.partial(_router_kernel, prompt_tiles=prompt_tiles, sorted_tiles=sorted_tiles,
                          zero_per_step=pl.cdiv(sorted_tiles, prompt_tiles + 1)),
        grid=(prompt_tiles + 1,),
        in_specs=[pl.BlockSpec((tm, d), lambda i: (jnp.minimum(i, prompt_tiles - 1), 0)),
                  pl.BlockSpec((nb, d), lambda i: (0, 0)),
                  pl.BlockSpec((1, d), lambda i: (0, 0)),
                  pl.BlockSpec((d, ROUTER_LANES), lambda i: (0, 0)),
                  pl.BlockSpec((1, ROUTER_LANES), lambda i: (0, 0))],
        out_specs=[pl.BlockSpec((tm * ROW_CHUNKS, LANES), lambda i: (i, 0)),
                   pl.BlockSpec((tm, ROUTER_LANES), lambda i: (i, 0)),
                   pl.BlockSpec((SUBLANES, ROUTER_LANES), lambda i: (0, 0)),
                   pl.BlockSpec(memory_space=pl.ANY)],
        out_shape=[jax.ShapeDtypeStruct((n * ROW_CHUNKS, LANES), F32),
                   jax.ShapeDtypeStruct((n, ROUTER_LANES), F32),
                   jax.ShapeDtypeStruct((SUBLANES, ROUTER_LANES), F32),
                   jax.ShapeDtypeStruct((sorted_tiles * tile_lines, LANES), F32)],
        scratch_shapes=[pltpu.VMEM((SUBLANES, ROUTER_LANES), F32),
                        pltpu.VMEM((tile_lines, LANES), F32), pltpu.SemaphoreType.DMA(())],
        compiler_params=_cparams(("arbitrary",)),
        name="router",
    )(yp, ys, g.reshape(1, d), w_router, b_router)


ROW_CHUNKS = D_MODEL // LANES


def _store_row_tiles(ref, x):
    for c in range(ROW_CHUNKS):
        ref[pl.ds(c, x.shape[0], stride=ROW_CHUNKS), :] = x[:, c * LANES:(c + 1) * LANES]


def _load_row_tiles(ref, rows):
    return jnp.concatenate(
        [ref[pl.ds(c, rows, stride=ROW_CHUNKS), :] for c in range(ROW_CHUNKS)], axis=1)


def _row_copy(src, i, dst, j, sem):
    def lines(k):
        start = k * ROW_CHUNKS
        if not isinstance(k, int):
            start = pl.multiple_of(start, ROW_CHUNKS)
        return pl.ds(start, ROW_CHUNKS)
    return pltpu.make_async_copy(src.at[lines(i)], dst.at[lines(j)], sem)


def _dispatch_kernel(pos_ref, x_ref, xs_in, xs_out, sem, *, rows, n_tokens):
    del xs_in
    i = pl.program_id(0)
    base = i * rows
    full_steps, tail = divmod(n_tokens, rows)

    def copies(r):
        t = base + r
        return (_row_copy(x_ref, r, xs_out, pos_ref[2 * t], sem),
                _row_copy(x_ref, r, xs_out, pos_ref[2 * t + 1], sem))

    def scatter(count):
        def start(r, c):
            for cp in copies(r):
                cp.start()
            return c
        lax.fori_loop(0, count, start, 0, unroll=8)
        for _ in range(2 * count):
            _row_copy(x_ref, 0, xs_out, 0, sem).wait()

    @pl.when(i < full_steps)
    def _():
        scatter(rows)

    if tail:
        @pl.when(i == full_steps)
        def _():
            scatter(tail)


def _dispatch(pos, xn, xs_zero):
    n_tokens = xn.shape[0] // ROW_CHUNKS
    rows = TM_NORM
    any_spec = pl.BlockSpec(memory_space=pl.ANY)
    return pl.pallas_call(
        functools.partial(_dispatch_kernel, rows=rows, n_tokens=n_tokens),
        grid_spec=pltpu.PrefetchScalarGridSpec(
            num_scalar_prefetch=1, grid=(pl.cdiv(n_tokens, rows),),
            in_specs=[pl.BlockSpec((rows * ROW_CHUNKS, LANES), lambda i, pos: (i, 0)), any_spec],
            out_specs=any_spec,
            scratch_shapes=[pltpu.SemaphoreType.DMA(())]),
        out_shape=jax.ShapeDtypeStruct(xs_zero.shape, xs_zero.dtype),
        input_output_aliases={2: 0},
        compiler_params=_cparams(("arbitrary",)),
        name="dispatch",
    )(pos, xn, xs_zero)


def _moe_kernel(te_ref, na_ref, first_ref, ord_ref, next_ref, x_ref, wg_hbm, wu_hbm, wd_hbm,
                ys_ref, wg_buf, wu_buf, wd_buf, wsem, *, layer):
    g = pl.program_id(0)
    n_act = na_ref[0]

    def weight_copies(e, slot):
        return [pltpu.make_async_copy(hbm.at[layer, e], vm.at[slot], wsem.at[slot])
                for hbm, vm in ((wg_hbm, wg_buf), (wu_hbm, wu_buf), (wd_hbm, wd_buf))]

    @pl.when(g == 0)
    def _():
        for cp in weight_copies(te_ref[0], 0):
            cp.start()

    @pl.when(g < n_act)
    def _():
        wslot = ord_ref[g] % 2

        @pl.when(first_ref[g] == 1)
        def _():
            @pl.when(next_ref[g] >= 0)
            def _():
                for cp in weight_copies(next_ref[g], 1 - wslot):
                    cp.start()
            for cp in weight_copies(te_ref[g], wslot):
                cp.wait()

        x = _load_row_tiles(x_ref, TM_MOE).astype(BF16)
        a = _dot(x, wg_buf[wslot].astype(BF16))
        u = _dot(x, wu_buf[wslot].astype(BF16))
        h = (a * jax.nn.sigmoid(a)) * u
        _store_row_tiles(ys_ref, _dot(h.astype(BF16), wd_buf[wslot].astype(BF16)))

    @pl.when(g >= n_act)
    def _():
        ys_ref[...] = jnp.zeros_like(ys_ref)


def _moe(plan, xs, w_gate, w_up, w_down, layer):
    d = D_MODEL
    n_tiles = xs.shape[0] // (TM_MOE * ROW_CHUNKS)
    any_spec = pl.BlockSpec(memory_space=pl.ANY)
    tile = (TM_MOE * ROW_CHUNKS, LANES)
    return pl.pallas_call(
        functools.partial(_moe_kernel, layer=layer),
        grid_spec=pltpu.PrefetchScalarGridSpec(
            num_scalar_prefetch=5, grid=(n_tiles,),
            in_specs=[pl.BlockSpec(tile, lambda g, te, na, *_: (jnp.minimum(g, na[0] - 1), 0)),
                      any_spec, any_spec, any_spec],
            out_specs=pl.BlockSpec(tile, lambda g, *_: (g, 0)),
            scratch_shapes=[pltpu.VMEM((2, d, EXPERT_FF), F32), pltpu.VMEM((2, d, EXPERT_FF), F32),
                            pltpu.VMEM((2, EXPERT_FF, d), F32), pltpu.SemaphoreType.DMA((2,))]),
        out_shape=jax.ShapeDtypeStruct(xs.shape, F32),
        compiler_params=_cparams(("arbitrary",)),
        name="moe_ffn",
    )(plan["tile_expert"], plan["n_active"], plan["first"], plan["ord"], plan["next"],
      xs, w_gate, w_up, w_down)


def _moe_sample_kernel(ue_ref, nu_ref, x_ref, info_ref, y_ref, g_ref, wg_ref, wu_ref, wd_ref,
                       xo_ref, xno_ref, acc_ref):
    s = pl.program_id(0)

    @pl.when(s == 0)
    def _():
        acc_ref[...] = y_ref[...]

    @pl.when(s < nu_ref[0])
    def _():
        x = _load_row_tiles(x_ref, y_ref.shape[0])
        a = _dot3(x, wg_ref[...])
        u = _dot3(x, wu_ref[...])
        out = _dot3((a * jax.nn.sigmoid(a)) * u, wd_ref[...])
        e = ue_ref[s].astype(F32)
        w = (jnp.where(info_ref[:, 0:1] == e, info_ref[:, 4:5], 0.0)
             + jnp.where(info_ref[:, 1:2] == e, info_ref[:, 5:6], 0.0))
        acc_ref[...] += w * out

    @pl.when(s == pl.num_programs(0) - 1)
    def _():
        xo_ref[...] = acc_ref[...]
        xno_ref[...] = _rms(acc_ref[...], g_ref[...])


def _moe_sample(xn2, info, y, g, w_gate, w_up, w_down, layer, row0):
    nb, d = y.shape
    blk0 = row0 // nb
    e_s = info[row0:, 0:2].astype(jnp.int32).reshape(-1)
    used = jnp.zeros((N_EXPERTS,), jnp.int32).at[e_s].set(1)
    n_used = jnp.sum(used)
    order = jnp.argsort(1 - used, stable=True).astype(jnp.int32)
    used_list = jnp.where(jnp.arange(N_EXPERTS) < n_used, order, order[n_used - 1])
    rows = lambda s, ue, nu: (blk0, 0)
    fixed = lambda s, ue, nu: (0, 0)
    wsel = lambda s, ue, nu: (layer, ue[s], 0, 0)
    return pl.pallas_call(
        _moe_sample_kernel,
        grid_spec=pltpu.PrefetchScalarGridSpec(
            num_scalar_prefetch=2, grid=(N_EXPERTS,),
            in_specs=[pl.BlockSpec((nb * ROW_CHUNKS, LANES), rows),
                      pl.BlockSpec((nb, ROUTER_LANES), rows),
                      pl.BlockSpec((nb, d), fixed),
                      pl.BlockSpec((1, d), fixed),
                      pl.BlockSpec((None, None, d, EXPERT_FF), wsel),
                      pl.BlockSpec((None, None, d, EXPERT_FF), wsel),
                      pl.BlockSpec((None, None, EXPERT_FF, d), wsel)],
            out_specs=[pl.BlockSpec((nb, d), fixed), pl.BlockSpec((nb, d), fixed)],
            scratch_shapes=[pltpu.VMEM((nb, d), F32)]),
        out_shape=[jax.ShapeDtypeStruct((nb, d), F32), jax.ShapeDtypeStruct((nb, d), F32)],
        compiler_params=_cparams(("arbitrary",)),
        name="moe_sample",
    )(used_list, n_used.reshape(1).astype(jnp.int32), xn2, info, y, g.reshape(1, d),
      w_gate, w_up, w_down)


def _combine_kernel(pos_ref, y_ref, info_ref, g_ref, ys_hbm, *rest, rows, write_x, blk0):
    if write_x:
        xo_ref, xno_ref, buf0, buf1, sem = rest
    else:
        xno_ref, buf0, buf1, sem = rest
    i = pl.program_id(0)

    def fetch(step, slot):
        base = (blk0 + step) * rows

        def body(r, c):
            t = base + r
            _row_copy(ys_hbm, pos_ref[2 * t], buf0.at[slot], r, sem.at[slot]).start()
            _row_copy(ys_hbm, pos_ref[2 * t + 1], buf1.at[slot], r, sem.at[slot]).start()
            return c
        lax.fori_loop(0, rows, body, 0, unroll=8)

    @pl.when(i == 0)
    def _():
        fetch(0, 0)

    @pl.when(i + 1 < pl.num_programs(0))
    def _():
        fetch(i + 1, (i + 1) % 2)

    slot = i % 2
    for r in range(rows):
        _row_copy(ys_hbm, 0, buf0.at[slot], r, sem.at[slot]).wait()
        _row_copy(ys_hbm, 0, buf1.at[slot], r, sem.at[slot]).wait()
    xnew = (y_ref[...] + info_ref[:, 4:5] * _load_row_tiles(buf0.at[slot], rows)
            + info_ref[:, 5:6] * _load_row_tiles(buf1.at[slot], rows))
    if write_x:
        xo_ref[...] = xnew
    xno_ref[...] = _rms(xnew, g_ref[...]).astype(xno_ref.dtype)


def _combine(pos, y, info, g, ys, norm_dtype, write_x, row0):
    n, d = y.shape
    rows = min(TM_COMBINE, n)
    assert row0 % rows == 0
    blk0 = row0 // rows
    tile = pl.BlockSpec((rows, d), lambda i, pos: (i, 0))
    out_specs = [tile]
    out_shape = [jax.ShapeDtypeStruct((n, d), norm_dtype)]
    if write_x:
        out_specs = [tile] + out_specs
        out_shape = [jax.ShapeDtypeStruct((n, d), F32)] + out_shape
    return pl.pallas_call(
        functools.partial(_combine_kernel, rows=rows, write_x=write_x, blk0=blk0),
        grid_spec=pltpu.PrefetchScalarGridSpec(
            num_scalar_prefetch=1, grid=(n // rows,),
            in_specs=[tile,
                      pl.BlockSpec((rows, ROUTER_LANES), lambda i, pos: (blk0 + i, 0)),
                      pl.BlockSpec((1, d), lambda i, pos: (0, 0)),
                      pl.BlockSpec(memory_space=pl.ANY)],
            out_specs=out_specs,
            scratch_shapes=[pltpu.VMEM((2, rows * ROW_CHUNKS, LANES), F32),
                            pltpu.VMEM((2, rows * ROW_CHUNKS, LANES), F32),
                            pltpu.SemaphoreType.DMA((2,))]),
        out_shape=out_shape,
        compiler_params=_cparams(("arbitrary",)),
        name="combine",
    )(pos, y, info, g.reshape(1, d), ys)


def _plan(info, cnt, n_tiles):
    count = cnt[0, :N_EXPERTS].astype(jnp.int32)
    tiles = (count + TM_MOE - 1) // TM_MOE
    tile_end = jnp.cumsum(tiles)
    row_off = (tile_end - tiles) * TM_MOE
    n_active = tile_end[-1]
    pos = (row_off[info[:, 0:2].astype(jnp.int32)] + info[:, 2:4].astype(jnp.int32)).reshape(-1)
    step = jnp.arange(n_tiles, dtype=jnp.int32)
    g = jnp.minimum(step, n_active - 1)
    tile_expert = jnp.sum((g[:, None] >= tile_end[None, :]).astype(jnp.int32), axis=1)
    prev = jnp.concatenate([jnp.full((1,), -1, jnp.int32), tile_expert[:-1]])
    first = ((step < n_active) & (tile_expert != prev)).astype(jnp.int32)
    ordinal = jnp.cumsum(first) - 1
    ids = jnp.arange(N_EXPERTS, dtype=jnp.int32)
    used_at_or_after = lax.cummin(jnp.where(tiles > 0, ids, N_EXPERTS), reverse=True)
    used_after = jnp.concatenate([used_at_or_after[1:], jnp.full((1,), N_EXPERTS, jnp.int32)])
    next_used = jnp.where(used_after == N_EXPERTS, -1, used_after)
    i32 = lambda a: a.astype(jnp.int32)
    return pos, dict(tile_expert=i32(tile_expert), n_active=i32(n_active.reshape(1)),
                     first=first, ord=i32(ordinal), next=i32(next_used[tile_expert]))


def kernel(x_prompt, x_sample, state_conv, cache_win_k, cache_win_v, state_pool, norm1_g, w_in,
           conv_w, attn_sinks, sgu_ln_g, sgu_ln_b, sgu_w, sgu_b, pool_w, pool_scale, w_branch, w_o,
           norm2_g, router_group_w, router_group_b, router_expert_w, router_expert_b, moe_w_gate,
           moe_w_up, moe_w_down, final_norm_g):
    batch, seq, d = x_prompt.shape
    assert seq % TB == 0 and seq >= 2 * TB and d == D_MODEL
    nb = x_sample.shape[0]
    depth = w_in.shape[0]
    n_p = batch * seq
    kv_w = N_KV * HEAD_DIM
    n_slots = 2 * (n_p + nb)
    n_tiles = (n_slots + N_EXPERTS * (TM_MOE - 1) + TM_MOE - 1) // TM_MOE

    xp = x_prompt.reshape(n_p, d)
    xs_ = x_sample.reshape(nb, d)
    xn_p = _rmsnorm(xp, norm1_g[0], BF16)
    xn_s = _rmsnorm(xs_, norm1_g[0], F32)

    pad = ROUTER_LANES - N_EXPERTS - N_EXP_GROUPS
    conv_p, k_p, v_p, pool_p = [], [], [], []
    conv_s, k_s, v_s, pool_s, chunk_s = [], [], [], [], []
    y_prompt = y_sample = None
    for l in range(depth):
        lw = dict(attn_sinks=attn_sinks[l], conv_w=conv_w[l], sgu_ln_g=sgu_ln_g[l],
                  sgu_ln_b=sgu_ln_b[l], sgu_w=sgu_w[l], sgu_b=sgu_b[l], pool_w=pool_w[l],
                  pool_scale=pool_scale[l])
        w_router = jnp.concatenate(
            [router_expert_w[l], router_group_w[l], jnp.zeros((d, pad), F32)], axis=1)
        b_router = jnp.concatenate(
            [router_expert_b[l], router_group_b[l], jnp.zeros((pad,), F32)]).reshape(1, -1)
        last = l == depth - 1
        g_next = final_norm_g if last else norm1_g[l + 1]

        zs_p = _in_proj(xn_p, w_in, l, False)
        outs_p, c_st, k_st, v_st, p_st = _branch_prompt(zs_p, batch, seq, lw)
        conv_p.append(c_st)
        k_p.append(k_st.reshape(batch, WINDOW, N_KV, HEAD_DIM))
        v_p.append(v_st.reshape(batch, WINDOW, N_KV, HEAD_DIM))
        pool_p.append(p_st)
        merged_p = _merge(xn_p, outs_p, w_in, w_branch, l, False)
        yp = _out_proj(merged_p, w_o, xp, l, False)

        zs_s = _in_proj(xn_s, w_in, l, True)
        outs_s, c_st, k_st, v_st, p_st, cv = _branch_sample(
            zs_s, lw, jnp.swapaxes(state_conv[l], 0, 1),
            cache_win_k[l].reshape(nb, WINDOW, kv_w), cache_win_v[l].reshape(nb, WINDOW, kv_w),
            jnp.swapaxes(state_pool[l], 0, 1))
        conv_s.append(jnp.swapaxes(c_st, 0, 1))
        k_s.append(k_st.reshape(nb, WINDOW, N_KV, HEAD_DIM))
        v_s.append(v_st.reshape(nb, WINDOW, N_KV, HEAD_DIM))
        pool_s.append(jnp.swapaxes(p_st, 0, 1))
        chunk_s.append(cv.reshape(nb, 1, BRANCH_W))

        xn_hi, outs_hi, x_hi = xn_s, outs_s, xs_
        if not last:
            x_t = xp.reshape(batch, seq, d)[:, seq - 2 * TB:].reshape(batch * 2 * TB, d)
            xn_t = _rmsnorm(x_t, norm1_g[l], F32)
            zs_t = _in_proj(xn_t, w_in, l, True)
            outs_t = _branch_prompt(zs_t, batch, seq, lw, hi=True, t_start=seq // TB - 2)[0]
            tail = lambda a: a.reshape(batch, 2 * TB, a.shape[-1])[:, 2 * TB - TAIL_EXACT:].reshape(
                batch * TAIL_EXACT, -1)
            outs_t = jnp.stack([tail(outs_t[b]) for b in range(N_BRANCH)])
            xn_hi = jnp.concatenate([tail(xn_t), xn_s])
            outs_hi = jnp.concatenate([outs_t, outs_s], axis=1)
            x_hi = jnp.concatenate([tail(x_t), xs_])
        y_hi = _out_proj(_merge(xn_hi, outs_hi, w_in, w_branch, l, True), w_o, x_hi, l, True)
        ys_ = y_hi[y_hi.shape[0] - nb:]
        if not last:
            yp = _put_last_rows(yp, y_hi[:batch * TAIL_EXACT], batch, seq)

        xn2, info, cnt, rows_sorted = _router(yp, ys_, norm2_g[l], w_router, b_router, n_tiles)
        pos, plan = _plan(info, cnt, n_tiles)
        rows_sorted = _dispatch(pos, xn2, rows_sorted)
        ffn = _moe(plan, rows_sorted, moe_w_gate, moe_w_up, moe_w_down, l)
        if last:
            (y_prompt,) = _combine(pos, yp, info, g_next, ffn, F32, False, 0)
            (y_sample,) = _combine(pos, ys_, info, g_next, ffn, F32, False, n_p)
        else:
            xp, xn_p = _combine(pos, yp, info, g_next, ffn, BF16, True, 0)
            xs_, xn_s = _moe_sample(xn2, info, ys_, g_next, moe_w_gate, moe_w_up, moe_w_down, l, n_p)

    return (y_prompt.reshape(batch, seq, d), y_sample.reshape(nb, 1, d),
            jnp.stack(conv_p), jnp.stack(k_p), jnp.stack(v_p), jnp.stack(pool_p),
            jnp.stack(conv_s), jnp.stack(k_s), jnp.stack(v_s), jnp.stack(pool_s),
            jnp.stack(chunk_s))
```

```python
import functools

import jax
import jax.numpy as jnp
from jax import lax
from jax.experimental import pallas as pl
from jax.experimental.pallas import tpu as pltpu

F32 = jnp.float32
BF16 = jnp.bfloat16

D_MODEL = 2048
BRANCH_W = 512
N_BRANCH = 4
CONV_W = 3
HEAD_DIM = 64
N_HEADS = 8
N_KV = 2
GQA_G = 4
WINDOW = 128
ROPE_THETA = 500000.0
ROPE_DIM = 16
CHUNK = 128
SGU_GROUPS = 4
POOL_WINDOWS = (2, 4, 8, 16)
POOL_MAX = 16
N_EXP_GROUPS = 4
EXP_PER_GROUP = 8
N_EXPERTS = 32
EXPERT_FF = 512
EPS = 1e-6
PAST_LEN = 16384
NEG = -3.0e38

A_H_OFF = 0
A_C_OFF = 512
A_B_OFF = 1024
Q_OFF = 1536
K_OFF = 2048
V_OFF = 2176
C_OFF = 2304
D_OFF = 3328
G_OFF = 3840

LANES = 128
SUBLANES = 8
VMEM_LIMIT = 52 * 1024 * 1024

TM_NORM = 512
TM_PROJ = 1024
TN_PROJ = 768
TM_MERGE = 2048
TN_MERGE = 256
TM_WO = 2048
TN_WO = 512
TM_MOE = 288
TM_COMBINE = 256
TB = 128
TAIL_EXACT = 16
ROUTER_LANES = 128


def _cparams(sem):
    return pltpu.CompilerParams(dimension_semantics=sem, vmem_limit_bytes=VMEM_LIMIT)


_NN = (((1,), (0,)), ((), ()))
_NT = (((1,), (1,)), ((), ()))


def _dot(a, b, dims=_NN):
    return lax.dot_general(a, b, dims, preferred_element_type=F32)


def _split(x):
    hi = x.astype(BF16)
    lo = (x - hi.astype(F32)).astype(BF16)
    return hi, lo


def _dot3(a, b, dims=_NN):
    ah, al = _split(a)
    bh, bl = _split(b)
    return _dot(ah, bh, dims) + _dot(al, bh, dims) + _dot(ah, bl, dims)


def _mm(a, w, hi, dims=_NN):
    if hi:
        return _dot3(a.astype(F32), w.astype(F32), dims)
    return _dot(a.astype(BF16), w.astype(BF16), dims)


def _rms(x, g):
    return (x * lax.rsqrt(jnp.mean(x * x, axis=-1, keepdims=True) + EPS)) * g


def _rmsnorm_kernel(x_ref, g_ref, o_ref):
    o_ref[...] = _rms(x_ref[...], g_ref[...]).astype(o_ref.dtype)


def _rmsnorm(x, g, out_dtype):
    n, d = x.shape
    tm = min(TM_NORM, n)
    return pl.pallas_call(
        _rmsnorm_kernel,
        grid=(n // tm,),
        in_specs=[pl.BlockSpec((tm, d), lambda i: (i, 0)),
                  pl.BlockSpec((1, d), lambda i: (0, 0))],
        out_specs=pl.BlockSpec((tm, d), lambda i: (i, 0)),
        out_shape=jax.ShapeDtypeStruct((n, d), out_dtype),
        compiler_params=_cparams(("parallel",)),
        name="rmsnorm",
    )(x, g.reshape(1, d))


def _proj_kernel(x_ref, w_ref, o_ref, *, hi):
    o_ref[...] = _mm(x_ref[...], w_ref[...], hi)


def _in_proj(xn, w_in, layer, hi):
    n, d = xn.shape
    tm = min(TM_PROJ // 2 if hi else TM_PROJ, n)
    return pl.pallas_call(
        functools.partial(_proj_kernel, hi=hi),
        grid=(n // tm, G_OFF // TN_PROJ),
        in_specs=[pl.BlockSpec((tm, d), lambda i, j: (i, 0)),
                  pl.BlockSpec((None, d, TN_PROJ), lambda i, j: (layer, 0, j))],
        out_specs=pl.BlockSpec((tm, TN_PROJ), lambda i, j: (i, j)),
        out_shape=jax.ShapeDtypeStruct((n, G_OFF), F32),
        compiler_params=_cparams(("parallel", "arbitrary")),
        name="in_proj",
    )(xn, w_in)


def _rope_tables(pos):
    half = ROPE_DIM // 2
    inv = jnp.power(jnp.float32(ROPE_THETA), -jnp.arange(half, dtype=F32) * (2.0 / ROPE_DIM))
    ang = pos.astype(F32)[:, None] * inv[None, :]
    cos, sin = jnp.cos(ang), jnp.sin(ang)
    t = pos.shape[0]
    rest = HEAD_DIM - ROPE_DIM
    c = jnp.concatenate([cos, cos, jnp.ones((t, rest), F32)], axis=1)
    s1 = jnp.concatenate([-sin, jnp.zeros((t, HEAD_DIM - half), F32)], axis=1)
    s2 = jnp.concatenate([jnp.zeros((t, half), F32), sin, jnp.zeros((t, rest), F32)], axis=1)
    rep = LANES // HEAD_DIM
    return jnp.tile(c, (1, rep)), jnp.tile(s1, (1, rep)), jnp.tile(s2, (1, rep))


def _rope(x, c, s1, s2):
    half = ROPE_DIM // 2
    return x * c + pltpu.roll(x, LANES - half, 1) * s1 + pltpu.roll(x, half, 1) * s2


def _gelu(x):
    return 0.5 * x * (1.0 + lax.erf(x * 0.7071067811865476))


def _layernorm(v, g, b):
    mu = jnp.mean(v, axis=-1, keepdims=True)
    vc = v - mu
    return vc * lax.rsqrt(jnp.mean(vc * vc, axis=-1, keepdims=True) + EPS) * g + b


def _branch_prompt_kernel(sink_ref, z_ref, rc_ref, rs1_ref, rs2_ref, cw_ref, lng_ref, lnb_ref,
                          sw_ref, sbt_ref, pw_ref, ps_ref,
                          outs_ref, conv_ref, kst_ref, vst_ref, pool_ref,
                          kprev, vprev, ci_ext, p_ext, *, hi, t_start):
    tb = pl.program_id(1)
    halo_c = SUBLANES
    halo_p = POOL_MAX

    @pl.when(tb == 0)
    def _():
        kprev[...] = jnp.zeros_like(kprev)
        vprev[...] = jnp.zeros_like(vprev)
        ci_ext[0:halo_c, :] = jnp.zeros((halo_c, BRANCH_W), F32)
        p_ext[0:halo_p, :] = jnp.zeros((halo_p, BRANCH_W), F32)

    ci = z_ref[:, A_C_OFF:A_B_OFF] * z_ref[:, A_H_OFF:A_C_OFF]
    ci_ext[halo_c:halo_c + TB, :] = ci
    y = cw_ref[2:3, :] * ci
    for j in range(CONV_W - 1):
        y = y + cw_ref[j:j + 1, :] * ci_ext[pl.ds(halo_c - (CONV_W - 1) + j, TB), :]
    outs_ref[0] = (z_ref[:, A_B_OFF:Q_OFF] * y).astype(outs_ref.dtype)
    conv_ref[...] = ci_ext[pl.ds(halo_c + TB - (CONV_W - 1), CONV_W - 1), :]
    ci_ext[0:halo_c, :] = ci[TB - halo_c:, :]

    p = z_ref[:, D_OFF:G_OFF]
    p_ext[halo_p:halo_p + TB, :] = p
    posf = ((t_start + tb) * TB + lax.broadcasted_iota(jnp.int32, (TB, 1), 0)).astype(F32)
    for g, w in enumerate(POOL_WINDOWS):
        sl = slice(g * LANES, (g + 1) * LANES)
        s = p[:, sl]
        for k in range(1, w):
            s = s + p_ext[pl.ds(halo_p - k, TB), sl]
        dlt = s / jnp.minimum(posf + 1.0, float(w)) - p[:, sl]
        yg = _mm(dlt, pw_ref[g], hi) * ps_ref[:, sl]
        outs_ref[3, :, sl] = yg.astype(outs_ref.dtype)
    pool_ref[...] = p_ext[pl.ds(halo_p + TB - (POOL_MAX - 1), POOL_MAX - 1), :]
    p_ext[0:halo_p, :] = p[TB - halo_p:, :]

    ge = _gelu(z_ref[:, C_OFF:D_OFF])
    u = ge[:, :BRANCH_W]
    vn = _layernorm(ge[:, BRANCH_W:], lng_ref[...], lnb_ref[...])
    r_i = lax.broadcasted_iota(jnp.int32, (TB, TB), 0)
    c_i = lax.broadcasted_iota(jnp.int32, (TB, TB), 1)
    for g in range(SGU_GROUPS):
        sl = slice(g * LANES, (g + 1) * LANES)
        wt = jnp.where(c_i <= r_i, sw_ref[g], 0.0)
        f = _mm(wt, vn[:, sl], hi) + sbt_ref[:, g:g + 1]
        outs_ref[2, :, sl] = (u[:, sl] * f).astype(outs_ref.dtype)

    rc, rs1, rs2 = rc_ref[...], rs1_ref[...], rs2_ref[...]
    kr = _rope(z_ref[:, K_OFF:V_OFF], rc, rs1, rs2)
    v = z_ref[:, V_OFF:C_OFF]
    qr = [_rope(z_ref[:, Q_OFF + c * LANES:Q_OFF + (c + 1) * LANES], rc, rs1, rs2)
          for c in range(N_HEADS * HEAD_DIM // LANES)]
    qi = lax.broadcasted_iota(jnp.int32, (TB, 2 * TB), 0)
    kj = lax.broadcasted_iota(jnp.int32, (TB, 2 * TB), 1)
    allowed = (kj > qi) & (kj <= qi + WINDOW) & ((kj >= TB) | (tb > 0))
    kp, vp = kprev[...], vprev[...]
    heads = [None] * N_HEADS
    for hk in range(N_KV):
        hs = slice(hk * HEAD_DIM, (hk + 1) * HEAD_DIM)
        kk = jnp.concatenate([kp[:, hs], kr[:, hs]], axis=0)
        vv = jnp.concatenate([vp[:, hs], v[:, hs]], axis=0)
        if not hi:
            kk, vv = kk.astype(BF16), vv.astype(BF16)
        for g in range(GQA_G):
            hq = hk * GQA_G + g
            per = LANES // HEAD_DIM
            qh = qr[hq // per][:, (hq % per) * HEAD_DIM:(hq % per + 1) * HEAD_DIM]
            s = _mm(qh, kk, hi, _NT) * (HEAD_DIM ** -0.5)
            s = jnp.where(allowed, s, -1e30)
            sink = sink_ref[hq]
            m = jnp.maximum(jnp.max(s, axis=-1, keepdims=True), sink)
            e = jnp.exp(s - m)
            den = jnp.sum(e, axis=-1, keepdims=True) + jnp.exp(sink - m)
            heads[hq] = _mm(e / den, vv, hi)
    outs_ref[1] = jnp.concatenate(heads, axis=1).astype(outs_ref.dtype)
    kst_ref[...] = kr
    vst_ref[...] = v
    kprev[...] = kr
    vprev[...] = v


def _branch_prompt(zs, batch, seq, lw, hi=False, t_start=0):
    n = zs.shape[0]
    nt = n // (batch * TB)
    rc, rs1, rs2 = _rope_tables(jnp.arange(seq))
    full = lambda shape: pl.BlockSpec(shape, lambda b, t: (0,) * len(shape))
    tab = pl.BlockSpec((TB, LANES), lambda b, t: (t_start + t, 0))
    kv_w = N_KV * HEAD_DIM
    return pl.pallas_call(
        functools.partial(_branch_prompt_kernel, hi=hi, t_start=t_start),
        grid=(batch, nt),
        in_specs=[pl.BlockSpec(memory_space=pltpu.SMEM),
                  pl.BlockSpec((TB, G_OFF), lambda b, t: (b * nt + t, 0)),
                  tab, tab, tab,
                  full((CONV_W, BRANCH_W)), full((1, BRANCH_W)), full((1, BRANCH_W)),
                  full((SGU_GROUPS, CHUNK, CHUNK)), full((CHUNK, SGU_GROUPS)),
                  full((len(POOL_WINDOWS), LANES, LANES)), full((1, BRANCH_W))],
        out_specs=[pl.BlockSpec((N_BRANCH, TB, BRANCH_W), lambda b, t: (0, b * nt + t, 0)),
                   pl.BlockSpec((None, CONV_W - 1, BRANCH_W), lambda b, t: (b, 0, 0)),
                   pl.BlockSpec((None, WINDOW, kv_w), lambda b, t: (b, 0, 0)),
                   pl.BlockSpec((None, WINDOW, kv_w), lambda b, t: (b, 0, 0)),
                   pl.BlockSpec((None, POOL_MAX - 1, BRANCH_W), lambda b, t: (b, 0, 0))],
        out_shape=[jax.ShapeDtypeStruct((N_BRANCH, n, BRANCH_W), F32 if hi else BF16),
                   jax.ShapeDtypeStruct((batch, CONV_W - 1, BRANCH_W), F32),
                   jax.ShapeDtypeStruct((batch, WINDOW, kv_w), F32),
                   jax.ShapeDtypeStruct((batch, WINDOW, kv_w), F32),
                   jax.ShapeDtypeStruct((batch, POOL_MAX - 1, BRANCH_W), F32)],
        scratch_shapes=[pltpu.VMEM((TB, kv_w), F32), pltpu.VMEM((TB, kv_w), F32),
                        pltpu.VMEM((SUBLANES + TB, BRANCH_W), F32),
                        pltpu.VMEM((POOL_MAX + TB, BRANCH_W), F32)],
        compiler_params=_cparams(("arbitrary", "arbitrary")),
        name="branch_prompt",
    )(lw["attn_sinks"], zs, rc, rs1, rs2, lw["conv_w"], lw["sgu_ln_g"].reshape(1, -1),
      lw["sgu_ln_b"].reshape(1, -1), lw["sgu_w"], lw["sgu_b"].T, lw["pool_w"],
      lw["pool_scale"].reshape(1, -1))


def _branch_sample_kernel(sink_ref, z_ref, rc_ref, rs1_ref, rs2_ref, cw_ref, lng_ref, lnb_ref,
                          ssc_ref, ssh_ref, pw_ref, ps_ref, cpast_ref, kc_ref, vc_ref, ppast_ref,
                          outs_ref, conv_ref, kst_ref, vst_ref, pool_ref, chunkv_ref,
                          qrot, krot, vnew):
    nb = z_ref.shape[0]

    ci = z_ref[:, A_C_OFF:A_B_OFF] * z_ref[:, A_H_OFF:A_C_OFF]
    y = cw_ref[CONV_W - 1:CONV_W, :] * ci
    for j in range(CONV_W - 1):
        y = y + cw_ref[j:j + 1, :] * cpast_ref[j]
    outs_ref[0] = z_ref[:, A_B_OFF:Q_OFF] * y
    for j in range(CONV_W - 2):
        conv_ref[j] = cpast_ref[j + 1]
    conv_ref[CONV_W - 2] = ci

    p = z_ref[:, D_OFF:G_OFF]
    for g, w in enumerate(POOL_WINDOWS):
        sl = slice(g * LANES, (g + 1) * LANES)
        s = p[:, sl]
        for k in range(1, w):
            s = s + ppast_ref[POOL_MAX - 1 - k, :, sl]
        dlt = s / float(min(PAST_LEN + 1, w)) - p[:, sl]
        outs_ref[3, :, sl] = _dot3(dlt, pw_ref[g]) * ps_ref[:, sl]
    for j in range(POOL_MAX - 2):
        pool_ref[j] = ppast_ref[j + 1]
    pool_ref[POOL_MAX - 2] = p

    ge = _gelu(z_ref[:, C_OFF:D_OFF])
    vn = _layernorm(ge[:, BRANCH_W:], lng_ref[...], lnb_ref[...])
    outs_ref[2] = ge[:, :BRANCH_W] * (vn * ssc_ref[...] + ssh_ref[...])
    chunkv_ref[...] = vn

    rc, rs1, rs2 = rc_ref[...], rs1_ref[...], rs2_ref[...]
    krot[...] = _rope(z_ref[:, K_OFF:V_OFF], rc, rs1, rs2)
    vnew[...] = z_ref[:, V_OFF:C_OFF]
    for c in range(N_HEADS * HEAD_DIM // LANES):
        cs = slice(c * LANES, (c + 1) * LANES)
        qrot[:, cs] = _rope(z_ref[:, Q_OFF + c * LANES:Q_OFF + (c + 1) * LANES], rc, rs1, rs2)
    row = lax.broadcasted_iota(jnp.int32, (WINDOW, LANES), 0)
    lo = lax.broadcasted_iota(jnp.int32, (WINDOW, LANES), 1) < HEAD_DIM
    lo1 = lax.broadcasted_iota(jnp.int32, (1, LANES), 1) < HEAD_DIM
    scale = HEAD_DIM ** -0.5

    def body(b, carry):
        kn = jnp.where(row == WINDOW - 1, krot[pl.ds(b, 1), :], pltpu.roll(kc_ref[b], WINDOW - 1, 0))
        vn_b = jnp.where(row == WINDOW - 1, vnew[pl.ds(b, 1), :], pltpu.roll(vc_ref[b], WINDOW - 1, 0))
        kst_ref[b] = kn
        vst_ref[b] = vn_b
        qb = qrot[pl.ds(b, 1), :]
        o = []
        for g in range(GQA_G):
            ca = qb[:, (g // 2) * LANES:(g // 2 + 1) * LANES]
            cb = qb[:, (2 + g // 2) * LANES:(3 + g // 2) * LANES]
            if g % 2 == 0:
                qrow = jnp.where(lo1, ca, pltpu.roll(cb, HEAD_DIM, 1))
            else:
                qrow = jnp.where(lo1, pltpu.roll(ca, HEAD_DIM, 1), cb)
            prod = kn * qrow
            pn = []
            for part, hq in ((jnp.where(lo, prod, 0.0), g), (jnp.where(lo, 0.0, prod), GQA_G + g)):
                s = jnp.sum(part, axis=1, keepdims=True) * scale
                sink = sink_ref[hq]
                m = jnp.maximum(jnp.max(s, axis=0, keepdims=True), sink)
                e = jnp.exp(s - m)
                den = jnp.sum(e, axis=0, keepdims=True) + jnp.exp(sink - m)
                pn.append(e / den)
            o.append(jnp.sum(jnp.where(lo, pn[0], pn[1]) * vn_b, axis=0, keepdims=True))
        sw = lambda x: pltpu.roll(x, HEAD_DIM, 1)
        chunks = [jnp.where(lo1, o[0], sw(o[1])), jnp.where(lo1, o[2], sw(o[3])),
                  jnp.where(lo1, sw(o[0]), o[1]), jnp.where(lo1, sw(o[2]), o[3])]
        outs_ref[1, pl.ds(b, 1), :] = jnp.concatenate(chunks, axis=1)
        return carry

    lax.fori_loop(0, nb, body, 0)


def _branch_sample(zs, lw, cpast, kc, vc, ppast):
    nb = zs.shape[0]
    kv_w = N_KV * HEAD_DIM
    rc, rs1, rs2 = _rope_tables(jnp.full((1,), PAST_LEN, jnp.int32))
    ssc = jnp.repeat(lw["sgu_w"][:, 0, 0], CHUNK).reshape(1, BRANCH_W)
    ssh = jnp.repeat(lw["sgu_b"][:, 0], CHUNK).reshape(1, BRANCH_W)
    vspec = pl.BlockSpec(memory_space=pltpu.VMEM)
    return pl.pallas_call(
        _branch_sample_kernel,
        in_specs=[pl.BlockSpec(memory_space=pltpu.SMEM)] + [vspec] * 15,
        out_specs=[vspec] * 6,
        out_shape=[jax.ShapeDtypeStruct((N_BRANCH, nb, BRANCH_W), F32),
                   jax.ShapeDtypeStruct((CONV_W - 1, nb, BRANCH_W), F32),
                   jax.ShapeDtypeStruct((nb, WINDOW, kv_w), F32),
                   jax.ShapeDtypeStruct((nb, WINDOW, kv_w), F32),
                   jax.ShapeDtypeStruct((POOL_MAX - 1, nb, BRANCH_W), F32),
                   jax.ShapeDtypeStruct((nb, BRANCH_W), F32)],
        scratch_shapes=[pltpu.VMEM((nb, N_HEADS * HEAD_DIM), F32), pltpu.VMEM((nb, kv_w), F32),
                        pltpu.VMEM((nb, kv_w), F32)],
        compiler_params=pltpu.CompilerParams(vmem_limit_bytes=VMEM_LIMIT),
        name="branch_sample",
    )(lw["attn_sinks"], zs, rc, rs1, rs2, lw["conv_w"], lw["sgu_ln_g"].reshape(1, -1),
      lw["sgu_ln_b"].reshape(1, -1), ssc, ssh, lw["pool_w"], lw["pool_scale"].reshape(1, -1),
      cpast, kc, vc, ppast)


def _merge_kernel(xn_ref, outs_ref, wg0_ref, wg1_ref, wb_ref, o_ref, acc_ref, *, hi):
    nbr = pl.program_id(2)

    @pl.when(nbr == 0)
    def _():
        acc_ref[...] = jnp.zeros_like(acc_ref)

    xn, outs = xn_ref[...], outs_ref[...]
    for h, wg_ref in enumerate((wg0_ref, wg1_ref)):
        cols = slice(h * TN_MERGE, (h + 1) * TN_MERGE)
        gate = jax.nn.sigmoid(_mm(xn, wg_ref[...], hi))
        acc_ref[:, cols] += gate * _mm(outs, wb_ref[:, cols], hi)

    @pl.when(nbr == N_BRANCH - 1)
    def _():
        o_ref[...] = acc_ref[...].astype(o_ref.dtype)


def _merge(xn, outs, w_in, w_branch, layer, hi):
    n, d = xn.shape
    tm = min(TM_MERGE, n)
    g_blk = G_OFF // TN_MERGE
    per_branch = D_MODEL // TN_MERGE
    wide = 2 * TN_MERGE
    gate_block = lambda h: pl.BlockSpec(
        (None, d, TN_MERGE), lambda i, j, b: (layer, 0, g_blk + b * per_branch + 2 * j + h))
    return pl.pallas_call(
        functools.partial(_merge_kernel, hi=hi),
        grid=(n // tm, D_MODEL // wide, N_BRANCH),
        in_specs=[pl.BlockSpec((tm, d), lambda i, j, b: (i, 0)),
                  pl.BlockSpec((None, tm, BRANCH_W), lambda i, j, b: (b, i, 0)),
                  gate_block(0), gate_block(1),
                  pl.BlockSpec((None, None, BRANCH_W, wide), lambda i, j, b: (layer, b, 0, j))],
        out_specs=pl.BlockSpec((tm, wide), lambda i, j, b: (i, j)),
        out_shape=jax.ShapeDtypeStruct((n, D_MODEL), F32 if hi else BF16),
        scratch_shapes=[pltpu.VMEM((tm, wide), F32)],
        compiler_params=_cparams(("parallel", "arbitrary", "arbitrary")),
        name="merge",
    )(xn, outs, w_in, w_in, w_branch)


def _wo_kernel(m_ref, w_ref, x_ref, o_ref, *, hi):
    o_ref[...] = x_ref[...] + _mm(m_ref[...], w_ref[...], hi)


def _out_proj(merged, w_o, x, layer, hi):
    n, d = x.shape
    tm = min(TM_WO, n)
    return pl.pallas_call(
        functools.partial(_wo_kernel, hi=hi),
        grid=(n // tm, d // TN_WO),
        in_specs=[pl.BlockSpec((tm, d), lambda i, j: (i, 0)),
                  pl.BlockSpec((None, d, TN_WO), lambda i, j: (layer, 0, j)),
                  pl.BlockSpec((tm, TN_WO), lambda i, j: (i, j))],
        out_specs=pl.BlockSpec((tm, TN_WO), lambda i, j: (i, j)),
        out_shape=jax.ShapeDtypeStruct((n, d), F32),
        compiler_params=_cparams(("parallel", "arbitrary")),
        name="out_proj",
    )(merged, w_o, x)


def _put_rows_kernel(y_hbm, t_ref, o_ref):
    del y_hbm
    o_ref[...] = t_ref[...]


def _put_last_rows(y, y_tail, batch, seq):
    d = y.shape[1]
    nt = seq // TAIL_EXACT
    return pl.pallas_call(
        _put_rows_kernel,
        grid=(batch,),
        in_specs=[pl.BlockSpec(memory_space=pl.ANY),
                  pl.BlockSpec((TAIL_EXACT, d), lambda b: (b, 0))],
        out_specs=pl.BlockSpec((TAIL_EXACT, d), lambda b: (b * nt + nt - 1, 0)),
        out_shape=jax.ShapeDtypeStruct(y.shape, y.dtype),
        input_output_aliases={0: 0},
        compiler_params=_cparams(("arbitrary",)),
        name="put_last_rows",
    )(y, y_tail)


def _route_rows(y, g_ref, wr_ref, br_ref, carry_ref):
    tm = y.shape[0]
    xn = _rms(y, g_ref[...])
    logits = _dot3(xn, wr_ref[...]) + br_ref[...]
    lane_i = lax.broadcasted_iota(jnp.int32, (tm, ROUTER_LANES), 1)
    lane = lane_i.astype(F32)
    lane_grp = lax.shift_right_logical(lane_i, 3).astype(F32)
    big = float(ROUTER_LANES)
    is_grp = (lane_i >= N_EXPERTS) & (lane_i < N_EXPERTS + N_EXP_GROUPS)
    lg = jnp.where(is_grp, logits, NEG)
    gmax = jnp.max(lg, axis=-1, keepdims=True)
    gsel = jnp.min(jnp.where(lg == gmax, lane - N_EXPERTS, big), axis=-1, keepdims=True)
    wg = 1.0 / jnp.sum(jnp.where(is_grp, jnp.exp(logits - gmax), 0.0), axis=-1, keepdims=True)
    in_grp = (lane_i < N_EXPERTS) & (lane_grp == gsel)
    v1 = jnp.where(in_grp, logits, NEG)
    m1 = jnp.max(v1, axis=-1, keepdims=True)
    i1 = jnp.min(jnp.where(v1 == m1, lane, big), axis=-1, keepdims=True)
    v2 = jnp.where(lane == i1, NEG, v1)
    m2 = jnp.max(v2, axis=-1, keepdims=True)
    i2 = jnp.min(jnp.where(v2 == m2, lane, big), axis=-1, keepdims=True)
    e2 = jnp.exp(m2 - m1)
    w1 = wg / (1.0 + e2)
    w2 = wg * e2 / (1.0 + e2)

    sel1 = lane == i1
    sel2 = lane == i2
    onehot = jnp.where(sel1 | sel2, 1.0, 0.0)
    r_i = lax.broadcasted_iota(jnp.int32, (tm, tm), 0)
    c_i = lax.broadcasted_iota(jnp.int32, (tm, tm), 1)
    before = jnp.where(c_i < r_i, 1.0, 0.0).astype(BF16)
    counts = _dot(before, onehot.astype(BF16)) + carry_ref[0:1, :]
    r1 = jnp.sum(jnp.where(sel1, counts, 0.0), axis=-1, keepdims=True)
    r2 = jnp.sum(jnp.where(sel2, counts, 0.0), axis=-1, keepdims=True)
    carry_ref[...] = carry_ref[...] + jnp.sum(onehot, axis=0, keepdims=True)

    info = jnp.zeros((tm, ROUTER_LANES), F32)
    for k, val in enumerate((i1, i2, r1, r2, w1, w2)):
        info = jnp.where(lane_i == k, val, info)
    return xn, info


def _router_kernel(yp_ref, ys_ref, g_ref, wr_ref, br_ref, xn_ref, info_ref, cnt_ref, zero_hbm,
                   carry_ref, zbuf, zsem, *, prompt_tiles, sorted_tiles, zero_per_step):
    i = pl.program_id(0)
    tile_lines = TM_MOE * ROW_CHUNKS

    @pl.when(i == 0)
    def _():
        carry_ref[...] = jnp.zeros_like(carry_ref)
        zbuf[...] = jnp.zeros_like(zbuf)

    def zero_tile(j):
        k = i * zero_per_step + j
        dst = zero_hbm.at[pl.ds(pl.multiple_of(k * tile_lines, tile_lines), tile_lines)]
        return k, pltpu.make_async_copy(zbuf, dst, zsem)

    for j in range(zero_per_step):
        k, cp = zero_tile(j)

        @pl.when(k < sorted_tiles)
        def _():
            cp.start()

    @pl.when(i < prompt_tiles)
    def _():
        xn, info = _route_rows(yp_ref[...], g_ref, wr_ref, br_ref, carry_ref)
        _store_row_tiles(xn_ref, xn)
        info_ref[...] = info

    @pl.when(i == prompt_tiles)
    def _():
        nb = ys_ref.shape[0]
        xn, info = _route_rows(ys_ref[...], g_ref, wr_ref, br_ref, carry_ref)
        _store_row_tiles(xn_ref, xn)
        info_ref[0:nb, :] = info

    cnt_ref[...] = carry_ref[...]

    for j in range(zero_per_step):
        k, cp = zero_tile(j)

        @pl.when(k < sorted_tiles)
        def _():
            cp.wait()


def _router(yp, ys, g, w_router, b_router, sorted_tiles):
    n_p, d = yp.shape
    nb = ys.shape[0]
    tm = min(TM_NORM, n_p)
    prompt_tiles = n_p // tm
    n = n_p + nb
    tile_lines = TM_MOE * ROW_CHUNKS
    return pl.pallas_call(
        functools.partial(_router_kernel, prompt_tiles=prompt_tiles, sorted_tiles=sorted_tiles,
                          zero_per_step=pl.cdiv(sorted_tiles, prompt_tiles + 1)),
        grid=(prompt_tiles + 1,),
        in_specs=[pl.BlockSpec((tm, d), lambda i: (jnp.minimum(i, prompt_tiles - 1), 0)),
                  pl.BlockSpec((nb, d), lambda i: (0, 0)),
                  pl.BlockSpec((1, d), lambda i: (0, 0)),
                  pl.BlockSpec((d, ROUTER_LANES), lambda i: (0, 0)),
                  pl.BlockSpec((1, ROUTER_LANES), lambda i: (0, 0))],
        out_specs=[pl.BlockSpec((tm * ROW_CHUNKS, LANES), lambda i: (i, 0)),
                   pl.BlockSpec((tm, ROUTER_LANES), lambda i: (i, 0)),
                   pl.BlockSpec((SUBLANES, ROUTER_LANES), lambda i: (0, 0)),
                   pl.BlockSpec(memory_space=pl.ANY)],
        out_shape=[jax.ShapeDtypeStruct((n * ROW_CHUNKS, LANES), F32),
                   jax.ShapeDtypeStruct((n, ROUTER_LANES), F32),
                   jax.ShapeDtypeStruct((SUBLANES, ROUTER_LANES), F32),
                   jax.ShapeDtypeStruct((sorted_tiles * tile_lines, LANES), F32)],
        scratch_shapes=[pltpu.VMEM((SUBLANES, ROUTER_LANES), F32),
                        pltpu.VMEM((tile_lines, LANES), F32), pltpu.SemaphoreType.DMA(())],
        compiler_params=_cparams(("arbitrary",)),
        name="router",
    )(yp, ys, g.reshape(1, d), w_router, b_router)


ROW_CHUNKS = D_MODEL // LANES


def _store_row_tiles(ref, x):
    for c in range(ROW_CHUNKS):
        ref[pl.ds(c, x.shape[0], stride=ROW_CHUNKS), :] = x[:, c * LANES:(c + 1) * LANES]


def _load_row_tiles(ref, rows):
    return jnp.concatenate(
        [ref[pl.ds(c, rows, stride=ROW_CHUNKS), :] for c in range(ROW_CHUNKS)], axis=1)


def _row_copy(src, i, dst, j, sem):
    def lines(k):
        start = k * ROW_CHUNKS
        if not isinstance(k, int):
            start = pl.multiple_of(start, ROW_CHUNKS)
        return pl.ds(start, ROW_CHUNKS)
    return pltpu.make_async_copy(src.at[lines(i)], dst.at[lines(j)], sem)


def _dispatch_kernel(pos_ref, x_ref, xs_in, xs_out, sem, *, rows, n_tokens):
    del xs_in
    i = pl.program_id(0)
    base = i * rows
    full_steps, tail = divmod(n_tokens, rows)

    def copies(r):
        t = base + r
        return (_row_copy(x_ref, r, xs_out, pos_ref[2 * t], sem),
                _row_copy(x_ref, r, xs_out, pos_ref[2 * t + 1], sem))

    def scatter(count):
        def start(r, c):
            for queue, cp in enumerate(copies(r)):
                cp.start(priority=queue)
            return c
        lax.fori_loop(0, count, start, 0, unroll=8)
        for _ in range(2 * count):
            _row_copy(x_ref, 0, xs_out, 0, sem).wait()

    @pl.when(i < full_steps)
    def _():
        scatter(rows)

    if tail:
        @pl.when(i == full_steps)
        def _():
            scatter(tail)


def _dispatch(pos, xn, xs_zero):
    n_tokens = xn.shape[0] // ROW_CHUNKS
    rows = TM_NORM
    any_spec = pl.BlockSpec(memory_space=pl.ANY)
    return pl.pallas_call(
        functools.partial(_dispatch_kernel, rows=rows, n_tokens=n_tokens),
        grid_spec=pltpu.PrefetchScalarGridSpec(
            num_scalar_prefetch=1, grid=(pl.cdiv(n_tokens, rows),),
            in_specs=[pl.BlockSpec((rows * ROW_CHUNKS, LANES), lambda i, pos: (i, 0)), any_spec],
            out_specs=any_spec,
            scratch_shapes=[pltpu.SemaphoreType.DMA(())]),
        out_shape=jax.ShapeDtypeStruct(xs_zero.shape, xs_zero.dtype),
        input_output_aliases={2: 0},
        compiler_params=_cparams(("arbitrary",)),
        name="dispatch",
    )(pos, xn, xs_zero)


def _moe_kernel(te_ref, na_ref, first_ref, ord_ref, next_ref, x_ref, wg_hbm, wu_hbm, wd_hbm,
                ys_ref, wg_buf, wu_buf, wd_buf, wsem, *, layer):
    g = pl.program_id(0)
    n_act = na_ref[0]

    def weight_copies(e, slot):
        return [pltpu.make_async_copy(hbm.at[layer, e], vm.at[slot], wsem.at[slot])
                for hbm, vm in ((wg_hbm, wg_buf), (wu_hbm, wu_buf), (wd_hbm, wd_buf))]

    @pl.when(g == 0)
    def _():
        for cp in weight_copies(te_ref[0], 0):
            cp.start()

    @pl.when(g < n_act)
    def _():
        wslot = ord_ref[g] % 2

        @pl.when(first_ref[g] == 1)
        def _():
            @pl.when(next_ref[g] >= 0)
            def _():
                for cp in weight_copies(next_ref[g], 1 - wslot):
                    cp.start()
            for cp in weight_copies(te_ref[g], wslot):
                cp.wait()

        x = _load_row_tiles(x_ref, TM_MOE).astype(BF16)
        a = _dot(x, wg_buf[wslot].astype(BF16))
        u = _dot(x, wu_buf[wslot].astype(BF16))
        h = (a * jax.nn.sigmoid(a)) * u
        ys_ref[...] = _dot(h.astype(BF16), wd_buf[wslot].astype(BF16))

    @pl.when(g >= n_act)
    def _():
        ys_ref[...] = jnp.zeros_like(ys_ref)


def _moe(plan, xs, w_gate, w_up, w_down, layer):
    d = D_MODEL
    n_tiles = xs.shape[0] // (TM_MOE * ROW_CHUNKS)
    any_spec = pl.BlockSpec(memory_space=pl.ANY)
    tile = (TM_MOE * ROW_CHUNKS, LANES)
    return pl.pallas_call(
        functools.partial(_moe_kernel, layer=layer),
        grid_spec=pltpu.PrefetchScalarGridSpec(
            num_scalar_prefetch=5, grid=(n_tiles,),
            in_specs=[pl.BlockSpec(tile, lambda g, te, na, *_: (jnp.minimum(g, na[0] - 1), 0)),
                      any_spec, any_spec, any_spec],
            out_specs=pl.BlockSpec((TM_MOE, d), lambda g, *_: (g, 0)),
            scratch_shapes=[pltpu.VMEM((2, d, EXPERT_FF), F32), pltpu.VMEM((2, d, EXPERT_FF), F32),
                            pltpu.VMEM((2, EXPERT_FF, d), F32), pltpu.SemaphoreType.DMA((2,))]),
        out_shape=jax.ShapeDtypeStruct((n_tiles * TM_MOE, d), F32),
        compiler_params=_cparams(("arbitrary",)),
        name="moe_ffn",
    )(plan["tile_expert"], plan["n_active"], plan["first"], plan["ord"], plan["next"],
      xs, w_gate, w_up, w_down)


def _moe_sample_kernel(ue_ref, nu_ref, x_ref, info_ref, y_ref, g_ref, wg_ref, wu_ref, wd_ref,
                       xo_ref, xno_ref, acc_ref):
    s = pl.program_id(0)

    @pl.when(s == 0)
    def _():
        acc_ref[...] = y_ref[...]

    @pl.when(s < nu_ref[0])
    def _():
        x = _load_row_tiles(x_ref, y_ref.shape[0])
        a = _dot3(x, wg_ref[...])
        u = _dot3(x, wu_ref[...])
        out = _dot3((a * jax.nn.sigmoid(a)) * u, wd_ref[...])
        e = ue_ref[s].astype(F32)
        w = (jnp.where(info_ref[:, 0:1] == e, info_ref[:, 4:5], 0.0)
             + jnp.where(info_ref[:, 1:2] == e, info_ref[:, 5:6], 0.0))
        acc_ref[...] += w * out

    @pl.when(s == pl.num_programs(0) - 1)
    def _():
        xo_ref[...] = acc_ref[...]
        xno_ref[...] = _rms(acc_ref[...], g_ref[...])


def _moe_sample(xn2, info, y, g, w_gate, w_up, w_down, layer, row0):
    nb, d = y.shape
    blk0 = row0 // nb
    e_s = info[row0:, 0:2].astype(jnp.int32).reshape(-1)
    used = jnp.zeros((N_EXPERTS,), jnp.int32).at[e_s].set(1)
    n_used = jnp.sum(used)
    order = jnp.argsort(1 - used, stable=True).astype(jnp.int32)
    used_list = jnp.where(jnp.arange(N_EXPERTS) < n_used, order, order[n_used - 1])
    rows = lambda s, ue, nu: (blk0, 0)
    fixed = lambda s, ue, nu: (0, 0)
    wsel = lambda s, ue, nu: (layer, ue[s], 0, 0)
    return pl.pallas_call(
        _moe_sample_kernel,
        grid_spec=pltpu.PrefetchScalarGridSpec(
            num_scalar_prefetch=2, grid=(N_EXPERTS,),
            in_specs=[pl.BlockSpec((nb * ROW_CHUNKS, LANES), rows),
                      pl.BlockSpec((nb, ROUTER_LANES), rows),
                      pl.BlockSpec((nb, d), fixed),
                      pl.BlockSpec((1, d), fixed),
                      pl.BlockSpec((None, None, d, EXPERT_FF), wsel),
                      pl.BlockSpec((None, None, d, EXPERT_FF), wsel),
                      pl.BlockSpec((None, None, EXPERT_FF, d), wsel)],
            out_specs=[pl.BlockSpec((nb, d), fixed), pl.BlockSpec((nb, d), fixed)],
            scratch_shapes=[pltpu.VMEM((nb, d), F32)]),
        out_shape=[jax.ShapeDtypeStruct((nb, d), F32), jax.ShapeDtypeStruct((nb, d), F32)],
        compiler_params=_cparams(("arbitrary",)),
        name="moe_sample",
    )(used_list, n_used.reshape(1).astype(jnp.int32), xn2, info, y, g.reshape(1, d),
      w_gate, w_up, w_down)


def _combine_kernel(pos_ref, y_ref, info_ref, g_ref, ys_hbm, *rest, rows, write_x, blk0):
    if write_x:
        xo_ref, xno_ref, buf0, buf1, sem = rest
    else:
        xno_ref, buf0, buf1, sem = rest
    i = pl.program_id(0)

    def row(src_row, buf, slot, r):
        return pltpu.make_async_copy(ys_hbm.at[pl.ds(src_row, 1)], buf.at[slot, pl.ds(r, 1)],
                                     sem.at[slot])

    def fetch(step, slot):
        base = (blk0 + step) * rows

        def body(r, c):
            t = base + r
            row(pos_ref[2 * t], buf0, slot, r).start(priority=0)
            row(pos_ref[2 * t + 1], buf1, slot, r).start(priority=1)
            return c
        lax.fori_loop(0, rows, body, 0, unroll=8)

    @pl.when(i == 0)
    def _():
        fetch(0, 0)

    @pl.when(i + 1 < pl.num_programs(0))
    def _():
        fetch(i + 1, (i + 1) % 2)

    slot = i % 2
    for r in range(rows):
        row(0, buf0, slot, r).wait()
        row(0, buf1, slot, r).wait()
    xnew = y_ref[...] + info_ref[:, 4:5] * buf0[slot] + info_ref[:, 5:6] * buf1[slot]
    if write_x:
        xo_ref[...] = xnew
    xno_ref[...] = _rms(xnew, g_ref[...]).astype(xno_ref.dtype)


def _combine(pos, y, info, g, ys, norm_dtype, write_x, row0):
    n, d = y.shape
    rows = min(TM_COMBINE, n)
    assert row0 % rows == 0
    blk0 = row0 // rows
    tile = pl.BlockSpec((rows, d), lambda i, pos: (i, 0))
    out_specs = [tile]
    out_shape = [jax.ShapeDtypeStruct((n, d), norm_dtype)]
    if write_x:
        out_specs = [tile] + out_specs
        out_shape = [jax.ShapeDtypeStruct((n, d), F32)] + out_shape
    return pl.pallas_call(
        functools.partial(_combine_kernel, rows=rows, write_x=write_x, blk0=blk0),
        grid_spec=pltpu.PrefetchScalarGridSpec(
            num_scalar_prefetch=1, grid=(n // rows,),
            in_specs=[tile,
                      pl.BlockSpec((rows, ROUTER_LANES), lambda i, pos: (blk0 + i, 0)),
                      pl.BlockSpec((1, d), lambda i, pos: (0, 0)),
                      pl.BlockSpec(memory_space=pl.ANY)],
            out_specs=out_specs,
            scratch_shapes=[pltpu.VMEM((2, rows, d), F32), pltpu.VMEM((2, rows, d), F32),
                            pltpu.SemaphoreType.DMA((2,))]),
        out_shape=out_shape,
        compiler_params=_cparams(("arbitrary",)),
        name="combine",
    )(pos, y, info, g.reshape(1, d), ys)


def _plan(info, cnt, n_tiles):
    count = cnt[0, :N_EXPERTS].astype(jnp.int32)
    tiles = (count + TM_MOE - 1) // TM_MOE
    tile_end = jnp.cumsum(tiles)
    row_off = (tile_end - tiles) * TM_MOE
    n_active = tile_end[-1]
    pos = (row_off[info[:, 0:2].astype(jnp.int32)] + info[:, 2:4].astype(jnp.int32)).reshape(-1)
    step = jnp.arange(n_tiles, dtype=jnp.int32)
    g = jnp.minimum(step, n_active - 1)
    tile_expert = jnp.sum((g[:, None] >= tile_end[None, :]).astype(jnp.int32), axis=1)
    prev = jnp.concatenate([jnp.full((1,), -1, jnp.int32), tile_expert[:-1]])
    first = ((step < n_active) & (tile_expert != prev)).astype(jnp.int32)
    ordinal = jnp.cumsum(first) - 1
    ids = jnp.arange(N_EXPERTS, dtype=jnp.int32)
    used_at_or_after = lax.cummin(jnp.where(tiles > 0, ids, N_EXPERTS), reverse=True)
    used_after = jnp.concatenate([used_at_or_after[1:], jnp.full((1,), N_EXPERTS, jnp.int32)])
    next_used = jnp.where(used_after == N_EXPERTS, -1, used_after)
    i32 = lambda a: a.astype(jnp.int32)
    return pos, dict(tile_expert=i32(tile_expert), n_active=i32(n_active.reshape(1)),
                     first=first, ord=i32(ordinal), next=i32(next_used[tile_expert]))


def kernel(x_prompt, x_sample, state_conv, cache_win_k, cache_win_v, state_pool, norm1_g, w_in,
           conv_w, attn_sinks, sgu_ln_g, sgu_ln_b, sgu_w, sgu_b, pool_w, pool_scale, w_branch, w_o,
           norm2_g, router_group_w, router_group_b, router_expert_w, router_expert_b, moe_w_gate,
           moe_w_up, moe_w_down, final_norm_g):
    batch, seq, d = x_prompt.shape
    assert seq % TB == 0 and seq >= 2 * TB and d == D_MODEL
    nb = x_sample.shape[0]
    depth = w_in.shape[0]
    n_p = batch * seq
    kv_w = N_KV * HEAD_DIM
    n_slots = 2 * (n_p + nb)
    n_tiles = (n_slots + N_EXPERTS * (TM_MOE - 1) + TM_MOE - 1) // TM_MOE

    xp = x_prompt.reshape(n_p, d)
    xs_ = x_sample.reshape(nb, d)
    xn_p = _rmsnorm(xp, norm1_g[0], BF16)
    xn_s = _rmsnorm(xs_, norm1_g[0], F32)

    pad = ROUTER_LANES - N_EXPERTS - N_EXP_GROUPS
    conv_p, k_p, v_p, pool_p = [], [], [], []
    conv_s, k_s, v_s, pool_s, chunk_s = [], [], [], [], []
    y_prompt = y_sample = None
    for l in range(depth):
        lw = dict(attn_sinks=attn_sinks[l], conv_w=conv_w[l], sgu_ln_g=sgu_ln_g[l],
                  sgu_ln_b=sgu_ln_b[l], sgu_w=sgu_w[l], sgu_b=sgu_b[l], pool_w=pool_w[l],
                  pool_scale=pool_scale[l])
        w_router = jnp.concatenate(
            [router_expert_w[l], router_group_w[l], jnp.zeros((d, pad), F32)], axis=1)
        b_router = jnp.concatenate(
            [router_expert_b[l], router_group_b[l], jnp.zeros((pad,), F32)]).reshape(1, -1)
        last = l == depth - 1
        g_next = final_norm_g if last else norm1_g[l + 1]

        zs_p = _in_proj(xn_p, w_in, l, False)
        outs_p, c_st, k_st, v_st, p_st = _branch_prompt(zs_p, batch, seq, lw)
        conv_p.append(c_st)
        k_p.append(k_st.reshape(batch, WINDOW, N_KV, HEAD_DIM))
        v_p.append(v_st.reshape(batch, WINDOW, N_KV, HEAD_DIM))
        pool_p.append(p_st)
        merged_p = _merge(xn_p, outs_p, w_in, w_branch, l, False)
        yp = _out_proj(merged_p, w_o, xp, l, False)

        zs_s = _in_proj(xn_s, w_in, l, True)
        outs_s, c_st, k_st, v_st, p_st, cv = _branch_sample(
            zs_s, lw, jnp.swapaxes(state_conv[l], 0, 1),
            cache_win_k[l].reshape(nb, WINDOW, kv_w), cache_win_v[l].reshape(nb, WINDOW, kv_w),
            jnp.swapaxes(state_pool[l], 0, 1))
        conv_s.append(jnp.swapaxes(c_st, 0, 1))
        k_s.append(k_st.reshape(nb, WINDOW, N_KV, HEAD_DIM))
        v_s.append(v_st.reshape(nb, WINDOW, N_KV, HEAD_DIM))
        pool_s.append(jnp.swapaxes(p_st, 0, 1))
        chunk_s.append(cv.reshape(nb, 1, BRANCH_W))

        xn_hi, outs_hi, x_hi = xn_s, outs_s, xs_
        if not last:
            x_t = xp.reshape(batch, seq, d)[:, seq - 2 * TB:].reshape(batch * 2 * TB, d)
            xn_t = _rmsnorm(x_t, norm1_g[l], F32)
            zs_t = _in_proj(xn_t, w_in, l, True)
            outs_t = _branch_prompt(zs_t, batch, seq, lw, hi=True, t_start=seq // TB - 2)[0]
            tail = lambda a: a.reshape(batch, 2 * TB, a.shape[-1])[:, 2 * TB - TAIL_EXACT:].reshape(
                batch * TAIL_EXACT, -1)
            outs_t = jnp.stack([tail(outs_t[b]) for b in range(N_BRANCH)])
            xn_hi = jnp.concatenate([tail(xn_t), xn_s])
            outs_hi = jnp.concatenate([outs_t, outs_s], axis=1)
            x_hi = jnp.concatenate([tail(x_t), xs_])
        y_hi = _out_proj(_merge(xn_hi, outs_hi, w_in, w_branch, l, True), w_o, x_hi, l, True)
        ys_ = y_hi[y_hi.shape[0] - nb:]
        if not last:
            yp = _put_last_rows(yp, y_hi[:batch * TAIL_EXACT], batch, seq)

        xn2, info, cnt, rows_sorted = _router(yp, ys_, norm2_g[l], w_router, b_router, n_tiles)
        pos, plan = _plan(info, cnt, n_tiles)
        rows_sorted = _dispatch(pos, xn2, rows_sorted)
        ffn = _moe(plan, rows_sorted, moe_w_gate, moe_w_up, moe_w_down, l)
        if last:
            (y_prompt,) = _combine(pos, yp, info, g_next, ffn, F32, False, 0)
            (y_sample,) = _combine(pos, ys_, info, g_next, ffn, F32, False, n_p)
        else:
            xp, xn_p = _combine(pos, yp, info, g_next, ffn, BF16, True, 0)
            xs_, xn_s = _moe_sample(xn2, info, ys_, g_next, moe_w_gate, moe_w_up, moe_w_down, l, n_p)

    return (y_prompt.reshape(batch, seq, d), y_sample.reshape(nb, 1, d),
            jnp.stack(conv_p), jnp.stack(k_p), jnp.stack(v_p), jnp.stack(pool_p),
            jnp.stack(conv_s), jnp.stack(k_s), jnp.stack(v_s), jnp.stack(pool_s),
            jnp.stack(chunk_s))
```

```python
import functools

import jax
import jax.numpy as jnp
from jax import lax
from jax.experimental import pallas as pl
from jax.experimental.pallas import tpu as pltpu

F32 = jnp.float32
BF16 = jnp.bfloat16

D_MODEL = 2048
BRANCH_W = 512
N_BRANCH = 4
CONV_W = 3
HEAD_DIM = 64
N_HEADS = 8
N_KV = 2
GQA_G = 4
WINDOW = 128
ROPE_THETA = 500000.0
ROPE_DIM = 16
CHUNK = 128
SGU_GROUPS = 4
POOL_WINDOWS = (2, 4, 8, 16)
POOL_MAX = 16
N_EXP_GROUPS = 4
EXP_PER_GROUP = 8
N_EXPERTS = 32
EXPERT_FF = 512
EPS = 1e-6
PAST_LEN = 16384
NEG = -3.0e38

A_H_OFF = 0
A_C_OFF = 512
A_B_OFF = 1024
Q_OFF = 1536
K_OFF = 2048
V_OFF = 2176
C_OFF = 2304
D_OFF = 3328
G_OFF = 3840

LANES = 128
SUBLANES = 8
VMEM_LIMIT = 52 * 1024 * 1024

TM_NORM = 512
TM_PROJ = 2048
TN_PROJ = 768
TM_MERGE = 2048
TN_MERGE = 256
TM_WO = 2048
TN_WO = 512
TM_MOE = 288
TM_COMBINE = 256
TB = 128
TAIL_EXACT = 16
ROUTER_LANES = 128


def _cparams(sem):
    return pltpu.CompilerParams(dimension_semantics=sem, vmem_limit_bytes=VMEM_LIMIT)


_NN = (((1,), (0,)), ((), ()))
_NT = (((1,), (1,)), ((), ()))


def _dot(a, b, dims=_NN):
    return lax.dot_general(a, b, dims, preferred_element_type=F32)


def _split(x):
    hi = x.astype(BF16)
    lo = (x - hi.astype(F32)).astype(BF16)
    return hi, lo


def _dot3(a, b, dims=_NN):
    ah, al = _split(a)
    bh, bl = _split(b)
    return _dot(ah, bh, dims) + _dot(al, bh, dims) + _dot(ah, bl, dims)


def _mm(a, w, hi, dims=_NN):
    if hi:
        return _dot3(a.astype(F32), w.astype(F32), dims)
    return _dot(a.astype(BF16), w.astype(BF16), dims)


def _rms(x, g):
    return (x * lax.rsqrt(jnp.mean(x * x, axis=-1, keepdims=True) + EPS)) * g


def _rmsnorm_kernel(x_ref, g_ref, o_ref):
    o_ref[...] = _rms(x_ref[...], g_ref[...]).astype(o_ref.dtype)


def _rmsnorm(x, g, out_dtype):
    n, d = x.shape
    tm = min(TM_NORM, n)
    return pl.pallas_call(
        _rmsnorm_kernel,
        grid=(n // tm,),
        in_specs=[pl.BlockSpec((tm, d), lambda i: (i, 0)),
                  pl.BlockSpec((1, d), lambda i: (0, 0))],
        out_specs=pl.BlockSpec((tm, d), lambda i: (i, 0)),
        out_shape=jax.ShapeDtypeStruct((n, d), out_dtype),
        compiler_params=_cparams(("parallel",)),
        name="rmsnorm",
    )(x, g.reshape(1, d))


def _proj_kernel(x_ref, w_ref, o_ref, *, hi):
    o_ref[...] = _mm(x_ref[...], w_ref[...], hi)


def _in_proj(xn, w_in, layer, hi):
    n, d = xn.shape
    tm = min(TM_NORM if hi else TM_PROJ, n)
    return pl.pallas_call(
        functools.partial(_proj_kernel, hi=hi),
        grid=(n // tm, G_OFF // TN_PROJ),
        in_specs=[pl.BlockSpec((tm, d), lambda i, j: (i, 0)),
                  pl.BlockSpec((None, d, TN_PROJ), lambda i, j: (layer, 0, j))],
        out_specs=pl.BlockSpec((tm, TN_PROJ), lambda i, j: (i, j)),
        out_shape=jax.ShapeDtypeStruct((n, G_OFF), F32),
        compiler_params=_cparams(("parallel", "arbitrary")),
        name="in_proj",
    )(xn, w_in)


def _rope_tables(pos):
    half = ROPE_DIM // 2
    inv = jnp.power(jnp.float32(ROPE_THETA), -jnp.arange(half, dtype=F32) * (2.0 / ROPE_DIM))
    ang = pos.astype(F32)[:, None] * inv[None, :]
    cos, sin = jnp.cos(ang), jnp.sin(ang)
    t = pos.shape[0]
    rest = HEAD_DIM - ROPE_DIM
    c = jnp.concatenate([cos, cos, jnp.ones((t, rest), F32)], axis=1)
    s1 = jnp.concatenate([-sin, jnp.zeros((t, HEAD_DIM - half), F32)], axis=1)
    s2 = jnp.concatenate([jnp.zeros((t, half), F32), sin, jnp.zeros((t, rest), F32)], axis=1)
    rep = LANES // HEAD_DIM
    return jnp.tile(c, (1, rep)), jnp.tile(s1, (1, rep)), jnp.tile(s2, (1, rep))


def _rope(x, c, s1, s2):
    half = ROPE_DIM // 2
    return x * c + pltpu.roll(x, LANES - half, 1) * s1 + pltpu.roll(x, half, 1) * s2


def _gelu(x):
    return 0.5 * x * (1.0 + lax.erf(x * 0.7071067811865476))


def _layernorm(v, g, b):
    mu = jnp.mean(v, axis=-1, keepdims=True)
    vc = v - mu
    return vc * lax.rsqrt(jnp.mean(vc * vc, axis=-1, keepdims=True) + EPS) * g + b


def _branch_prompt_kernel(sink_ref, z_ref, rc_ref, rs1_ref, rs2_ref, cw_ref, lng_ref, lnb_ref,
                          sw_ref, sbt_ref, pw_ref, ps_ref,
                          outs_ref, conv_ref, kst_ref, vst_ref, pool_ref,
                          kprev, vprev, ci_ext, p_ext, *, hi, t_start):
    tb = pl.program_id(1)
    halo_c = SUBLANES
    halo_p = POOL_MAX

    @pl.when(tb == 0)
    def _():
        kprev[...] = jnp.zeros_like(kprev)
        vprev[...] = jnp.zeros_like(vprev)
        ci_ext[0:halo_c, :] = jnp.zeros((halo_c, BRANCH_W), F32)
        p_ext[0:halo_p, :] = jnp.zeros((halo_p, BRANCH_W), F32)

    ci = z_ref[:, A_C_OFF:A_B_OFF] * z_ref[:, A_H_OFF:A_C_OFF]
    ci_ext[halo_c:halo_c + TB, :] = ci
    y = cw_ref[2:3, :] * ci
    for j in range(CONV_W - 1):
        y = y + cw_ref[j:j + 1, :] * ci_ext[pl.ds(halo_c - (CONV_W - 1) + j, TB), :]
    outs_ref[0] = (z_ref[:, A_B_OFF:Q_OFF] * y).astype(outs_ref.dtype)
    conv_ref[...] = ci_ext[pl.ds(halo_c + TB - (CONV_W - 1), CONV_W - 1), :]
    ci_ext[0:halo_c, :] = ci[TB - halo_c:, :]

    p = z_ref[:, D_OFF:G_OFF]
    p_ext[halo_p:halo_p + TB, :] = p
    posf = ((t_start + tb) * TB + lax.broadcasted_iota(jnp.int32, (TB, 1), 0)).astype(F32)
    for g, w in enumerate(POOL_WINDOWS):
        sl = slice(g * LANES, (g + 1) * LANES)
        s = p[:, sl]
        for k in range(1, w):
            s = s + p_ext[pl.ds(halo_p - k, TB), sl]
        dlt = s / jnp.minimum(posf + 1.0, float(w)) - p[:, sl]
        yg = _mm(dlt, pw_ref[g], hi) * ps_ref[:, sl]
        outs_ref[3, :, sl] = yg.astype(outs_ref.dtype)
    pool_ref[...] = p_ext[pl.ds(halo_p + TB - (POOL_MAX - 1), POOL_MAX - 1), :]
    p_ext[0:halo_p, :] = p[TB - halo_p:, :]

    ge = _gelu(z_ref[:, C_OFF:D_OFF])
    u = ge[:, :BRANCH_W]
    vn = _layernorm(ge[:, BRANCH_W:], lng_ref[...], lnb_ref[...])
    r_i = lax.broadcasted_iota(jnp.int32, (TB, TB), 0)
    c_i = lax.broadcasted_iota(jnp.int32, (TB, TB), 1)
    for g in range(SGU_GROUPS):
        sl = slice(g * LANES, (g + 1) * LANES)
        wt = jnp.where(c_i <= r_i, sw_ref[g], 0.0)
        f = _mm(wt, vn[:, sl], hi) + sbt_ref[:, g:g + 1]
        outs_ref[2, :, sl] = (u[:, sl] * f).astype(outs_ref.dtype)

    rc, rs1, rs2 = rc_ref[...], rs1_ref[...], rs2_ref[...]
    kr = _rope(z_ref[:, K_OFF:V_OFF], rc, rs1, rs2)
    v = z_ref[:, V_OFF:C_OFF]
    qr = [_rope(z_ref[:, Q_OFF + c * LANES:Q_OFF + (c + 1) * LANES], rc, rs1, rs2)
          for c in range(N_HEADS * HEAD_DIM // LANES)]
    qi = lax.broadcasted_iota(jnp.int32, (TB, 2 * TB), 0)
    kj = lax.broadcasted_iota(jnp.int32, (TB, 2 * TB), 1)
    allowed = (kj > qi) & (kj <= qi + WINDOW) & ((kj >= TB) | (tb > 0))
    kp, vp = kprev[...], vprev[...]
    heads = [None] * N_HEADS
    for hk in range(N_KV):
        hs = slice(hk * HEAD_DIM, (hk + 1) * HEAD_DIM)
        kk = jnp.concatenate([kp[:, hs], kr[:, hs]], axis=0)
        vv = jnp.concatenate([vp[:, hs], v[:, hs]], axis=0)
        if not hi:
            kk, vv = kk.astype(BF16), vv.astype(BF16)
        for g in range(GQA_G):
            hq = hk * GQA_G + g
            per = LANES // HEAD_DIM
            qh = qr[hq // per][:, (hq % per) * HEAD_DIM:(hq % per + 1) * HEAD_DIM]
            s = _mm(qh, kk, hi, _NT) * (HEAD_DIM ** -0.5)
            s = jnp.where(allowed, s, -1e30)
            sink = sink_ref[hq]
            m = jnp.maximum(jnp.max(s, axis=-1, keepdims=True), sink)
            e = jnp.exp(s - m)
            den = jnp.sum(e, axis=-1, keepdims=True) + jnp.exp(sink - m)
            heads[hq] = _mm(e / den, vv, hi)
    outs_ref[1] = jnp.concatenate(heads, axis=1).astype(outs_ref.dtype)
    kst_ref[...] = kr
    vst_ref[...] = v
    kprev[...] = kr
    vprev[...] = v


def _branch_prompt(zs, batch, seq, lw, hi=False, t_start=0):
    n = zs.shape[0]
    nt = n // (batch * TB)
    rc, rs1, rs2 = _rope_tables(jnp.arange(seq))
    full = lambda shape: pl.BlockSpec(shape, lambda b, t: (0,) * len(shape))
    tab = pl.BlockSpec((TB, LANES), lambda b, t: (t_start + t, 0))
    kv_w = N_KV * HEAD_DIM
    return pl.pallas_call(
        functools.partial(_branch_prompt_kernel, hi=hi, t_start=t_start),
        grid=(batch, nt),
        in_specs=[pl.BlockSpec(memory_space=pltpu.SMEM),
                  pl.BlockSpec((TB, G_OFF), lambda b, t: (b * nt + t, 0)),
                  tab, tab, tab,
                  full((CONV_W, BRANCH_W)), full((1, BRANCH_W)), full((1, BRANCH_W)),
                  full((SGU_GROUPS, CHUNK, CHUNK)), full((CHUNK, SGU_GROUPS)),
                  full((len(POOL_WINDOWS), LANES, LANES)), full((1, BRANCH_W))],
        out_specs=[pl.BlockSpec((N_BRANCH, TB, BRANCH_W), lambda b, t: (0, b * nt + t, 0)),
                   pl.BlockSpec((None, CONV_W - 1, BRANCH_W), lambda b, t: (b, 0, 0)),
                   pl.BlockSpec((None, WINDOW, kv_w), lambda b, t: (b, 0, 0)),
                   pl.BlockSpec((None, WINDOW, kv_w), lambda b, t: (b, 0, 0)),
                   pl.BlockSpec((None, POOL_MAX - 1, BRANCH_W), lambda b, t: (b, 0, 0))],
        out_shape=[jax.ShapeDtypeStruct((N_BRANCH, n, BRANCH_W), F32 if hi else BF16),
                   jax.ShapeDtypeStruct((batch, CONV_W - 1, BRANCH_W), F32),
                   jax.ShapeDtypeStruct((batch, WINDOW, kv_w), F32),
                   jax.ShapeDtypeStruct((batch, WINDOW, kv_w), F32),
                   jax.ShapeDtypeStruct((batch, POOL_MAX - 1, BRANCH_W), F32)],
        scratch_shapes=[pltpu.VMEM((TB, kv_w), F32), pltpu.VMEM((TB, kv_w), F32),
                        pltpu.VMEM((SUBLANES + TB, BRANCH_W), F32),
                        pltpu.VMEM((POOL_MAX + TB, BRANCH_W), F32)],
        compiler_params=_cparams(("arbitrary", "arbitrary")),
        name="branch_prompt",
    )(lw["attn_sinks"], zs, rc, rs1, rs2, lw["conv_w"], lw["sgu_ln_g"].reshape(1, -1),
      lw["sgu_ln_b"].reshape(1, -1), lw["sgu_w"], lw["sgu_b"].T, lw["pool_w"],
      lw["pool_scale"].reshape(1, -1))


def _branch_sample_kernel(sink_ref, z_ref, rc_ref, rs1_ref, rs2_ref, cw_ref, lng_ref, lnb_ref,
                          ssc_ref, ssh_ref, pw_ref, ps_ref, cpast_ref, kc_ref, vc_ref, ppast_ref,
                          outs_ref, conv_ref, kst_ref, vst_ref, pool_ref, chunkv_ref,
                          qrot, krot, vnew):
    nb = z_ref.shape[0]

    ci = z_ref[:, A_C_OFF:A_B_OFF] * z_ref[:, A_H_OFF:A_C_OFF]
    y = cw_ref[CONV_W - 1:CONV_W, :] * ci
    for j in range(CONV_W - 1):
        y = y + cw_ref[j:j + 1, :] * cpast_ref[j]
    outs_ref[0] = z_ref[:, A_B_OFF:Q_OFF] * y
    for j in range(CONV_W - 2):
        conv_ref[j] = cpast_ref[j + 1]
    conv_ref[CONV_W - 2] = ci

    p = z_ref[:, D_OFF:G_OFF]
    for g, w in enumerate(POOL_WINDOWS):
        sl = slice(g * LANES, (g + 1) * LANES)
        s = p[:, sl]
        for k in range(1, w):
            s = s + ppast_ref[POOL_MAX - 1 - k, :, sl]
        dlt = s / float(min(PAST_LEN + 1, w)) - p[:, sl]
        outs_ref[3, :, sl] = _dot3(dlt, pw_ref[g]) * ps_ref[:, sl]
    for j in range(POOL_MAX - 2):
        pool_ref[j] = ppast_ref[j + 1]
    pool_ref[POOL_MAX - 2] = p

    ge = _gelu(z_ref[:, C_OFF:D_OFF])
    vn = _layernorm(ge[:, BRANCH_W:], lng_ref[...], lnb_ref[...])
    outs_ref[2] = ge[:, :BRANCH_W] * (vn * ssc_ref[...] + ssh_ref[...])
    chunkv_ref[...] = vn

    rc, rs1, rs2 = rc_ref[...], rs1_ref[...], rs2_ref[...]
    krot[...] = _rope(z_ref[:, K_OFF:V_OFF], rc, rs1, rs2)
    vnew[...] = z_ref[:, V_OFF:C_OFF]
    for c in range(N_HEADS * HEAD_DIM // LANES):
        cs = slice(c * LANES, (c + 1) * LANES)
        qrot[:, cs] = _rope(z_ref[:, Q_OFF + c * LANES:Q_OFF + (c + 1) * LANES], rc, rs1, rs2)
    row = lax.broadcasted_iota(jnp.int32, (WINDOW, LANES), 0)
    lo = lax.broadcasted_iota(jnp.int32, (WINDOW, LANES), 1) < HEAD_DIM
    lo1 = lax.broadcasted_iota(jnp.int32, (1, LANES), 1) < HEAD_DIM
    scale = HEAD_DIM ** -0.5

    def body(b, carry):
        kn = jnp.where(row == WINDOW - 1, krot[pl.ds(b, 1), :], pltpu.roll(kc_ref[b], WINDOW - 1, 0))
        vn_b = jnp.where(row == WINDOW - 1, vnew[pl.ds(b, 1), :], pltpu.roll(vc_ref[b], WINDOW - 1, 0))
        kst_ref[b] = kn
        vst_ref[b] = vn_b
        qb = qrot[pl.ds(b, 1), :]
        o = []
        for g in range(GQA_G):
            ca = qb[:, (g // 2) * LANES:(g // 2 + 1) * LANES]
            cb = qb[:, (2 + g // 2) * LANES:(3 + g // 2) * LANES]
            if g % 2 == 0:
                qrow = jnp.where(lo1, ca, pltpu.roll(cb, HEAD_DIM, 1))
            else:
                qrow = jnp.where(lo1, pltpu.roll(ca, HEAD_DIM, 1), cb)
            prod = kn * qrow
            pn = []
            for part, hq in ((jnp.where(lo, prod, 0.0), g), (jnp.where(lo, 0.0, prod), GQA_G + g)):
                s = jnp.sum(part, axis=1, keepdims=True) * scale
                sink = sink_ref[hq]
                m = jnp.maximum(jnp.max(s, axis=0, keepdims=True), sink)
                e = jnp.exp(s - m)
                den = jnp.sum(e, axis=0, keepdims=True) + jnp.exp(sink - m)
                pn.append(e / den)
            o.append(jnp.sum(jnp.where(lo, pn[0], pn[1]) * vn_b, axis=0, keepdims=True))
        sw = lambda x: pltpu.roll(x, HEAD_DIM, 1)
        chunks = [jnp.where(lo1, o[0], sw(o[1])), jnp.where(lo1, o[2], sw(o[3])),
                  jnp.where(lo1, sw(o[0]), o[1]), jnp.where(lo1, sw(o[2]), o[3])]
        outs_ref[1, pl.ds(b, 1), :] = jnp.concatenate(chunks, axis=1)
        return carry

    lax.fori_loop(0, nb, body, 0)


def _branch_sample(zs, lw, cpast, kc, vc, ppast):
    nb = zs.shape[0]
    kv_w = N_KV * HEAD_DIM
    rc, rs1, rs2 = _rope_tables(jnp.full((1,), PAST_LEN, jnp.int32))
    ssc = jnp.repeat(lw["sgu_w"][:, 0, 0], CHUNK).reshape(1, BRANCH_W)
    ssh = jnp.repeat(lw["sgu_b"][:, 0], CHUNK).reshape(1, BRANCH_W)
    vspec = pl.BlockSpec(memory_space=pltpu.VMEM)
    return pl.pallas_call(
        _branch_sample_kernel,
        in_specs=[pl.BlockSpec(memory_space=pltpu.SMEM)] + [vspec] * 15,
        out_specs=[vspec] * 6,
        out_shape=[jax.ShapeDtypeStruct((N_BRANCH, nb, BRANCH_W), F32),
                   jax.ShapeDtypeStruct((CONV_W - 1, nb, BRANCH_W), F32),
                   jax.ShapeDtypeStruct((nb, WINDOW, kv_w), F32),
                   jax.ShapeDtypeStruct((nb, WINDOW, kv_w), F32),
                   jax.ShapeDtypeStruct((POOL_MAX - 1, nb, BRANCH_W), F32),
                   jax.ShapeDtypeStruct((nb, BRANCH_W), F32)],
        scratch_shapes=[pltpu.VMEM((nb, N_HEADS * HEAD_DIM), F32), pltpu.VMEM((nb, kv_w), F32),
                        pltpu.VMEM((nb, kv_w), F32)],
        compiler_params=pltpu.CompilerParams(vmem_limit_bytes=VMEM_LIMIT),
        name="branch_sample",
    )(lw["attn_sinks"], zs, rc, rs1, rs2, lw["conv_w"], lw["sgu_ln_g"].reshape(1, -1),
      lw["sgu_ln_b"].reshape(1, -1), ssc, ssh, lw["pool_w"], lw["pool_scale"].reshape(1, -1),
      cpast, kc, vc, ppast)


def _merge_kernel(xn_ref, outs_ref, wg0_ref, wg1_ref, wb_ref, o_ref, acc_ref, *, hi):
    nbr = pl.program_id(2)

    @pl.when(nbr == 0)
    def _():
        acc_ref[...] = jnp.zeros_like(acc_ref)

    xn, outs = xn_ref[...], outs_ref[...]
    for h, wg_ref in enumerate((wg0_ref, wg1_ref)):
        cols = slice(h * TN_MERGE, (h + 1) * TN_MERGE)
        gate = jax.nn.sigmoid(_mm(xn, wg_ref[...], hi))
        acc_ref[:, cols] += gate * _mm(outs, wb_ref[:, cols], hi)

    @pl.when(nbr == N_BRANCH - 1)
    def _():
        o_ref[...] = acc_ref[...].astype(o_ref.dtype)


def _merge(xn, outs, w_in, w_branch, layer, hi):
    n, d = xn.shape
    tm = min(TM_MERGE, n)
    g_blk = G_OFF // TN_MERGE
    per_branch = D_MODEL // TN_MERGE
    wide = 2 * TN_MERGE
    gate_block = lambda h: pl.BlockSpec(
        (None, d, TN_MERGE), lambda i, j, b: (layer, 0, g_blk + b * per_branch + 2 * j + h))
    return pl.pallas_call(
        functools.partial(_merge_kernel, hi=hi),
        grid=(n // tm, D_MODEL // wide, N_BRANCH),
        in_specs=[pl.BlockSpec((tm, d), lambda i, j, b: (i, 0)),
                  pl.BlockSpec((None, tm, BRANCH_W), lambda i, j, b: (b, i, 0)),
                  gate_block(0), gate_block(1),
                  pl.BlockSpec((None, None, BRANCH_W, wide), lambda i, j, b: (layer, b, 0, j))],
        out_specs=pl.BlockSpec((tm, wide), lambda i, j, b: (i, j)),
        out_shape=jax.ShapeDtypeStruct((n, D_MODEL), F32 if hi else BF16),
        scratch_shapes=[pltpu.VMEM((tm, wide), F32)],
        compiler_params=_cparams(("parallel", "arbitrary", "arbitrary")),
        name="merge",
    )(xn, outs, w_in, w_in, w_branch)


def _wo_kernel(m_ref, w_ref, x_ref, o_ref, *, hi):
    o_ref[...] = x_ref[...] + _mm(m_ref[...], w_ref[...], hi)


def _out_proj(merged, w_o, x, layer, hi):
    n, d = x.shape
    tm = min(TM_WO, n)
    return pl.pallas_call(
        functools.partial(_wo_kernel, hi=hi),
        grid=(n // tm, d // TN_WO),
        in_specs=[pl.BlockSpec((tm, d), lambda i, j: (i, 0)),
                  pl.BlockSpec((None, d, TN_WO), lambda i, j: (layer, 0, j)),
                  pl.BlockSpec((tm, TN_WO), lambda i, j: (i, j))],
        out_specs=pl.BlockSpec((tm, TN_WO), lambda i, j: (i, j)),
        out_shape=jax.ShapeDtypeStruct((n, d), F32),
        compiler_params=_cparams(("parallel", "arbitrary")),
        name="out_proj",
    )(merged, w_o, x)


def _put_rows_kernel(y_hbm, t_ref, o_ref):
    del y_hbm
    o_ref[...] = t_ref[...]


def _put_last_rows(y, y_tail, batch, seq):
    d = y.shape[1]
    nt = seq // TAIL_EXACT
    return pl.pallas_call(
        _put_rows_kernel,
        grid=(batch,),
        in_specs=[pl.BlockSpec(memory_space=pl.ANY),
                  pl.BlockSpec((TAIL_EXACT, d), lambda b: (b, 0))],
        out_specs=pl.BlockSpec((TAIL_EXACT, d), lambda b: (b * nt + nt - 1, 0)),
        out_shape=jax.ShapeDtypeStruct(y.shape, y.dtype),
        input_output_aliases={0: 0},
        compiler_params=_cparams(("arbitrary",)),
        name="put_last_rows",
    )(y, y_tail)


def _route_rows(y, g_ref, wr_ref, br_ref, carry_ref):
    tm = y.shape[0]
    xn = _rms(y, g_ref[...])
    logits = _dot3(xn, wr_ref[...]) + br_ref[...]
    lane_i = lax.broadcasted_iota(jnp.int32, (tm, ROUTER_LANES), 1)
    lane = lane_i.astype(F32)
    lane_grp = lax.shift_right_logical(lane_i, 3).astype(F32)
    big = float(ROUTER_LANES)
    is_grp = (lane_i >= N_EXPERTS) & (lane_i < N_EXPERTS + N_EXP_GROUPS)
    lg = jnp.where(is_grp, logits, NEG)
    gmax = jnp.max(lg, axis=-1, keepdims=True)
    gsel = jnp.min(jnp.where(lg == gmax, lane - N_EXPERTS, big), axis=-1, keepdims=True)
    wg = 1.0 / jnp.sum(jnp.where(is_grp, jnp.exp(logits - gmax), 0.0), axis=-1, keepdims=True)
    in_grp = (lane_i < N_EXPERTS) & (lane_grp == gsel)
    v1 = jnp.where(in_grp, logits, NEG)
    m1 = jnp.max(v1, axis=-1, keepdims=True)
    i1 = jnp.min(jnp.where(v1 == m1, lane, big), axis=-1, keepdims=True)
    v2 = jnp.where(lane == i1, NEG, v1)
    m2 = jnp.max(v2, axis=-1, keepdims=True)
    i2 = jnp.min(jnp.where(v2 == m2, lane, big), axis=-1, keepdims=True)
    e2 = jnp.exp(m2 - m1)
    w1 = wg / (1.0 + e2)
    w2 = wg * e2 / (1.0 + e2)

    sel1 = lane == i1
    sel2 = lane == i2
    onehot = jnp.where(sel1 | sel2, 1.0, 0.0)
    r_i = lax.broadcasted_iota(jnp.int32, (tm, tm), 0)
    c_i = lax.broadcasted_iota(jnp.int32, (tm, tm), 1)
    before = jnp.where(c_i < r_i, 1.0, 0.0).astype(BF16)
    counts = _dot(before, onehot.astype(BF16)) + carry_ref[0:1, :]
    r1 = jnp.sum(jnp.where(sel1, counts, 0.0), axis=-1, keepdims=True)
    r2 = jnp.sum(jnp.where(sel2, counts, 0.0), axis=-1, keepdims=True)
    carry_ref[...] = carry_ref[...] + jnp.sum(onehot, axis=0, keepdims=True)

    info = jnp.zeros((tm, ROUTER_LANES), F32)
    for k, val in enumerate((i1, i2, r1, r2, w1, w2)):
        info = jnp.where(lane_i == k, val, info)
    return xn, info


def _router_kernel(yp_ref, ys_ref, g_ref, wr_ref, br_ref, xn_ref, info_ref, idx_ref, cnt_ref, zero_hbm,
                   carry_ref, zbuf, zsem, *, prompt_tiles, sorted_tiles, zero_per_step):
    i = pl.program_id(0)
    tile_lines = TM_MOE * ROW_CHUNKS

    @pl.when(i == 0)
    def _():
        carry_ref[...] = jnp.zeros_like(carry_ref)
        zbuf[...] = jnp.zeros_like(zbuf)

    def zero_tile(j):
        k = i * zero_per_step + j
        dst = zero_hbm.at[pl.ds(pl.multiple_of(k * tile_lines, tile_lines), tile_lines)]
        return k, pltpu.make_async_copy(zbuf, dst, zsem)

    for j in range(zero_per_step):
        k, cp = zero_tile(j)

        @pl.when(k < sorted_tiles)
        def _():
            cp.start()

    @pl.when(i < prompt_tiles)
    def _():
        xn, info = _route_rows(yp_ref[...], g_ref, wr_ref, br_ref, carry_ref)
        _store_row_tiles(xn_ref, xn)
        info_ref[...] = info
        idx_ref[...] = info.T[0:SUBLANES, :]

    @pl.when(i == prompt_tiles)
    def _():
        nb = ys_ref.shape[0]
        xn, info = _route_rows(ys_ref[...], g_ref, wr_ref, br_ref, carry_ref)
        _store_row_tiles(xn_ref, xn)
        info_ref[0:nb, :] = info
        idx_ref[:, 0:nb] = info.T[0:SUBLANES, :]

    cnt_ref[...] = carry_ref[...]

    for j in range(zero_per_step):
        k, cp = zero_tile(j)

        @pl.when(k < sorted_tiles)
        def _():
            cp.wait()


def _router(yp, ys, g, w_router, b_router, sorted_tiles):
    n_p, d = yp.shape
    nb = ys.shape[0]
    tm = min(TM_NORM, n_p)
    prompt_tiles = n_p // tm
    n = n_p + nb
    tile_lines = TM_MOE * ROW_CHUNKS
    return pl.pallas_call(
        functools.partial(_router_kernel, prompt_tiles=prompt_tiles, sorted_tiles=sorted_tiles,
                          zero_per_step=pl.cdiv(sorted_tiles, prompt_tiles + 1)),
        grid=(prompt_tiles + 1,),
        in_specs=[pl.BlockSpec((tm, d), lambda i: (jnp.minimum(i, prompt_tiles - 1), 0)),
                  pl.BlockSpec((nb, d), lambda i: (0, 0)),
                  pl.BlockSpec((1, d), lambda i: (0, 0)),
                  pl.BlockSpec((d, ROUTER_LANES), lambda i: (0, 0)),
                  pl.BlockSpec((1, ROUTER_LANES), lambda i: (0, 0))],
        out_specs=[pl.BlockSpec((tm * ROW_CHUNKS, LANES), lambda i: (i, 0)),
                   pl.BlockSpec((tm, ROUTER_LANES), lambda i: (i, 0)),
                   pl.BlockSpec((SUBLANES, tm), lambda i: (0, i)),
                   pl.BlockSpec((SUBLANES, ROUTER_LANES), lambda i: (0, 0)),
                   pl.BlockSpec(memory_space=pl.ANY)],
        out_shape=[jax.ShapeDtypeStruct((n * ROW_CHUNKS, LANES), F32),
                   jax.ShapeDtypeStruct((n, ROUTER_LANES), F32),
                   jax.ShapeDtypeStruct((SUBLANES, n), F32),
                   jax.ShapeDtypeStruct((SUBLANES, ROUTER_LANES), F32),
                   jax.ShapeDtypeStruct((sorted_tiles * tile_lines, LANES), F32)],
        scratch_shapes=[pltpu.VMEM((SUBLANES, ROUTER_LANES), F32),
                        pltpu.VMEM((tile_lines, LANES), F32), pltpu.SemaphoreType.DMA(())],
        compiler_params=_cparams(("arbitrary",)),
        name="router",
    )(yp, ys, g.reshape(1, d), w_router, b_router)


ROW_CHUNKS = D_MODEL // LANES


def _store_row_tiles(ref, x):
    for c in range(ROW_CHUNKS):
        ref[pl.ds(c, x.shape[0], stride=ROW_CHUNKS), :] = x[:, c * LANES:(c + 1) * LANES]


def _load_row_tiles(ref, rows):
    return jnp.concatenate(
        [ref[pl.ds(c, rows, stride=ROW_CHUNKS), :] for c in range(ROW_CHUNKS)], axis=1)


def _row_copy(src, i, dst, j, sem):
    def lines(k):
        start = k * ROW_CHUNKS
        if not isinstance(k, int):
            start = pl.multiple_of(start, ROW_CHUNKS)
        return pl.ds(start, ROW_CHUNKS)
    return pltpu.make_async_copy(src.at[lines(i)], dst.at[lines(j)], sem)


def _dispatch_kernel(pos_ref, x_ref, xs_in, xs_out, sem, *, rows, n_tokens):
    del xs_in
    i = pl.program_id(0)
    base = i * rows
    full_steps, tail = divmod(n_tokens, rows)

    def copies(r):
        t = base + r
        return (_row_copy(x_ref, r, xs_out, pos_ref[t], sem),
                _row_copy(x_ref, r, xs_out, pos_ref[n_tokens + t], sem))

    def scatter(count):
        def start(r, c):
            for queue, cp in enumerate(copies(r)):
                cp.start(priority=queue)
            return c
        lax.fori_loop(0, count, start, 0, unroll=8)
        for _ in range(2 * count):
            _row_copy(x_ref, 0, xs_out, 0, sem).wait()

    @pl.when(i < full_steps)
    def _():
        scatter(rows)

    if tail:
        @pl.when(i == full_steps)
        def _():
            scatter(tail)


def _dispatch(pos, xn, xs_zero):
    n_tokens = xn.shape[0] // ROW_CHUNKS
    rows = TM_NORM
    any_spec = pl.BlockSpec(memory_space=pl.ANY)
    return pl.pallas_call(
        functools.partial(_dispatch_kernel, rows=rows, n_tokens=n_tokens),
        grid_spec=pltpu.PrefetchScalarGridSpec(
            num_scalar_prefetch=1, grid=(pl.cdiv(n_tokens, rows),),
            in_specs=[pl.BlockSpec((rows * ROW_CHUNKS, LANES), lambda i, pos: (i, 0)), any_spec],
            out_specs=any_spec,
            scratch_shapes=[pltpu.SemaphoreType.DMA(())]),
        out_shape=jax.ShapeDtypeStruct(xs_zero.shape, xs_zero.dtype),
        input_output_aliases={2: 0},
        compiler_params=_cparams(("arbitrary",)),
        name="dispatch",
    )(pos, xn, xs_zero)


def _moe_kernel(te_ref, na_ref, first_ref, ord_ref, next_ref, x_ref, wg_hbm, wu_hbm, wd_hbm,
                ys_ref, wg_buf, wu_buf, wd_buf, wsem, *, layer):
    g = pl.program_id(0)
    n_act = na_ref[0]

    def weight_copies(e, slot):
        return [pltpu.make_async_copy(hbm.at[layer, e], vm.at[slot], wsem.at[slot])
                for hbm, vm in ((wg_hbm, wg_buf), (wu_hbm, wu_buf), (wd_hbm, wd_buf))]

    @pl.when(g == 0)
    def _():
        for cp in weight_copies(te_ref[0], 0):
            cp.start()

    @pl.when(g < n_act)
    def _():
        wslot = ord_ref[g] % 2

        @pl.when(first_ref[g] == 1)
        def _():
            @pl.when(next_ref[g] >= 0)
            def _():
                for cp in weight_copies(next_ref[g], 1 - wslot):
                    cp.start()
            for cp in weight_copies(te_ref[g], wslot):
                cp.wait()

        x = _load_row_tiles(x_ref, TM_MOE).astype(BF16)
        a = _dot(x, wg_buf[wslot].astype(BF16))
        u = _dot(x, wu_buf[wslot].astype(BF16))
        h = (a * jax.nn.sigmoid(a)) * u
        ys_ref[...] = _dot(h.astype(BF16), wd_buf[wslot].astype(BF16))

    @pl.when(g >= n_act)
    def _():
        ys_ref[...] = jnp.zeros_like(ys_ref)


def _moe(plan, xs, w_gate, w_up, w_down, layer):
    d = D_MODEL
    n_tiles = xs.shape[0] // (TM_MOE * ROW_CHUNKS)
    any_spec = pl.BlockSpec(memory_space=pl.ANY)
    tile = (TM_MOE * ROW_CHUNKS, LANES)
    return pl.pallas_call(
        functools.partial(_moe_kernel, layer=layer),
        grid_spec=pltpu.PrefetchScalarGridSpec(
            num_scalar_prefetch=5, grid=(n_tiles,),
            in_specs=[pl.BlockSpec(tile, lambda g, te, na, *_: (jnp.minimum(g, na[0] - 1), 0)),
                      any_spec, any_spec, any_spec],
            out_specs=pl.BlockSpec((TM_MOE, d), lambda g, *_: (g, 0)),
            scratch_shapes=[pltpu.VMEM((2, d, EXPERT_FF), F32), pltpu.VMEM((2, d, EXPERT_FF), F32),
                            pltpu.VMEM((2, EXPERT_FF, d), F32), pltpu.SemaphoreType.DMA((2,))]),
        out_shape=jax.ShapeDtypeStruct((n_tiles * TM_MOE, d), F32),
        compiler_params=_cparams(("arbitrary",)),
        name="moe_ffn",
    )(plan["tile_expert"], plan["n_active"], plan["first"], plan["ord"], plan["next"],
      xs, w_gate, w_up, w_down)


def _moe_sample_kernel(ue_ref, nu_ref, x_ref, info_ref, y_ref, g_ref, wg_ref, wu_ref, wd_ref,
                       xo_ref, xno_ref, acc_ref):
    s = pl.program_id(0)

    @pl.when(s == 0)
    def _():
        acc_ref[...] = y_ref[...]

    @pl.when(s < nu_ref[0])
    def _():
        x = _load_row_tiles(x_ref, y_ref.shape[0])
        a = _dot3(x, wg_ref[...])
        u = _dot3(x, wu_ref[...])
        out = _dot3((a * jax.nn.sigmoid(a)) * u, wd_ref[...])
        e = ue_ref[s].astype(F32)
        w = (jnp.where(info_ref[:, 0:1] == e, info_ref[:, 4:5], 0.0)
             + jnp.where(info_ref[:, 1:2] == e, info_ref[:, 5:6], 0.0))
        acc_ref[...] += w * out

    @pl.when(s == pl.num_programs(0) - 1)
    def _():
        xo_ref[...] = acc_ref[...]
        xno_ref[...] = _rms(acc_ref[...], g_ref[...])


def _moe_sample(xn2, info, y, g, w_gate, w_up, w_down, layer, row0):
    nb, d = y.shape
    blk0 = row0 // nb
    e_s = info[row0:, 0:2].astype(jnp.int32).reshape(-1)
    used = jnp.zeros((N_EXPERTS,), jnp.int32).at[e_s].set(1)
    n_used = jnp.sum(used)
    order = jnp.argsort(1 - used, stable=True).astype(jnp.int32)
    used_list = jnp.where(jnp.arange(N_EXPERTS) < n_used, order, order[n_used - 1])
    rows = lambda s, ue, nu: (blk0, 0)
    fixed = lambda s, ue, nu: (0, 0)
    wsel = lambda s, ue, nu: (layer, ue[s], 0, 0)
    return pl.pallas_call(
        _moe_sample_kernel,
        grid_spec=pltpu.PrefetchScalarGridSpec(
            num_scalar_prefetch=2, grid=(N_EXPERTS,),
            in_specs=[pl.BlockSpec((nb * ROW_CHUNKS, LANES), rows),
                      pl.BlockSpec((nb, ROUTER_LANES), rows),
                      pl.BlockSpec((nb, d), fixed),
                      pl.BlockSpec((1, d), fixed),
                      pl.BlockSpec((None, None, d, EXPERT_FF), wsel),
                      pl.BlockSpec((None, None, d, EXPERT_FF), wsel),
                      pl.BlockSpec((None, None, EXPERT_FF, d), wsel)],
            out_specs=[pl.BlockSpec((nb, d), fixed), pl.BlockSpec((nb, d), fixed)],
            scratch_shapes=[pltpu.VMEM((nb, d), F32)]),
        out_shape=[jax.ShapeDtypeStruct((nb, d), F32), jax.ShapeDtypeStruct((nb, d), F32)],
        compiler_params=_cparams(("arbitrary",)),
        name="moe_sample",
    )(used_list, n_used.reshape(1).astype(jnp.int32), xn2, info, y, g.reshape(1, d),
      w_gate, w_up, w_down)


def _combine_kernel(pos_ref, y_ref, info_ref, g_ref, ys_hbm, *rest, rows, write_x, blk0, n_total):
    if write_x:
        xo_ref, xno_ref, buf0, buf1, sem = rest
    else:
        xno_ref, buf0, buf1, sem = rest
    i = pl.program_id(0)

    def row(src_row, buf, slot, r):
        return pltpu.make_async_copy(ys_hbm.at[pl.ds(src_row, 1)], buf.at[slot, pl.ds(r, 1)],
                                     sem.at[slot])

    def fetch(step, slot):
        base = (blk0 + step) * rows

        def body(r, c):
            t = base + r
            row(pos_ref[t], buf0, slot, r).start(priority=0)
            row(pos_ref[n_total + t], buf1, slot, r).start(priority=1)
            return c
        lax.fori_loop(0, rows, body, 0, unroll=8)

    @pl.when(i == 0)
    def _():
        fetch(0, 0)

    @pl.when(i + 1 < pl.num_programs(0))
    def _():
        fetch(i + 1, (i + 1) % 2)

    slot = i % 2
    for r in range(rows):
        row(0, buf0, slot, r).wait()
        row(0, buf1, slot, r).wait()
    xnew = y_ref[...] + info_ref[:, 4:5] * buf0[slot] + info_ref[:, 5:6] * buf1[slot]
    if write_x:
        xo_ref[...] = xnew
    xno_ref[...] = _rms(xnew, g_ref[...]).astype(xno_ref.dtype)


def _combine(pos, y, info, g, ys, norm_dtype, write_x, row0):
    n, d = y.shape
    rows = min(TM_COMBINE, n)
    assert row0 % rows == 0
    blk0 = row0 // rows
    tile = pl.BlockSpec((rows, d), lambda i, pos: (i, 0))
    out_specs = [tile]
    out_shape = [jax.ShapeDtypeStruct((n, d), norm_dtype)]
    if write_x:
        out_specs = [tile] + out_specs
        out_shape = [jax.ShapeDtypeStruct((n, d), F32)] + out_shape
    return pl.pallas_call(
        functools.partial(_combine_kernel, rows=rows, write_x=write_x, blk0=blk0,
                          n_total=pos.shape[0] // 2),
        grid_spec=pltpu.PrefetchScalarGridSpec(
            num_scalar_prefetch=1, grid=(n // rows,),
            in_specs=[tile,
                      pl.BlockSpec((rows, ROUTER_LANES), lambda i, pos: (blk0 + i, 0)),
                      pl.BlockSpec((1, d), lambda i, pos: (0, 0)),
                      pl.BlockSpec(memory_space=pl.ANY)],
            out_specs=out_specs,
            scratch_shapes=[pltpu.VMEM((2, rows, d), F32), pltpu.VMEM((2, rows, d), F32),
                            pltpu.SemaphoreType.DMA((2,))]),
        out_shape=out_shape,
        compiler_params=_cparams(("arbitrary",)),
        name="combine",
    )(pos, y, info, g.reshape(1, d), ys)


def _plan(idx, cnt, n_tiles):
    count = cnt[0, :N_EXPERTS].astype(jnp.int32)
    tiles = (count + TM_MOE - 1) // TM_MOE
    tile_end = jnp.cumsum(tiles)
    row_off = (tile_end - tiles) * TM_MOE
    n_active = tile_end[-1]
    expert = idx[0:2].astype(jnp.int32)
    ids3 = jnp.arange(N_EXPERTS, dtype=jnp.int32)[:, None, None]
    base = jnp.sum(jnp.where(expert[None] == ids3, row_off[:, None, None], 0), axis=0)
    pos = (base + idx[2:4].astype(jnp.int32)).reshape(-1)
    step = jnp.arange(n_tiles, dtype=jnp.int32)
    g = jnp.minimum(step, n_active - 1)
    tile_expert = jnp.sum((g[:, None] >= tile_end[None, :]).astype(jnp.int32), axis=1)
    prev = jnp.concatenate([jnp.full((1,), -1, jnp.int32), tile_expert[:-1]])
    first = ((step < n_active) & (tile_expert != prev)).astype(jnp.int32)
    ordinal = jnp.cumsum(first) - 1
    ids = jnp.arange(N_EXPERTS, dtype=jnp.int32)
    used_at_or_after = lax.cummin(jnp.where(tiles > 0, ids, N_EXPERTS), reverse=True)
    used_after = jnp.concatenate([used_at_or_after[1:], jnp.full((1,), N_EXPERTS, jnp.int32)])
    next_used = jnp.where(used_after == N_EXPERTS, -1, used_after)
    i32 = lambda a: a.astype(jnp.int32)
    return pos, dict(tile_expert=i32(tile_expert), n_active=i32(n_active.reshape(1)),
                     first=first, ord=i32(ordinal), next=i32(next_used[tile_expert]))


def kernel(x_prompt, x_sample, state_conv, cache_win_k, cache_win_v, state_pool, norm1_g, w_in,
           conv_w, attn_sinks, sgu_ln_g, sgu_ln_b, sgu_w, sgu_b, pool_w, pool_scale, w_branch, w_o,
           norm2_g, router_group_w, router_group_b, router_expert_w, router_expert_b, moe_w_gate,
           moe_w_up, moe_w_down, final_norm_g):
    batch, seq, d = x_prompt.shape
    assert seq % TB == 0 and seq >= 2 * TB and d == D_MODEL
    nb = x_sample.shape[0]
    depth = w_in.shape[0]
    n_p = batch * seq
    kv_w = N_KV * HEAD_DIM
    n_slots = 2 * (n_p + nb)
    n_tiles = (n_slots + N_EXPERTS * (TM_MOE - 1) + TM_MOE - 1) // TM_MOE

    xp = x_prompt.reshape(n_p, d)
    xs_ = x_sample.reshape(nb, d)
    xn_p = _rmsnorm(xp, norm1_g[0], BF16)
    xn_s = _rmsnorm(xs_, norm1_g[0], F32)

    pad = ROUTER_LANES - N_EXPERTS - N_EXP_GROUPS
    conv_p, k_p, v_p, pool_p = [], [], [], []
    conv_s, k_s, v_s, pool_s, chunk_s = [], [], [], [], []
    y_prompt = y_sample = None
    for l in range(depth):
        lw = dict(attn_sinks=attn_sinks[l], conv_w=conv_w[l], sgu_ln_g=sgu_ln_g[l],
                  sgu_ln_b=sgu_ln_b[l], sgu_w=sgu_w[l], sgu_b=sgu_b[l], pool_w=pool_w[l],
                  pool_scale=pool_scale[l])
        w_router = jnp.concatenate(
            [router_expert_w[l], router_group_w[l], jnp.zeros((d, pad), F32)], axis=1)
        b_router = jnp.concatenate(
            [router_expert_b[l], router_group_b[l], jnp.zeros((pad,), F32)]).reshape(1, -1)
        last = l == depth - 1
        g_next = final_norm_g if last else norm1_g[l + 1]

        zs_p = _in_proj(xn_p, w_in, l, False)
        outs_p, c_st, k_st, v_st, p_st = _branch_prompt(zs_p, batch, seq, lw)
        conv_p.append(c_st)
        k_p.append(k_st.reshape(batch, WINDOW, N_KV, HEAD_DIM))
        v_p.append(v_st.reshape(batch, WINDOW, N_KV, HEAD_DIM))
        pool_p.append(p_st)
        merged_p = _merge(xn_p, outs_p, w_in, w_branch, l, False)
        yp = _out_proj(merged_p, w_o, xp, l, False)

        zs_s = _in_proj(xn_s, w_in, l, True)
        outs_s, c_st, k_st, v_st, p_st, cv = _branch_sample(
            zs_s, lw, jnp.swapaxes(state_conv[l], 0, 1),
            cache_win_k[l].reshape(nb, WINDOW, kv_w), cache_win_v[l].reshape(nb, WINDOW, kv_w),
            jnp.swapaxes(state_pool[l], 0, 1))
        conv_s.append(jnp.swapaxes(c_st, 0, 1))
        k_s.append(k_st.reshape(nb, WINDOW, N_KV, HEAD_DIM))
        v_s.append(v_st.reshape(nb, WINDOW, N_KV, HEAD_DIM))
        pool_s.append(jnp.swapaxes(p_st, 0, 1))
        chunk_s.append(cv.reshape(nb, 1, BRANCH_W))

        xn_hi, outs_hi, x_hi = xn_s, outs_s, xs_
        if not last:
            x_t = xp.reshape(batch, seq, d)[:, seq - 2 * TB:].reshape(batch * 2 * TB, d)
            xn_t = _rmsnorm(x_t, norm1_g[l], F32)
            zs_t = _in_proj(xn_t, w_in, l, True)
            outs_t = _branch_prompt(zs_t, batch, seq, lw, hi=True, t_start=seq // TB - 2)[0]
            tail = lambda a: a.reshape(batch, 2 * TB, a.shape[-1])[:, 2 * TB - TAIL_EXACT:].reshape(
                batch * TAIL_EXACT, -1)
            outs_t = jnp.stack([tail(outs_t[b]) for b in range(N_BRANCH)])
            xn_hi = jnp.concatenate([tail(xn_t), xn_s])
            outs_hi = jnp.concatenate([outs_t, outs_s], axis=1)
            x_hi = jnp.concatenate([tail(x_t), xs_])
        y_hi = _out_proj(_merge(xn_hi, outs_hi, w_in, w_branch, l, True), w_o, x_hi, l, True)
        ys_ = y_hi[y_hi.shape[0] - nb:]
        if not last:
            yp = _put_last_rows(yp, y_hi[:batch * TAIL_EXACT], batch, seq)

        xn2, info, idx, cnt, rows_sorted = _router(yp, ys_, norm2_g[l], w_router, b_router, n_tiles)
        pos, plan = _plan(idx, cnt, n_tiles)
        rows_sorted = _dispatch(pos, xn2, rows_sorted)
        ffn = _moe(plan, rows_sorted, moe_w_gate, moe_w_up, moe_w_down, l)
        if last:
            (y_prompt,) = _combine(pos, yp, info, g_next, ffn, F32, False, 0)
            (y_sample,) = _combine(pos, ys_, info, g_next, ffn, F32, False, n_p)
        else:
            xp, xn_p = _combine(pos, yp, info, g_next, ffn, BF16, True, 0)
            xs_, xn_s = _moe_sample(xn2, info, ys_, g_next, moe_w_gate, moe_w_up, moe_w_down, l, n_p)

    return (y_prompt.reshape(batch, seq, d), y_sample.reshape(nb, 1, d),
            jnp.stack(conv_p), jnp.stack(k_p), jnp.stack(v_p), jnp.stack(pool_p),
            jnp.stack(conv_s), jnp.stack(k_s), jnp.stack(v_s), jnp.stack(pool_s),
            jnp.stack(chunk_s))
```

```python
import functools

import jax
import jax.numpy as jnp
from jax import lax
from jax.experimental import pallas as pl
from jax.experimental.pallas import tpu as pltpu

F32 = jnp.float32
BF16 = jnp.bfloat16

D_MODEL = 2048
BRANCH_W = 512
N_BRANCH = 4
CONV_W = 3
HEAD_DIM = 64
N_HEADS = 8
N_KV = 2
GQA_G = 4
WINDOW = 128
ROPE_THETA = 500000.0
ROPE_DIM = 16
CHUNK = 128
SGU_GROUPS = 4
POOL_WINDOWS = (2, 4, 8, 16)
POOL_MAX = 16
N_EXP_GROUPS = 4
EXP_PER_GROUP = 8
N_EXPERTS = 32
EXPERT_FF = 512
EPS = 1e-6
PAST_LEN = 16384
NEG = -3.0e38

A_H_OFF = 0
A_C_OFF = 512
A_B_OFF = 1024
Q_OFF = 1536
K_OFF = 2048
V_OFF = 2176
C_OFF = 2304
D_OFF = 3328
G_OFF = 3840

LANES = 128
SUBLANES = 8
VMEM_LIMIT = 52 * 1024 * 1024

TM_NORM = 512
TM_ROUTER = 512
TM_PROJ = 2048
TN_PROJ = 768
TM_MERGE = 2048
TN_MERGE = 256
TM_WO = 2048
TN_WO = 512
TM_MOE = 288
TM_COMBINE = 256
TB = 128
TAIL_EXACT = 16
ROUTER_LANES = 128


def _cparams(sem):
    return pltpu.CompilerParams(dimension_semantics=sem, vmem_limit_bytes=VMEM_LIMIT)


_NN = (((1,), (0,)), ((), ()))
_NT = (((1,), (1,)), ((), ()))


def _dot(a, b, dims=_NN):
    return lax.dot_general(a, b, dims, preferred_element_type=F32)


def _split(x):
    hi = x.astype(BF16)
    lo = (x - hi.astype(F32)).astype(BF16)
    return hi, lo


def _dot3(a, b, dims=_NN):
    ah, al = _split(a)
    bh, bl = _split(b)
    return _dot(ah, bh, dims) + _dot(al, bh, dims) + _dot(ah, bl, dims)


def _mm(a, w, hi, dims=_NN):
    if hi:
        return _dot3(a.astype(F32), w.astype(F32), dims)
    return _dot(a.astype(BF16), w.astype(BF16), dims)


def _rms(x, g):
    return (x * lax.rsqrt(jnp.mean(x * x, axis=-1, keepdims=True) + EPS)) * g


def _rmsnorm_kernel(x_ref, g_ref, o_ref):
    o_ref[...] = _rms(x_ref[...], g_ref[...]).astype(o_ref.dtype)


def _rmsnorm(x, g, out_dtype):
    n, d = x.shape
    tm = min(TM_NORM, n)
    return pl.pallas_call(
        _rmsnorm_kernel,
        grid=(n // tm,),
        in_specs=[pl.BlockSpec((tm, d), lambda i: (i, 0)),
                  pl.BlockSpec((1, d), lambda i: (0, 0))],
        out_specs=pl.BlockSpec((tm, d), lambda i: (i, 0)),
        out_shape=jax.ShapeDtypeStruct((n, d), out_dtype),
        compiler_params=_cparams(("parallel",)),
        name="rmsnorm",
    )(x, g.reshape(1, d))


def _proj_kernel(x_ref, w_ref, o_ref, *, hi):
    o_ref[...] = _mm(x_ref[...], w_ref[...], hi)


def _in_proj(xn, w_in, layer, hi):
    n, d = xn.shape
    tm = min(TM_NORM if hi else TM_PROJ, n)
    return pl.pallas_call(
        functools.partial(_proj_kernel, hi=hi),
        grid=(n // tm, G_OFF // TN_PROJ),
        in_specs=[pl.BlockSpec((tm, d), lambda i, j: (i, 0)),
                  pl.BlockSpec((None, d, TN_PROJ), lambda i, j: (layer, 0, j))],
        out_specs=pl.BlockSpec((tm, TN_PROJ), lambda i, j: (i, j)),
        out_shape=jax.ShapeDtypeStruct((n, G_OFF), F32),
        compiler_params=_cparams(("parallel", "arbitrary")),
        name="in_proj",
    )(xn, w_in)


def _rope_tables(pos):
    half = ROPE_DIM // 2
    inv = jnp.power(jnp.float32(ROPE_THETA), -jnp.arange(half, dtype=F32) * (2.0 / ROPE_DIM))
    ang = pos.astype(F32)[:, None] * inv[None, :]
    cos, sin = jnp.cos(ang), jnp.sin(ang)
    t = pos.shape[0]
    rest = HEAD_DIM - ROPE_DIM
    c = jnp.concatenate([cos, cos, jnp.ones((t, rest), F32)], axis=1)
    s1 = jnp.concatenate([-sin, jnp.zeros((t, HEAD_DIM - half), F32)], axis=1)
    s2 = jnp.concatenate([jnp.zeros((t, half), F32), sin, jnp.zeros((t, rest), F32)], axis=1)
    rep = LANES // HEAD_DIM
    return jnp.tile(c, (1, rep)), jnp.tile(s1, (1, rep)), jnp.tile(s2, (1, rep))


def _rope(x, c, s1, s2):
    half = ROPE_DIM // 2
    return x * c + pltpu.roll(x, LANES - half, 1) * s1 + pltpu.roll(x, half, 1) * s2


def _gelu(x):
    return 0.5 * x * (1.0 + lax.erf(x * 0.7071067811865476))


def _layernorm(v, g, b):
    mu = jnp.mean(v, axis=-1, keepdims=True)
    vc = v - mu
    return vc * lax.rsqrt(jnp.mean(vc * vc, axis=-1, keepdims=True) + EPS) * g + b


def _branch_prompt_kernel(sink_ref, z_ref, rc_ref, rs1_ref, rs2_ref, cw_ref, lng_ref, lnb_ref,
                          sw_ref, sbt_ref, pw_ref, ps_ref,
                          outs_ref, conv_ref, kst_ref, vst_ref, pool_ref,
                          kprev, vprev, ci_ext, p_ext, *, hi, t_start):
    tb = pl.program_id(1)
    halo_c = SUBLANES
    halo_p = POOL_MAX

    @pl.when(tb == 0)
    def _():
        kprev[...] = jnp.zeros_like(kprev)
        vprev[...] = jnp.zeros_like(vprev)
        ci_ext[0:halo_c, :] = jnp.zeros((halo_c, BRANCH_W), F32)
        p_ext[0:halo_p, :] = jnp.zeros((halo_p, BRANCH_W), F32)

    ci = z_ref[:, A_C_OFF:A_B_OFF] * z_ref[:, A_H_OFF:A_C_OFF]
    ci_ext[halo_c:halo_c + TB, :] = ci
    y = cw_ref[2:3, :] * ci
    for j in range(CONV_W - 1):
        y = y + cw_ref[j:j + 1, :] * ci_ext[pl.ds(halo_c - (CONV_W - 1) + j, TB), :]
    outs_ref[0] = (z_ref[:, A_B_OFF:Q_OFF] * y).astype(outs_ref.dtype)
    conv_ref[...] = ci_ext[pl.ds(halo_c + TB - (CONV_W - 1), CONV_W - 1), :]
    ci_ext[0:halo_c, :] = ci[TB - halo_c:, :]

    p = z_ref[:, D_OFF:G_OFF]
    p_ext[halo_p:halo_p + TB, :] = p
    posf = ((t_start + tb) * TB + lax.broadcasted_iota(jnp.int32, (TB, 1), 0)).astype(F32)
    for g, w in enumerate(POOL_WINDOWS):
        sl = slice(g * LANES, (g + 1) * LANES)
        s = p[:, sl]
        for k in range(1, w):
            s = s + p_ext[pl.ds(halo_p - k, TB), sl]
        dlt = s / jnp.minimum(posf + 1.0, float(w)) - p[:, sl]
        yg = _mm(dlt, pw_ref[g], hi) * ps_ref[:, sl]
        outs_ref[3, :, sl] = yg.astype(outs_ref.dtype)
    pool_ref[...] = p_ext[pl.ds(halo_p + TB - (POOL_MAX - 1), POOL_MAX - 1), :]
    p_ext[0:halo_p, :] = p[TB - halo_p:, :]

    ge = _gelu(z_ref[:, C_OFF:D_OFF])
    u = ge[:, :BRANCH_W]
    vn = _layernorm(ge[:, BRANCH_W:], lng_ref[...], lnb_ref[...])
    r_i = lax.broadcasted_iota(jnp.int32, (TB, TB), 0)
    c_i = lax.broadcasted_iota(jnp.int32, (TB, TB), 1)
    for g in range(SGU_GROUPS):
        sl = slice(g * LANES, (g + 1) * LANES)
        wt = jnp.where(c_i <= r_i, sw_ref[g], 0.0)
        f = _mm(wt, vn[:, sl], hi) + sbt_ref[:, g:g + 1]
        outs_ref[2, :, sl] = (u[:, sl] * f).astype(outs_ref.dtype)

    rc, rs1, rs2 = rc_ref[...], rs1_ref[...], rs2_ref[...]
    kr = _rope(z_ref[:, K_OFF:V_OFF], rc, rs1, rs2)
    v = z_ref[:, V_OFF:C_OFF]
    qr = [_rope(z_ref[:, Q_OFF + c * LANES:Q_OFF + (c + 1) * LANES], rc, rs1, rs2)
          for c in range(N_HEADS * HEAD_DIM // LANES)]
    qi = lax.broadcasted_iota(jnp.int32, (TB, 2 * TB), 0)
    kj = lax.broadcasted_iota(jnp.int32, (TB, 2 * TB), 1)
    allowed = (kj > qi) & (kj <= qi + WINDOW) & ((kj >= TB) | (tb > 0))
    kp, vp = kprev[...], vprev[...]
    heads = [None] * N_HEADS
    for hk in range(N_KV):
        hs = slice(hk * HEAD_DIM, (hk + 1) * HEAD_DIM)
        kk = jnp.concatenate([kp[:, hs], kr[:, hs]], axis=0)
        vv = jnp.concatenate([vp[:, hs], v[:, hs]], axis=0)
        if not hi:
            kk, vv = kk.astype(BF16), vv.astype(BF16)
        for g in range(GQA_G):
            hq = hk * GQA_G + g
            per = LANES // HEAD_DIM
            qh = qr[hq // per][:, (hq % per) * HEAD_DIM:(hq % per + 1) * HEAD_DIM]
            s = _mm(qh, kk, hi, _NT) * (HEAD_DIM ** -0.5)
            s = jnp.where(allowed, s, -1e30)
            sink = sink_ref[hq]
            m = jnp.maximum(jnp.max(s, axis=-1, keepdims=True), sink)
            e = jnp.exp(s - m)
            den = jnp.sum(e, axis=-1, keepdims=True) + jnp.exp(sink - m)
            heads[hq] = _mm(e / den, vv, hi)
    outs_ref[1] = jnp.concatenate(heads, axis=1).astype(outs_ref.dtype)
    kst_ref[...] = kr
    vst_ref[...] = v
    kprev[...] = kr
    vprev[...] = v


def _branch_prompt(zs, batch, seq, lw, hi=False, t_start=0):
    n = zs.shape[0]
    nt = n // (batch * TB)
    rc, rs1, rs2 = _rope_tables(jnp.arange(seq))
    full = lambda shape: pl.BlockSpec(shape, lambda b, t: (0,) * len(shape))
    tab = pl.BlockSpec((TB, LANES), lambda b, t: (t_start + t, 0))
    kv_w = N_KV * HEAD_DIM
    return pl.pallas_call(
        functools.partial(_branch_prompt_kernel, hi=hi, t_start=t_start),
        grid=(batch, nt),
        in_specs=[pl.BlockSpec(memory_space=pltpu.SMEM),
                  pl.BlockSpec((TB, G_OFF), lambda b, t: (b * nt + t, 0)),
                  tab, tab, tab,
                  full((CONV_W, BRANCH_W)), full((1, BRANCH_W)), full((1, BRANCH_W)),
                  full((SGU_GROUPS, CHUNK, CHUNK)), full((CHUNK, SGU_GROUPS)),
                  full((len(POOL_WINDOWS), LANES, LANES)), full((1, BRANCH_W))],
        out_specs=[pl.BlockSpec((N_BRANCH, TB, BRANCH_W), lambda b, t: (0, b * nt + t, 0)),
                   pl.BlockSpec((None, CONV_W - 1, BRANCH_W), lambda b, t: (b, 0, 0)),
                   pl.BlockSpec((None, WINDOW, kv_w), lambda b, t: (b, 0, 0)),
                   pl.BlockSpec((None, WINDOW, kv_w), lambda b, t: (b, 0, 0)),
                   pl.BlockSpec((None, POOL_MAX - 1, BRANCH_W), lambda b, t: (b, 0, 0))],
        out_shape=[jax.ShapeDtypeStruct((N_BRANCH, n, BRANCH_W), F32 if hi else BF16),
                   jax.ShapeDtypeStruct((batch, CONV_W - 1, BRANCH_W), F32),
                   jax.ShapeDtypeStruct((batch, WINDOW, kv_w), F32),
                   jax.ShapeDtypeStruct((batch, WINDOW, kv_w), F32),
                   jax.ShapeDtypeStruct((batch, POOL_MAX - 1, BRANCH_W), F32)],
        scratch_shapes=[pltpu.VMEM((TB, kv_w), F32), pltpu.VMEM((TB, kv_w), F32),
                        pltpu.VMEM((SUBLANES + TB, BRANCH_W), F32),
                        pltpu.VMEM((POOL_MAX + TB, BRANCH_W), F32)],
        compiler_params=_cparams(("arbitrary", "arbitrary")),
        name="branch_prompt",
    )(lw["attn_sinks"], zs, rc, rs1, rs2, lw["conv_w"], lw["sgu_ln_g"].reshape(1, -1),
      lw["sgu_ln_b"].reshape(1, -1), lw["sgu_w"], lw["sgu_b"].T, lw["pool_w"],
      lw["pool_scale"].reshape(1, -1))


def _branch_sample_kernel(sink_ref, z_ref, rc_ref, rs1_ref, rs2_ref, cw_ref, lng_ref, lnb_ref,
                          ssc_ref, ssh_ref, pw_ref, ps_ref, cpast_ref, kc_ref, vc_ref, ppast_ref,
                          outs_ref, conv_ref, kst_ref, vst_ref, pool_ref, chunkv_ref,
                          qrot, krot, vnew):
    nb = z_ref.shape[0]

    ci = z_ref[:, A_C_OFF:A_B_OFF] * z_ref[:, A_H_OFF:A_C_OFF]
    y = cw_ref[CONV_W - 1:CONV_W, :] * ci
    for j in range(CONV_W - 1):
        y = y + cw_ref[j:j + 1, :] * cpast_ref[j]
    outs_ref[0] = z_ref[:, A_B_OFF:Q_OFF] * y
    for j in range(CONV_W - 2):
        conv_ref[j] = cpast_ref[j + 1]
    conv_ref[CONV_W - 2] = ci

    p = z_ref[:, D_OFF:G_OFF]
    for g, w in enumerate(POOL_WINDOWS):
        sl = slice(g * LANES, (g + 1) * LANES)
        s = p[:, sl]
        for k in range(1, w):
            s = s + ppast_ref[POOL_MAX - 1 - k, :, sl]
        dlt = s / float(min(PAST_LEN + 1, w)) - p[:, sl]
        outs_ref[3, :, sl] = _dot3(dlt, pw_ref[g]) * ps_ref[:, sl]
    for j in range(POOL_MAX - 2):
        pool_ref[j] = ppast_ref[j + 1]
    pool_ref[POOL_MAX - 2] = p

    ge = _gelu(z_ref[:, C_OFF:D_OFF])
    vn = _layernorm(ge[:, BRANCH_W:], lng_ref[...], lnb_ref[...])
    outs_ref[2] = ge[:, :BRANCH_W] * (vn * ssc_ref[...] + ssh_ref[...])
    chunkv_ref[...] = vn

    rc, rs1, rs2 = rc_ref[...], rs1_ref[...], rs2_ref[...]
    krot[...] = _rope(z_ref[:, K_OFF:V_OFF], rc, rs1, rs2)
    vnew[...] = z_ref[:, V_OFF:C_OFF]
    for c in range(N_HEADS * HEAD_DIM // LANES):
        cs = slice(c * LANES, (c + 1) * LANES)
        qrot[:, cs] = _rope(z_ref[:, Q_OFF + c * LANES:Q_OFF + (c + 1) * LANES], rc, rs1, rs2)
    row = lax.broadcasted_iota(jnp.int32, (WINDOW, LANES), 0)
    lo = lax.broadcasted_iota(jnp.int32, (WINDOW, LANES), 1) < HEAD_DIM
    lo1 = lax.broadcasted_iota(jnp.int32, (1, LANES), 1) < HEAD_DIM
    scale = HEAD_DIM ** -0.5

    def body(b, carry):
        kn = jnp.where(row == WINDOW - 1, krot[pl.ds(b, 1), :], pltpu.roll(kc_ref[b], WINDOW - 1, 0))
        vn_b = jnp.where(row == WINDOW - 1, vnew[pl.ds(b, 1), :], pltpu.roll(vc_ref[b], WINDOW - 1, 0))
        kst_ref[b] = kn
        vst_ref[b] = vn_b
        qb = qrot[pl.ds(b, 1), :]
        o = []
        for g in range(GQA_G):
            ca = qb[:, (g // 2) * LANES:(g // 2 + 1) * LANES]
            cb = qb[:, (2 + g // 2) * LANES:(3 + g // 2) * LANES]
            if g % 2 == 0:
                qrow = jnp.where(lo1, ca, pltpu.roll(cb, HEAD_DIM, 1))
            else:
                qrow = jnp.where(lo1, pltpu.roll(ca, HEAD_DIM, 1), cb)
            prod = kn * qrow
            pn = []
            for part, hq in ((jnp.where(lo, prod, 0.0), g), (jnp.where(lo, 0.0, prod), GQA_G + g)):
                s = jnp.sum(part, axis=1, keepdims=True) * scale
                sink = sink_ref[hq]
                m = jnp.maximum(jnp.max(s, axis=0, keepdims=True), sink)
                e = jnp.exp(s - m)
                den = jnp.sum(e, axis=0, keepdims=True) + jnp.exp(sink - m)
                pn.append(e / den)
            o.append(jnp.sum(jnp.where(lo, pn[0], pn[1]) * vn_b, axis=0, keepdims=True))
        sw = lambda x: pltpu.roll(x, HEAD_DIM, 1)
        chunks = [jnp.where(lo1, o[0], sw(o[1])), jnp.where(lo1, o[2], sw(o[3])),
                  jnp.where(lo1, sw(o[0]), o[1]), jnp.where(lo1, sw(o[2]), o[3])]
        outs_ref[1, pl.ds(b, 1), :] = jnp.concatenate(chunks, axis=1)
        return carry

    lax.fori_loop(0, nb, body, 0)


def _branch_sample(zs, lw, cpast, kc, vc, ppast):
    nb = zs.shape[0]
    kv_w = N_KV * HEAD_DIM
    rc, rs1, rs2 = _rope_tables(jnp.full((1,), PAST_LEN, jnp.int32))
    ssc = jnp.repeat(lw["sgu_w"][:, 0, 0], CHUNK).reshape(1, BRANCH_W)
    ssh = jnp.repeat(lw["sgu_b"][:, 0], CHUNK).reshape(1, BRANCH_W)
    vspec = pl.BlockSpec(memory_space=pltpu.VMEM)
    return pl.pallas_call(
        _branch_sample_kernel,
        in_specs=[pl.BlockSpec(memory_space=pltpu.SMEM)] + [vspec] * 15,
        out_specs=[vspec] * 6,
        out_shape=[jax.ShapeDtypeStruct((N_BRANCH, nb, BRANCH_W), F32),
                   jax.ShapeDtypeStruct((CONV_W - 1, nb, BRANCH_W), F32),
                   jax.ShapeDtypeStruct((nb, WINDOW, kv_w), F32),
                   jax.ShapeDtypeStruct((nb, WINDOW, kv_w), F32),
                   jax.ShapeDtypeStruct((POOL_MAX - 1, nb, BRANCH_W), F32),
                   jax.ShapeDtypeStruct((nb, BRANCH_W), F32)],
        scratch_shapes=[pltpu.VMEM((nb, N_HEADS * HEAD_DIM), F32), pltpu.VMEM((nb, kv_w), F32),
                        pltpu.VMEM((nb, kv_w), F32)],
        compiler_params=pltpu.CompilerParams(vmem_limit_bytes=VMEM_LIMIT),
        name="branch_sample",
    )(lw["attn_sinks"], zs, rc, rs1, rs2, lw["conv_w"], lw["sgu_ln_g"].reshape(1, -1),
      lw["sgu_ln_b"].reshape(1, -1), ssc, ssh, lw["pool_w"], lw["pool_scale"].reshape(1, -1),
      cpast, kc, vc, ppast)


def _merge_kernel(xn_ref, outs_ref, wg0_ref, wg1_ref, wb_ref, o_ref, *rest, hi, zero_tiles,
                  zero_per_step):
    nbr = pl.program_id(2)
    if zero_tiles:
        zero_hbm, acc_ref, zbuf, zsem = rest
        step = (pl.program_id(0) * pl.num_programs(1) + pl.program_id(1)) * pl.num_programs(2) + nbr
        tile_lines = zbuf.shape[0]

        @pl.when(step == 0)
        def _():
            zbuf[...] = jnp.zeros_like(zbuf)

        def zero_tile(j):
            k = step * zero_per_step + j
            dst = zero_hbm.at[pl.ds(pl.multiple_of(k * tile_lines, tile_lines), tile_lines)]
            return k, pltpu.make_async_copy(zbuf, dst, zsem)

        for j in range(zero_per_step):
            k, cp = zero_tile(j)

            @pl.when(k < zero_tiles)
            def _():
                cp.start()
    else:
        (acc_ref,) = rest

    @pl.when(nbr == 0)
    def _():
        acc_ref[...] = jnp.zeros_like(acc_ref)

    xn, outs = xn_ref[...], outs_ref[...]
    for h, wg_ref in enumerate((wg0_ref, wg1_ref)):
        cols = slice(h * TN_MERGE, (h + 1) * TN_MERGE)
        gate = jax.nn.sigmoid(_mm(xn, wg_ref[...], hi))
        acc_ref[:, cols] += gate * _mm(outs, wb_ref[:, cols], hi)

    @pl.when(nbr == N_BRANCH - 1)
    def _():
        o_ref[...] = acc_ref[...].astype(o_ref.dtype)

    if zero_tiles:
        for j in range(zero_per_step):
            k, cp = zero_tile(j)

            @pl.when(k < zero_tiles)
            def _():
                cp.wait()


def _merge(xn, outs, w_in, w_branch, layer, hi, zero_tiles=0):
    n, d = xn.shape
    tm = min(TM_MERGE, n)
    g_blk = G_OFF // TN_MERGE
    per_branch = D_MODEL // TN_MERGE
    wide = 2 * TN_MERGE
    gate_block = lambda h: pl.BlockSpec(
        (None, d, TN_MERGE), lambda i, j, b: (layer, 0, g_blk + b * per_branch + 2 * j + h))
    grid = (n // tm, D_MODEL // wide, N_BRANCH)
    out_specs = [pl.BlockSpec((tm, wide), lambda i, j, b: (i, j))]
    out_shape = [jax.ShapeDtypeStruct((n, D_MODEL), F32 if hi else BF16)]
    scratch = [pltpu.VMEM((tm, wide), F32)]
    if zero_tiles:
        tile_lines = TM_MOE * ROW_CHUNKS
        out_specs.append(pl.BlockSpec(memory_space=pl.ANY))
        out_shape.append(jax.ShapeDtypeStruct((zero_tiles * tile_lines, LANES), F32))
        scratch += [pltpu.VMEM((tile_lines, LANES), F32), pltpu.SemaphoreType.DMA(())]
    res = pl.pallas_call(
        functools.partial(_merge_kernel, hi=hi, zero_tiles=zero_tiles,
                          zero_per_step=pl.cdiv(zero_tiles, grid[0] * grid[1] * grid[2])),
        grid=grid,
        in_specs=[pl.BlockSpec((tm, d), lambda i, j, b: (i, 0)),
                  pl.BlockSpec((None, tm, BRANCH_W), lambda i, j, b: (b, i, 0)),
                  gate_block(0), gate_block(1),
                  pl.BlockSpec((None, None, BRANCH_W, wide), lambda i, j, b: (layer, b, 0, j))],
        out_specs=out_specs,
        out_shape=out_shape,
        scratch_shapes=scratch,
        compiler_params=_cparams(("arbitrary", "arbitrary", "arbitrary")),
        name="merge",
    )(xn, outs, w_in, w_in, w_branch)
    return res if zero_tiles else res[0]


def _wo_kernel(m_ref, w_ref, x_ref, o_ref, *, hi):
    o_ref[...] = x_ref[...] + _mm(m_ref[...], w_ref[...], hi)


def _out_proj(merged, w_o, x, layer, hi):
    n, d = x.shape
    tm = min(TM_WO, n)
    return pl.pallas_call(
        functools.partial(_wo_kernel, hi=hi),
        grid=(n // tm, d // TN_WO),
        in_specs=[pl.BlockSpec((tm, d), lambda i, j: (i, 0)),
                  pl.BlockSpec((None, d, TN_WO), lambda i, j: (layer, 0, j)),
                  pl.BlockSpec((tm, TN_WO), lambda i, j: (i, j))],
        out_specs=pl.BlockSpec((tm, TN_WO), lambda i, j: (i, j)),
        out_shape=jax.ShapeDtypeStruct((n, d), F32),
        compiler_params=_cparams(("parallel", "arbitrary")),
        name="out_proj",
    )(merged, w_o, x)


def _put_rows_kernel(y_hbm, t_ref, o_ref):
    del y_hbm
    o_ref[...] = t_ref[...]


def _put_last_rows(y, y_tail, batch, seq):
    d = y.shape[1]
    nt = seq // TAIL_EXACT
    return pl.pallas_call(
        _put_rows_kernel,
        grid=(batch,),
        in_specs=[pl.BlockSpec(memory_space=pl.ANY),
                  pl.BlockSpec((TAIL_EXACT, d), lambda b: (b, 0))],
        out_specs=pl.BlockSpec((TAIL_EXACT, d), lambda b: (b * nt + nt - 1, 0)),
        out_shape=jax.ShapeDtypeStruct(y.shape, y.dtype),
        input_output_aliases={0: 0},
        compiler_params=_cparams(("arbitrary",)),
        name="put_last_rows",
    )(y, y_tail)


def _route_rows(y, g_ref, wr_ref, br_ref, carry_ref):
    tm = y.shape[0]
    xn = _rms(y, g_ref[...])
    logits = _dot3(xn, wr_ref[...]) + br_ref[...]
    lane_i = lax.broadcasted_iota(jnp.int32, (tm, ROUTER_LANES), 1)
    lane = lane_i.astype(F32)
    lane_grp = lax.shift_right_logical(lane_i, 3).astype(F32)
    big = float(ROUTER_LANES)
    is_grp = (lane_i >= N_EXPERTS) & (lane_i < N_EXPERTS + N_EXP_GROUPS)
    lg = jnp.where(is_grp, logits, NEG)
    gmax = jnp.max(lg, axis=-1, keepdims=True)
    gsel = jnp.min(jnp.where(lg == gmax, lane - N_EXPERTS, big), axis=-1, keepdims=True)
    wg = 1.0 / jnp.sum(jnp.where(is_grp, jnp.exp(logits - gmax), 0.0), axis=-1, keepdims=True)
    in_grp = (lane_i < N_EXPERTS) & (lane_grp == gsel)
    v1 = jnp.where(in_grp, logits, NEG)
    m1 = jnp.max(v1, axis=-1, keepdims=True)
    i1 = jnp.min(jnp.where(v1 == m1, lane, big), axis=-1, keepdims=True)
    v2 = jnp.where(lane == i1, NEG, v1)
    m2 = jnp.max(v2, axis=-1, keepdims=True)
    i2 = jnp.min(jnp.where(v2 == m2, lane, big), axis=-1, keepdims=True)
    e2 = jnp.exp(m2 - m1)
    w1 = wg / (1.0 + e2)
    w2 = wg * e2 / (1.0 + e2)

    sel1 = lane == i1
    sel2 = lane == i2
    onehot = jnp.where(sel1 | sel2, 1.0, 0.0)
    r_i = lax.broadcasted_iota(jnp.int32, (tm, tm), 0)
    c_i = lax.broadcasted_iota(jnp.int32, (tm, tm), 1)
    before = jnp.where(c_i < r_i, 1.0, 0.0).astype(BF16)
    counts = _dot(before, onehot.astype(BF16)) + carry_ref[0:1, :]
    r1 = jnp.sum(jnp.where(sel1, counts, 0.0), axis=-1, keepdims=True)
    r2 = jnp.sum(jnp.where(sel2, counts, 0.0), axis=-1, keepdims=True)
    carry_ref[...] = carry_ref[...] + jnp.sum(onehot, axis=0, keepdims=True)

    info = jnp.zeros((tm, ROUTER_LANES), F32)
    for k, val in enumerate((i1, i2, r1, r2, w1, w2)):
        info = jnp.where(lane_i == k, val, info)
    return xn, info


def _router_kernel(yp_ref, ys_ref, g_ref, wr_ref, br_ref, xn_ref, info_ref, idx_ref, cnt_ref,
                   carry_ref, *, prompt_tiles):
    i = pl.program_id(0)

    @pl.when(i == 0)
    def _():
        carry_ref[...] = jnp.zeros_like(carry_ref)

    @pl.when(i < prompt_tiles)
    def _():
        xn, info = _route_rows(yp_ref[...], g_ref, wr_ref, br_ref, carry_ref)
        _store_row_tiles(xn_ref, xn)
        info_ref[...] = info
        idx_ref[...] = info.T[0:SUBLANES, :]

    @pl.when(i == prompt_tiles)
    def _():
        nb = ys_ref.shape[0]
        xn, info = _route_rows(ys_ref[...], g_ref, wr_ref, br_ref, carry_ref)
        _store_row_tiles(xn_ref, xn)
        info_ref[0:nb, :] = info
        idx_ref[:, 0:nb] = info.T[0:SUBLANES, :]

    cnt_ref[...] = carry_ref[...]


def _router(yp, ys, g, w_router, b_router):
    n_p, d = yp.shape
    nb = ys.shape[0]
    tm = min(TM_ROUTER, n_p)
    prompt_tiles = n_p // tm
    n = n_p + nb
    return pl.pallas_call(
        functools.partial(_router_kernel, prompt_tiles=prompt_tiles),
        grid=(prompt_tiles + 1,),
        in_specs=[pl.BlockSpec((tm, d), lambda i: (jnp.minimum(i, prompt_tiles - 1), 0)),
                  pl.BlockSpec((nb, d), lambda i: (0, 0)),
                  pl.BlockSpec((1, d), lambda i: (0, 0)),
                  pl.BlockSpec((d, ROUTER_LANES), lambda i: (0, 0)),
                  pl.BlockSpec((1, ROUTER_LANES), lambda i: (0, 0))],
        out_specs=[pl.BlockSpec((tm * ROW_CHUNKS, LANES), lambda i: (i, 0)),
                   pl.BlockSpec((tm, ROUTER_LANES), lambda i: (i, 0)),
                   pl.BlockSpec((SUBLANES, tm), lambda i: (0, i)),
                   pl.BlockSpec((SUBLANES, ROUTER_LANES), lambda i: (0, 0))],
        out_shape=[jax.ShapeDtypeStruct((n * ROW_CHUNKS, LANES), F32),
                   jax.ShapeDtypeStruct((n, ROUTER_LANES), F32),
                   jax.ShapeDtypeStruct((SUBLANES, n), F32),
                   jax.ShapeDtypeStruct((SUBLANES, ROUTER_LANES), F32)],
        scratch_shapes=[pltpu.VMEM((SUBLANES, ROUTER_LANES), F32)],
        compiler_params=_cparams(("arbitrary",)),
        name="router",
    )(yp, ys, g.reshape(1, d), w_router, b_router)


ROW_CHUNKS = D_MODEL // LANES


def _store_row_tiles(ref, x):
    for c in range(ROW_CHUNKS):
        ref[pl.ds(c, x.shape[0], stride=ROW_CHUNKS), :] = x[:, c * LANES:(c + 1) * LANES]


def _load_row_tiles(ref, rows):
    return jnp.concatenate(
        [ref[pl.ds(c, rows, stride=ROW_CHUNKS), :] for c in range(ROW_CHUNKS)], axis=1)


def _row_copy(src, i, dst, j, sem):
    def lines(k):
        start = k * ROW_CHUNKS
        if not isinstance(k, int):
            start = pl.multiple_of(start, ROW_CHUNKS)
        return pl.ds(start, ROW_CHUNKS)
    return pltpu.make_async_copy(src.at[lines(i)], dst.at[lines(j)], sem)


def _dispatch_kernel(pos_ref, x_ref, xs_in, xs_out, sem, *, rows, n_tokens):
    del xs_in
    i = pl.program_id(0)
    base = i * rows
    full_steps, tail = divmod(n_tokens, rows)

    def copies(r):
        t = base + r
        return (_row_copy(x_ref, r, xs_out, pos_ref[t], sem),
                _row_copy(x_ref, r, xs_out, pos_ref[n_tokens + t], sem))

    def scatter(count):
        def start(r, c):
            for queue, cp in enumerate(copies(r)):
                cp.start(priority=queue)
            return c
        lax.fori_loop(0, count, start, 0, unroll=8)
        for _ in range(2 * count):
            _row_copy(x_ref, 0, xs_out, 0, sem).wait()

    @pl.when(i < full_steps)
    def _():
        scatter(rows)

    if tail:
        @pl.when(i == full_steps)
        def _():
            scatter(tail)


def _dispatch(pos, xn, xs_zero):
    n_tokens = xn.shape[0] // ROW_CHUNKS
    rows = TM_NORM
    any_spec = pl.BlockSpec(memory_space=pl.ANY)
    return pl.pallas_call(
        functools.partial(_dispatch_kernel, rows=rows, n_tokens=n_tokens),
        grid_spec=pltpu.PrefetchScalarGridSpec(
            num_scalar_prefetch=1, grid=(pl.cdiv(n_tokens, rows),),
            in_specs=[pl.BlockSpec((rows * ROW_CHUNKS, LANES), lambda i, pos: (i, 0)), any_spec],
            out_specs=any_spec,
            scratch_shapes=[pltpu.SemaphoreType.DMA(())]),
        out_shape=jax.ShapeDtypeStruct(xs_zero.shape, xs_zero.dtype),
        input_output_aliases={2: 0},
        compiler_params=_cparams(("arbitrary",)),
        name="dispatch",
    )(pos, xn, xs_zero)


def _moe_kernel(te_ref, na_ref, first_ref, ord_ref, next_ref, x_ref, wg_hbm, wu_hbm, wd_hbm,
                ys_ref, wg_buf, wu_buf, wd_buf, wsem, *, layer):
    g = pl.program_id(0)
    n_act = na_ref[0]

    def weight_copies(e, slot):
        return [pltpu.make_async_copy(hbm.at[layer, e], vm.at[slot], wsem.at[slot])
                for hbm, vm in ((wg_hbm, wg_buf), (wu_hbm, wu_buf), (wd_hbm, wd_buf))]

    @pl.when(g == 0)
    def _():
        for cp in weight_copies(te_ref[0], 0):
            cp.start()

    @pl.when(g < n_act)
    def _():
        wslot = ord_ref[g] % 2

        @pl.when(first_ref[g] == 1)
        def _():
            @pl.when(next_ref[g] >= 0)
            def _():
                for cp in weight_copies(next_ref[g], 1 - wslot):
                    cp.start()
            for cp in weight_copies(te_ref[g], wslot):
                cp.wait()

        x = _load_row_tiles(x_ref, TM_MOE).astype(BF16)
        a = _dot(x, wg_buf[wslot].astype(BF16))
        u = _dot(x, wu_buf[wslot].astype(BF16))
        h = (a * jax.nn.sigmoid(a)) * u
        ys_ref[...] = _dot(h.astype(BF16), wd_buf[wslot].astype(BF16))

    @pl.when(g >= n_act)
    def _():
        ys_ref[...] = jnp.zeros_like(ys_ref)


def _moe(plan, xs, w_gate, w_up, w_down, layer):
    d = D_MODEL
    n_tiles = xs.shape[0] // (TM_MOE * ROW_CHUNKS)
    any_spec = pl.BlockSpec(memory_space=pl.ANY)
    tile = (TM_MOE * ROW_CHUNKS, LANES)
    return pl.pallas_call(
        functools.partial(_moe_kernel, layer=layer),
        grid_spec=pltpu.PrefetchScalarGridSpec(
            num_scalar_prefetch=5, grid=(n_tiles,),
            in_specs=[pl.BlockSpec(tile, lambda g, te, na, *_: (jnp.minimum(g, na[0] - 1), 0)),
                      any_spec, any_spec, any_spec],
            out_specs=pl.BlockSpec((TM_MOE, d), lambda g, *_: (g, 0)),
            scratch_shapes=[pltpu.VMEM((2, d, EXPERT_FF), F32), pltpu.VMEM((2, d, EXPERT_FF), F32),
                            pltpu.VMEM((2, EXPERT_FF, d), F32), pltpu.SemaphoreType.DMA((2,))]),
        out_shape=jax.ShapeDtypeStruct((n_tiles * TM_MOE, d), F32),
        compiler_params=_cparams(("arbitrary",)),
        name="moe_ffn",
    )(plan["tile_expert"], plan["n_active"], plan["first"], plan["ord"], plan["next"],
      xs, w_gate, w_up, w_down)


def _moe_sample_kernel(ue_ref, nu_ref, x_ref, info_ref, y_ref, g_ref, wg_ref, wu_ref, wd_ref,
                       xo_ref, xno_ref, acc_ref):
    s = pl.program_id(0)

    @pl.when(s == 0)
    def _():
        acc_ref[...] = y_ref[...]

    @pl.when(s < nu_ref[0])
    def _():
        x = _load_row_tiles(x_ref, y_ref.shape[0])
        a = _dot3(x, wg_ref[...])
        u = _dot3(x, wu_ref[...])
        out = _dot3((a * jax.nn.sigmoid(a)) * u, wd_ref[...])
        e = ue_ref[s].astype(F32)
        w = (jnp.where(info_ref[:, 0:1] == e, info_ref[:, 4:5], 0.0)
             + jnp.where(info_ref[:, 1:2] == e, info_ref[:, 5:6], 0.0))
        acc_ref[...] += w * out

    @pl.when(s == pl.num_programs(0) - 1)
    def _():
        xo_ref[...] = acc_ref[...]
        xno_ref[...] = _rms(acc_ref[...], g_ref[...])


def _moe_sample(xn2, info, y, g, w_gate, w_up, w_down, layer, row0):
    nb, d = y.shape
    blk0 = row0 // nb
    e_s = info[row0:, 0:2].astype(jnp.int32).reshape(-1)
    used = jnp.zeros((N_EXPERTS,), jnp.int32).at[e_s].set(1)
    n_used = jnp.sum(used)
    order = jnp.argsort(1 - used, stable=True).astype(jnp.int32)
    used_list = jnp.where(jnp.arange(N_EXPERTS) < n_used, order, order[n_used - 1])
    rows = lambda s, ue, nu: (blk0, 0)
    fixed = lambda s, ue, nu: (0, 0)
    wsel = lambda s, ue, nu: (layer, ue[s], 0, 0)
    return pl.pallas_call(
        _moe_sample_kernel,
        grid_spec=pltpu.PrefetchScalarGridSpec(
            num_scalar_prefetch=2, grid=(N_EXPERTS,),
            in_specs=[pl.BlockSpec((nb * ROW_CHUNKS, LANES), rows),
                      pl.BlockSpec((nb, ROUTER_LANES), rows),
                      pl.BlockSpec((nb, d), fixed),
                      pl.BlockSpec((1, d), fixed),
                      pl.BlockSpec((None, None, d, EXPERT_FF), wsel),
                      pl.BlockSpec((None, None, d, EXPERT_FF), wsel),
                      pl.BlockSpec((None, None, EXPERT_FF, d), wsel)],
            out_specs=[pl.BlockSpec((nb, d), fixed), pl.BlockSpec((nb, d), fixed)],
            scratch_shapes=[pltpu.VMEM((nb, d), F32)]),
        out_shape=[jax.ShapeDtypeStruct((nb, d), F32), jax.ShapeDtypeStruct((nb, d), F32)],
        compiler_params=_cparams(("arbitrary",)),
        name="moe_sample",
    )(used_list, n_used.reshape(1).astype(jnp.int32), xn2, info, y, g.reshape(1, d),
      w_gate, w_up, w_down)


def _combine_kernel(pos_ref, y_ref, info_ref, g_ref, ys_hbm, *rest, rows, write_x, blk0, n_total):
    if write_x:
        xo_ref, xno_ref, buf0, buf1, sem = rest
    else:
        xno_ref, buf0, buf1, sem = rest
    i = pl.program_id(0)

    def row(src_row, buf, slot, r):
        return pltpu.make_async_copy(ys_hbm.at[pl.ds(src_row, 1)], buf.at[slot, pl.ds(r, 1)],
                                     sem.at[slot])

    def fetch(step, slot):
        base = (blk0 + step) * rows

        def body(r, c):
            t = base + r
            row(pos_ref[t], buf0, slot, r).start(priority=0)
            row(pos_ref[n_total + t], buf1, slot, r).start(priority=1)
            return c
        lax.fori_loop(0, rows, body, 0, unroll=8)

    @pl.when(i == 0)
    def _():
        fetch(0, 0)

    @pl.when(i + 1 < pl.num_programs(0))
    def _():
        fetch(i + 1, (i + 1) % 2)

    slot = i % 2
    for r in range(rows):
        row(0, buf0, slot, r).wait()
        row(0, buf1, slot, r).wait()
    xnew = y_ref[...] + info_ref[:, 4:5] * buf0[slot] + info_ref[:, 5:6] * buf1[slot]
    if write_x:
        xo_ref[...] = xnew
    xno_ref[...] = _rms(xnew, g_ref[...]).astype(xno_ref.dtype)


def _combine(pos, y, info, g, ys, norm_dtype, write_x, row0):
    n, d = y.shape
    rows = min(TM_COMBINE, n)
    assert row0 % rows == 0
    blk0 = row0 // rows
    tile = pl.BlockSpec((rows, d), lambda i, pos: (i, 0))
    out_specs = [tile]
    out_shape = [jax.ShapeDtypeStruct((n, d), norm_dtype)]
    if write_x:
        out_specs = [tile] + out_specs
        out_shape = [jax.ShapeDtypeStruct((n, d), F32)] + out_shape
    return pl.pallas_call(
        functools.partial(_combine_kernel, rows=rows, write_x=write_x, blk0=blk0,
                          n_total=pos.shape[0] // 2),
        grid_spec=pltpu.PrefetchScalarGridSpec(
            num_scalar_prefetch=1, grid=(n // rows,),
            in_specs=[tile,
                      pl.BlockSpec((rows, ROUTER_LANES), lambda i, pos: (blk0 + i, 0)),
                      pl.BlockSpec((1, d), lambda i, pos: (0, 0)),
                      pl.BlockSpec(memory_space=pl.ANY)],
            out_specs=out_specs,
            scratch_shapes=[pltpu.VMEM((2, rows, d), F32), pltpu.VMEM((2, rows, d), F32),
                            pltpu.SemaphoreType.DMA((2,))]),
        out_shape=out_shape,
        compiler_params=_cparams(("arbitrary",)),
        name="combine",
    )(pos, y, info, g.reshape(1, d), ys)


def _plan(idx, cnt, n_tiles):
    count = cnt[0, :N_EXPERTS].astype(jnp.int32)
    tiles = (count + TM_MOE - 1) // TM_MOE
    tile_end = jnp.cumsum(tiles)
    row_off = (tile_end - tiles) * TM_MOE
    n_active = tile_end[-1]
    expert = idx[0:2].astype(jnp.int32)
    ids3 = jnp.arange(N_EXPERTS, dtype=jnp.int32)[:, None, None]
    base = jnp.sum(jnp.where(expert[None] == ids3, row_off[:, None, None], 0), axis=0)
    pos = (base + idx[2:4].astype(jnp.int32)).reshape(-1)
    step = jnp.arange(n_tiles, dtype=jnp.int32)
    g = jnp.minimum(step, n_active - 1)
    tile_expert = jnp.sum((g[:, None] >= tile_end[None, :]).astype(jnp.int32), axis=1)
    prev = jnp.concatenate([jnp.full((1,), -1, jnp.int32), tile_expert[:-1]])
    first = ((step < n_active) & (tile_expert != prev)).astype(jnp.int32)
    ordinal = jnp.cumsum(first) - 1
    ids = jnp.arange(N_EXPERTS, dtype=jnp.int32)
    used_at_or_after = lax.cummin(jnp.where(tiles > 0, ids, N_EXPERTS), reverse=True)
    used_after = jnp.concatenate([used_at_or_after[1:], jnp.full((1,), N_EXPERTS, jnp.int32)])
    next_used = jnp.where(used_after == N_EXPERTS, -1, used_after)
    i32 = lambda a: a.astype(jnp.int32)
    return pos, dict(tile_expert=i32(tile_expert), n_active=i32(n_active.reshape(1)),
                     first=first, ord=i32(ordinal), next=i32(next_used[tile_expert]))


def kernel(x_prompt, x_sample, state_conv, cache_win_k, cache_win_v, state_pool, norm1_g, w_in,
           conv_w, attn_sinks, sgu_ln_g, sgu_ln_b, sgu_w, sgu_b, pool_w, pool_scale, w_branch, w_o,
           norm2_g, router_group_w, router_group_b, router_expert_w, router_expert_b, moe_w_gate,
           moe_w_up, moe_w_down, final_norm_g):
    batch, seq, d = x_prompt.shape
    assert seq % TB == 0 and seq >= 2 * TB and d == D_MODEL
    nb = x_sample.shape[0]
    depth = w_in.shape[0]
    n_p = batch * seq
    kv_w = N_KV * HEAD_DIM
    n_slots = 2 * (n_p + nb)
    n_tiles = (n_slots + N_EXPERTS * (TM_MOE - 1) + TM_MOE - 1) // TM_MOE

    xp = x_prompt.reshape(n_p, d)
    xs_ = x_sample.reshape(nb, d)
    xn_p = _rmsnorm(xp, norm1_g[0], BF16)
    xn_s = _rmsnorm(xs_, norm1_g[0], F32)

    pad = ROUTER_LANES - N_EXPERTS - N_EXP_GROUPS
    conv_p, k_p, v_p, pool_p = [], [], [], []
    conv_s, k_s, v_s, pool_s, chunk_s = [], [], [], [], []
    y_prompt = y_sample = None
    for l in range(depth):
        lw = dict(attn_sinks=attn_sinks[l], conv_w=conv_w[l], sgu_ln_g=sgu_ln_g[l],
                  sgu_ln_b=sgu_ln_b[l], sgu_w=sgu_w[l], sgu_b=sgu_b[l], pool_w=pool_w[l],
                  pool_scale=pool_scale[l])
        w_router = jnp.concatenate(
            [router_expert_w[l], router_group_w[l], jnp.zeros((d, pad), F32)], axis=1)
        b_router = jnp.concatenate(
            [router_expert_b[l], router_group_b[l], jnp.zeros((pad,), F32)]).reshape(1, -1)
        last = l == depth - 1
        g_next = final_norm_g if last else norm1_g[l + 1]

        zs_p = _in_proj(xn_p, w_in, l, False)
        outs_p, c_st, k_st, v_st, p_st = _branch_prompt(zs_p, batch, seq, lw)
        conv_p.append(c_st)
        k_p.append(k_st.reshape(batch, WINDOW, N_KV, HEAD_DIM))
        v_p.append(v_st.reshape(batch, WINDOW, N_KV, HEAD_DIM))
        pool_p.append(p_st)
        merged_p, rows_sorted = _merge(xn_p, outs_p, w_in, w_branch, l, False, zero_tiles=n_tiles)
        yp = _out_proj(merged_p, w_o, xp, l, False)

        zs_s = _in_proj(xn_s, w_in, l, True)
        outs_s, c_st, k_st, v_st, p_st, cv = _branch_sample(
            zs_s, lw, jnp.swapaxes(state_conv[l], 0, 1),
            cache_win_k[l].reshape(nb, WINDOW, kv_w), cache_win_v[l].reshape(nb, WINDOW, kv_w),
            jnp.swapaxes(state_pool[l], 0, 1))
        conv_s.append(jnp.swapaxes(c_st, 0, 1))
        k_s.append(k_st.reshape(nb, WINDOW, N_KV, HEAD_DIM))
        v_s.append(v_st.reshape(nb, WINDOW, N_KV, HEAD_DIM))
        pool_s.append(jnp.swapaxes(p_st, 0, 1))
        chunk_s.append(cv.reshape(nb, 1, BRANCH_W))

        xn_hi, outs_hi, x_hi = xn_s, outs_s, xs_
        if not last:
            x_t = xp.reshape(batch, seq, d)[:, seq - 2 * TB:].reshape(batch * 2 * TB, d)
            xn_t = _rmsnorm(x_t, norm1_g[l], F32)
            zs_t = _in_proj(xn_t, w_in, l, True)
            outs_t = _branch_prompt(zs_t, batch, seq, lw, hi=True, t_start=seq // TB - 2)[0]
            tail = lambda a: a.reshape(batch, 2 * TB, a.shape[-1])[:, 2 * TB - TAIL_EXACT:].reshape(
                batch * TAIL_EXACT, -1)
            outs_t = jnp.stack([tail(outs_t[b]) for b in range(N_BRANCH)])
            xn_hi = jnp.concatenate([tail(xn_t), xn_s])
            outs_hi = jnp.concatenate([outs_t, outs_s], axis=1)
            x_hi = jnp.concatenate([tail(x_t), xs_])
        y_hi = _out_proj(_merge(xn_hi, outs_hi, w_in, w_branch, l, True), w_o, x_hi, l, True)
        ys_ = y_hi[y_hi.shape[0] - nb:]
        if not last:
            yp = _put_last_rows(yp, y_hi[:batch * TAIL_EXACT], batch, seq)

        xn2, info, idx, cnt = _router(yp, ys_, norm2_g[l], w_router, b_router)
        pos, plan = _plan(idx, cnt, n_tiles)
        rows_sorted = _dispatch(pos, xn2, rows_sorted)
        ffn = _moe(plan, rows_sorted, moe_w_gate, moe_w_up, moe_w_down, l)
        if last:
            (y_prompt,) = _combine(pos, yp, info, g_next, ffn, F32, False, 0)
            (y_sample,) = _combine(pos, ys_, info, g_next, ffn, F32, False, n_p)
        else:
            xp, xn_p = _combine(pos, yp, info, g_next, ffn, BF16, True, 0)
            xs_, xn_s = _moe_sample(xn2, info, ys_, g_next, moe_w_gate, moe_w_up, moe_w_down, l, n_p)

    return (y_prompt.reshape(batch, seq, d), y_sample.reshape(nb, 1, d),
            jnp.stack(conv_p), jnp.stack(k_p), jnp.stack(v_p), jnp.stack(pool_p),
            jnp.stack(conv_s), jnp.stack(k_s), jnp.stack(v_s), jnp.stack(pool_s),
            jnp.stack(chunk_s))
```
